```python
import jax, jax.numpy as jnp
from jax import lax
import numpy as np

D_MODEL = 2048
BATCH = 4
SEQ = 4096
DEPTH = 1

N_ATTN_HEADS = 8
HEAD_DIM = 128
ATTN_WIDTH = N_ATTN_HEADS * HEAD_DIM
POOL_WIDTH = D_MODEL - ATTN_WIDTH
POOL_WINDOWS = (2, 4, 8, 16)
N_POOL_GROUPS = len(POOL_WINDOWS)
POOL_GROUP_DIM = POOL_WIDTH // N_POOL_GROUPS
MIX_WIDTH = ATTN_WIDTH + POOL_WIDTH
IN_PROJ_WIDTH = 3 * ATTN_WIDTH + POOL_WIDTH
MOBA_BLOCK = 256
MOBA_TOPK = 3
Q_CHUNK = 32
ROT_DIM = HEAD_DIM // 4
ROPE_THETA = 500000.0
D_FF = 5632
EPS = 1e-6
NEG_INF = -1e30
MAX_POS_OFFSET = 1024

kernel_name = "hymba_moba_pool_macaron_layer"


def rmsnorm(x, gain):
    xf = x.astype(jnp.float32)
    inv = lax.rsqrt(jnp.mean(xf * xf, axis=-1, keepdims=True) + EPS)
    return (xf * inv * gain.astype(jnp.float32)).astype(x.dtype)


def swiglu(h, w_gate, w_up, w_down):
    return (jax.nn.silu(h @ w_gate) * (h @ w_up)) @ w_down


def partial_rope(x, positions):
    inv_freq = ROPE_THETA ** (-jnp.arange(0, ROT_DIM, 2, dtype=jnp.float32) / ROT_DIM)
    ang = positions.astype(jnp.float32)[:, None, :, None] * inv_freq
    cos, sin = jnp.cos(ang), jnp.sin(ang)
    xr = x[..., :ROT_DIM].astype(jnp.float32)
    x1, x2 = xr[..., :ROT_DIM // 2], xr[..., ROT_DIM // 2:]
    rot = jnp.concatenate([x1 * cos - x2 * sin, x2 * cos + x1 * sin], axis=-1).astype(x.dtype)
    return jnp.concatenate([rot, x[..., ROT_DIM:]], axis=-1)


def moba_attention(q, k, v):
    B, H, S, Dh = q.shape
    nb = -(-S // MOBA_BLOCK)
    s_pad = nb * MOBA_BLOCK
    pad = ((0, 0), (0, 0), (0, s_pad - S), (0, 0))
    kb = jnp.pad(k, pad).reshape(B, H, nb, MOBA_BLOCK, Dh)
    vb = jnp.pad(v, pad).reshape(B, H, nb, MOBA_BLOCK, Dh)
    k_mean = jnp.mean(kb.astype(jnp.float32), axis=3)
    topk = min(MOBA_TOPK, nb)
    scale = HEAD_DIM ** -0.5
    n_chunks = S // Q_CHUNK
    q_chunks = q.reshape(B, H, n_chunks, Q_CHUNK, Dh).transpose(2, 0, 1, 3, 4)
    b_idx = jnp.arange(B)[:, None, None, None]
    h_idx = jnp.arange(H)[None, :, None, None]
    block_ids = jnp.arange(nb)

    def chunk_attn(args):
        c, qc = args
        q_start = c * Q_CHUNK
        qblk = q_start // MOBA_BLOCK
        gate = jnp.einsum('bhqd,bhnd->bhqn', qc.astype(jnp.float32), k_mean)
        gate = jnp.where(block_ids < qblk, gate, NEG_INF)
        _, sel = lax.top_k(gate, topk)
        sel_valid = sel < qblk
        k_sel = kb[b_idx, h_idx, sel]
        v_sel = vb[b_idx, h_idx, sel]
        s_sel = jnp.einsum('bhqd,bhqjpd->bhqjp', qc, k_sel).astype(jnp.float32) * scale
        s_sel = jnp.where(sel_valid[..., None], s_sel, NEG_INF).reshape(B, H, Q_CHUNK, topk * MOBA_BLOCK)
        k_own = lax.dynamic_index_in_dim(kb, qblk, axis=2, keepdims=False)
        v_own = lax.dynamic_index_in_dim(vb, qblk, axis=2, keepdims=False)
        s_own = jnp.einsum('bhqd,bhpd->bhqp', qc, k_own).astype(jnp.float32) * scale
        q_pos = q_start + jnp.arange(Q_CHUNK)
        k_pos = qblk * MOBA_BLOCK + jnp.arange(MOBA_BLOCK)
        s_own = jnp.where(k_pos[None, :] <= q_pos[:, None], s_own, NEG_INF)
        p = jax.nn.softmax(jnp.concatenate([s_sel, s_own], axis=-1), axis=-1)
        p_sel = p[..., :topk * MOBA_BLOCK].reshape(B, H, Q_CHUNK, topk, MOBA_BLOCK).astype(v.dtype)
        p_own = p[..., topk * MOBA_BLOCK:].astype(v.dtype)
        return (jnp.einsum('bhqjp,bhqjpd->bhqd', p_sel, v_sel)
                + jnp.einsum('bhqp,bhpd->bhqd', p_own, v_own))

    out = lax.map(chunk_attn, (jnp.arange(n_chunks), q_chunks))
    return out.transpose(1, 2, 0, 3, 4).reshape(B, H, S, Dh)


def multiscale_pool(u, pool_w, pool_scale):
    B, S, _ = u.shape
    ug = u.reshape(B, S, N_POOL_GROUPS, POOL_GROUP_DIM).astype(jnp.float32)
    csum = jnp.pad(jnp.cumsum(ug, axis=1), ((0, 0), (1, 0), (0, 0), (0, 0)))
    t = jnp.arange(S)
    windows = jnp.array(POOL_WINDOWS, dtype=jnp.int32)
    start = jnp.maximum(t[:, None] + 1 - windows[None, :], 0)
    count = (t[:, None] + 1 - start).astype(jnp.float32)
    g_idx = jnp.arange(N_POOL_GROUPS)[None, :]
    window_sum = csum[:, 1:] - csum[:, start, g_idx]
    pooled = (window_sum / count[None, :, :, None] - ug).astype(u.dtype)
    y = jnp.einsum('bsgc,gcd->bsgd', pooled, pool_w).reshape(B, S, POOL_WIDTH)
    return y * pool_scale


def setup_inputs(seed: int = 0) -> dict:
    key = jax.random.key(seed)
    ks = jax.random.split(key, 20)
    f32 = jnp.float32

    def normal(k, shape, fan_in):
        return jax.random.normal(k, shape, f32) * (fan_in ** -0.5)

    def gain(k, shape):
        return 1.0 + 0.02 * jax.random.normal(k, shape, f32)

    x = jax.random.normal(ks[0], (BATCH, SEQ, D_MODEL), f32)
    offsets = jax.random.randint(ks[1], (BATCH, 1), 0, MAX_POS_OFFSET, dtype=jnp.int32)
    positions = (jnp.arange(SEQ, dtype=jnp.int32)[None, :] + offsets).astype(jnp.int32)
    return {
        "x": x,
        "positions": positions,
        "norm_ffn1": gain(ks[2], (DEPTH, D_MODEL)),
        "w1_gate": normal(ks[3], (DEPTH, D_MODEL, D_FF), D_MODEL),
        "w1_up": normal(ks[4], (DEPTH, D_MODEL, D_FF), D_MODEL),
        "w1_down": normal(ks[5], (DEPTH, D_FF, D_MODEL), D_FF),
        "norm_mix": gain(ks[6], (DEPTH, D_MODEL)),
        "w_in": normal(ks[7], (DEPTH, D_MODEL, IN_PROJ_WIDTH), D_MODEL),
        "pool_w": normal(ks[8], (DEPTH, N_POOL_GROUPS, POOL_GROUP_DIM, POOL_GROUP_DIM), POOL_GROUP_DIM),
        "pool_scale": 1.0 + 0.1 * jax.random.normal(ks[9], (DEPTH, POOL_WIDTH), f32),
        "w_out": normal(ks[10], (DEPTH, MIX_WIDTH, D_MODEL), MIX_WIDTH),
        "norm_ffn2": gain(ks[11], (DEPTH, D_MODEL)),
        "w2_gate": normal(ks[12], (DEPTH, D_MODEL, D_FF), D_MODEL),
        "w2_up": normal(ks[13], (DEPTH, D_MODEL, D_FF), D_MODEL),
        "w2_down": normal(ks[14], (DEPTH, D_FF, D_MODEL), D_FF),
        "norm_final": gain(ks[15], (D_MODEL,)),
    }


def reference(x, positions, norm_ffn1, w1_gate, w1_up, w1_down, norm_mix, w_in, pool_w,
              pool_scale, w_out, norm_ffn2, w2_gate, w2_up, w2_down, norm_final):
    B, S, _ = x.shape
    for l in range(DEPTH):
        x = x + 0.5 * swiglu(rmsnorm(x, norm_ffn1[l]), w1_gate[l], w1_up[l], w1_down[l])
        h = rmsnorm(x, norm_mix[l])
        proj = h @ w_in[l]
        q, k, v, u = jnp.split(proj, [ATTN_WIDTH, 2 * ATTN_WIDTH, 3 * ATTN_WIDTH], axis=-1)
        to_heads = lambda t: t.reshape(B, S, N_ATTN_HEADS, HEAD_DIM).transpose(0, 2, 1, 3)
        q = partial_rope(to_heads(q), positions)
        k = partial_rope(to_heads(k), positions)
        attn = moba_attention(q, k, to_heads(v))
        attn = attn.transpose(0, 2, 1, 3).reshape(B, S, ATTN_WIDTH)
        pooled = multiscale_pool(u, pool_w[l], pool_scale[l])
        x = x + jnp.concatenate([attn, pooled], axis=-1) @ w_out[l]
        x = x + 0.5 * swiglu(rmsnorm(x, norm_ffn2[l]), w2_gate[l], w2_up[l], w2_down[l])
    return rmsnorm(x, norm_final)
```

```python
import functools

import jax
import jax.numpy as jnp
from jax import lax
from jax.experimental import pallas as pl
from jax.experimental.pallas import tpu as pltpu

N_ATTN_HEADS = 8
HEAD_DIM = 128
ATTN_WIDTH = N_ATTN_HEADS * HEAD_DIM
POOL_WINDOWS = (2, 4, 8, 16)
POOL_GROUP_DIM = 256
POOL_WIDTH = POOL_GROUP_DIM * len(POOL_WINDOWS)
MOBA_BLOCK = 256
MOBA_TOPK = 3
ROT_DIM = HEAD_DIM // 4
ROT_HALF = ROT_DIM // 2
ROPE_THETA = 500000.0
EPS = 1e-6
NEG_INF = -1e30
POOL_HALO = 16

V7X_VMEM_LIMIT_BYTES = 58 * 1024 * 1024

FFN_TOKEN_TILE = 1024
FFN_FF_TILE = 512
PROJ_TOKEN_TILE = 512
MIX_TOKEN_TILE = 512

_BF16 = jnp.bfloat16
_F32 = jnp.float32


def _rms(x, gain):
    inv = lax.rsqrt(jnp.mean(x * x, axis=-1, keepdims=True) + EPS)
    return x * inv * gain


def _dot(a, b):
    return jnp.dot(a, b, preferred_element_type=_F32)


def _dot_nt(a, b):
    return lax.dot_general(a, b, (((1,), (1,)), ((), ())), preferred_element_type=_F32)


def _resident(block_shape, index_map):
    return pl.BlockSpec(block_shape, index_map, pipeline_mode=pl.Buffered(1))


def _ffn_kernel(x_ref, gain_ref, wg_ref, wu_ref, wd_ref, fgain_ref, out_ref, h_scr, *, final_norm):
    f = pl.program_id(1)

    @pl.when(f == 0)
    def _():
        x = x_ref[...]
        h_scr[...] = _rms(x, gain_ref[...]).astype(_BF16)
        out_ref[...] = x

    h = h_scr[...]
    g = _dot(h, wg_ref[...])
    u = _dot(h, wu_ref[...])
    a = (g * jax.nn.sigmoid(g) * u).astype(_BF16)
    out_ref[...] += 0.5 * _dot(a, wd_ref[...])

    if final_norm:
        @pl.when(f == pl.num_programs(1) - 1)
        def _():
            out_ref[...] = _rms(out_ref[...], fgain_ref[...])


def _ffn(x2d, gain, wg, wu, wd, fgain, *, final_norm, name):
    t, d = x2d.shape
    d_ff = wg.shape[1]
    tm, tf = FFN_TOKEN_TILE, FFN_FF_TILE
    assert t % tm == 0 and d_ff % tf == 0
    return pl.pallas_call(
        functools.partial(_ffn_kernel, final_norm=final_norm),
        grid=(t // tm, d_ff // tf),
        in_specs=[
            pl.BlockSpec((tm, d), lambda i, f: (i, 0)),
            pl.BlockSpec((1, d), lambda i, f: (0, 0)),
            pl.BlockSpec((d, tf), lambda i, f: (0, f)),
            pl.BlockSpec((d, tf), lambda i, f: (0, f)),
            pl.BlockSpec((tf, d), lambda i, f: (f, 0)),
            pl.BlockSpec((1, d), lambda i, f: (0, 0)),
        ],
        out_specs=pl.BlockSpec((tm, d), lambda i, f: (i, 0)),
        out_shape=jax.ShapeDtypeStruct((t, d), _F32),
        scratch_shapes=[pltpu.VMEM((tm, d), _BF16)],
        compiler_params=pltpu.CompilerParams(
            dimension_semantics=("parallel", "arbitrary"),
            vmem_limit_bytes=V7X_VMEM_LIMIT_BYTES),
        name=name,
    )(x2d, gain, wg, wu, wd, fgain)


def _in_proj_kernel(x_ref, gain_ref, pos_col_ref, pos_row_ref, freq_row_ref, freq_col_ref,
                    wqt_ref, wk_ref, wvt_ref, wu_ref,
                    qt_ref, k_ref, vt_ref, u_ref, kmean_ref):
    tm = x_ref.shape[1]
    h = _rms(x_ref[0], gain_ref[...]).astype(_BF16)

    ang_t = freq_col_ref[...] * pos_row_ref[0].astype(_F32)
    cos_t, sin_t = jnp.cos(ang_t), jnp.sin(ang_t)

    qt = _dot_nt(wqt_ref[...], h)
    for hh in range(N_ATTN_HEADS):
        base = hh * HEAD_DIM
        x1 = qt[base:base + ROT_HALF]
        x2 = qt[base + ROT_HALF:base + ROT_DIM]
        qt_ref[0, hh, 0:ROT_HALF, :] = (x1 * cos_t - x2 * sin_t).astype(_BF16)
        qt_ref[0, hh, ROT_HALF:ROT_DIM, :] = (x2 * cos_t + x1 * sin_t).astype(_BF16)
        qt_ref[0, hh, ROT_DIM:, :] = qt[base + ROT_DIM:base + HEAD_DIM].astype(_BF16)

    vt = _dot_nt(wvt_ref[...], h)
    for hh in range(N_ATTN_HEADS):
        vt_ref[0, hh] = vt[hh * HEAD_DIM:(hh + 1) * HEAD_DIM].astype(_BF16)

    u_ref[0] = _dot(h, wu_ref[...])

    ang = pos_col_ref[0].astype(_F32) * freq_row_ref[...]
    lane = lax.broadcasted_iota(jnp.int32, (1, HEAD_DIM), 1)
    cos_n = jnp.cos(ang)
    sin_n = jnp.where(lane < ROT_HALF, -jnp.sin(ang), jnp.sin(ang))
    k = _dot(h, wk_ref[...])
    for hh in range(N_ATTN_HEADS):
        kh = k[:, hh * HEAD_DIM:(hh + 1) * HEAD_DIM]
        partner = jnp.where(lane < ROT_HALF,
                            pltpu.roll(kh, HEAD_DIM - ROT_HALF, 1),
                            pltpu.roll(kh, ROT_HALF, 1))
        kr = kh * cos_n + partner * sin_n
        k_ref[0, hh] = kr.astype(_BF16)
        for j in range(tm // MOBA_BLOCK):
            kmean_ref[0, j, :, hh * HEAD_DIM:(hh + 1) * HEAD_DIM] = jnp.mean(
                kr[j * MOBA_BLOCK:(j + 1) * MOBA_BLOCK], axis=0, keepdims=True)


def _in_proj(x1, gain, positions, freq_row, freq_col, wqt, wk, wvt, wu):
    b, s, d = x1.shape
    tm = PROJ_TOKEN_TILE
    assert s % tm == 0 and tm % MOBA_BLOCK == 0
    nb = s // MOBA_BLOCK
    bpt = tm // MOBA_BLOCK
    pos_col = positions.reshape(b, s, 1)
    pos_row = positions.reshape(b, 1, s)
    const2 = lambda bi, i: (0, 0)
    return pl.pallas_call(
        _in_proj_kernel,
        grid=(b, s // tm),
        in_specs=[
            pl.BlockSpec((1, tm, d), lambda bi, i: (bi, i, 0)),
            pl.BlockSpec((1, d), const2),
            pl.BlockSpec((1, tm, 1), lambda bi, i: (bi, i, 0)),
            pl.BlockSpec((1, 1, tm), lambda bi, i: (bi, 0, i)),
            pl.BlockSpec((1, HEAD_DIM), const2),
            pl.BlockSpec((ROT_HALF, 1), const2),
            _resident((ATTN_WIDTH, d), const2),
            _resident((d, ATTN_WIDTH), const2),
            _resident((ATTN_WIDTH, d), const2),
            _resident((d, POOL_WIDTH), const2),
        ],
        out_specs=[
            pl.BlockSpec((1, N_ATTN_HEADS, HEAD_DIM, tm), lambda bi, i: (bi, 0, 0, i)),
            pl.BlockSpec((1, N_ATTN_HEADS, tm, HEAD_DIM), lambda bi, i: (bi, 0, i, 0)),
            pl.BlockSpec((1, N_ATTN_HEADS, HEAD_DIM, tm), lambda bi, i: (bi, 0, 0, i)),
            pl.BlockSpec((1, tm, POOL_WIDTH), lambda bi, i: (bi, i, 0)),
            pl.BlockSpec((1, bpt, 1, ATTN_WIDTH), lambda bi, i: (bi, i, 0, 0)),
        ],
        out_shape=[
            jax.ShapeDtypeStruct((b, N_ATTN_HEADS, HEAD_DIM, s), _BF16),
            jax.ShapeDtypeStruct((b, N_ATTN_HEADS, s, HEAD_DIM), _BF16),
            jax.ShapeDtypeStruct((b, N_ATTN_HEADS, HEAD_DIM, s), _BF16),
            jax.ShapeDtypeStruct((b, s, POOL_WIDTH), _F32),
            jax.ShapeDtypeStruct((b, nb, 1, ATTN_WIDTH), _F32),
        ],
        compiler_params=pltpu.CompilerParams(
            dimension_semantics=("parallel", "parallel"),
            vmem_limit_bytes=V7X_VMEM_LIMIT_BYTES),
        name="in_proj",
    )(x1, gain, pos_col, pos_row, freq_row, freq_col, wqt, wk, wvt, wu)


def _moba_kernel(qt_ref, k_ref, vt_ref, kmean_ref, out_ref, cap_scr, *, nb):
    i = pl.program_id(2)
    blk = MOBA_BLOCK
    scale = HEAD_DIM ** -0.5
    qt = qt_ref[0, 0]

    km = kmean_ref[0]
    km_hi = km.astype(_BF16)
    km_lo = (km - km_hi.astype(_F32)).astype(_BF16)
    gate = _dot(km_hi, qt) + _dot(km_lo, qt)
    kb_id = lax.broadcasted_iota(jnp.int32, (nb, blk), 0)
    past = kb_id < i
    gate = jnp.where(past, gate, NEG_INF)
    rank = jnp.zeros((nb, blk), jnp.int32)
    for m in range(nb):
        gm = gate[m:m + 1, :]
        beats = (gm > gate) | ((gm == gate) & (kb_id > m))
        rank = rank + beats.astype(jnp.int32)
    selected = (rank < MOBA_TOPK) & past
    cap_scr[...] = jnp.where(selected, jnp.inf, NEG_INF).astype(_F32)

    def scores(n):
        kblk = k_ref[0, 0, pl.ds(pl.multiple_of(n * blk, blk), blk), :]
        return _dot(kblk, qt) * scale

    def accumulate(n, s, m_run, l_run, acc):
        m_new = jnp.maximum(m_run, jnp.max(s, axis=0, keepdims=True))
        alpha = jnp.exp(m_run - m_new)
        p = jnp.exp(s - m_new)
        l_new = alpha * l_run + jnp.sum(p, axis=0, keepdims=True)
        vblk = vt_ref[0, 0, :, pl.ds(pl.multiple_of(n * blk, blk), blk)]
        acc_new = alpha * acc + _dot(vblk, p.astype(_BF16))
        return m_new, l_new, acc_new

    key_pos = lax.broadcasted_iota(jnp.int32, (blk, blk), 0)
    qry_pos = lax.broadcasted_iota(jnp.int32, (blk, blk), 1)
    s_own = jnp.where(key_pos <= qry_pos, scores(i), NEG_INF)
    m0 = jnp.full((1, blk), NEG_INF, _F32)
    l0 = jnp.zeros((1, blk), _F32)
    acc0 = jnp.zeros((HEAD_DIM, blk), _F32)
    carry = accumulate(i, s_own, m0, l0, acc0)

    def body(n, carry):
        s = jnp.minimum(scores(n), cap_scr[pl.ds(n, 1), :])
        return accumulate(n, s, *carry)

    _, l_fin, acc = lax.fori_loop(0, i, body, carry)
    out_ref[0] = jnp.transpose(acc / l_fin).astype(out_ref.dtype)


def _moba(qt, k, vt, kmean):
    b, nh, hd, s = qt.shape
    nb = s // MOBA_BLOCK
    return pl.pallas_call(
        functools.partial(_moba_kernel, nb=nb),
        grid=(b, nh, nb),
        in_specs=[
            pl.BlockSpec((1, 1, hd, MOBA_BLOCK), lambda bi, h, i: (bi, h, 0, i)),
            pl.BlockSpec((1, 1, s, hd), lambda bi, h, i: (bi, h, 0, 0)),
            pl.BlockSpec((1, 1, hd, s), lambda bi, h, i: (bi, h, 0, 0)),
            pl.BlockSpec((1, nb, hd), lambda bi, h, i: (bi, 0, h)),
        ],
        out_specs=pl.BlockSpec((1, MOBA_BLOCK, hd), lambda bi, h, i: (bi, i, h)),
        out_shape=jax.ShapeDtypeStruct((b, s, nh * hd), _BF16),
        scratch_shapes=[pltpu.VMEM((nb, MOBA_BLOCK), _F32)],
        compiler_params=pltpu.CompilerParams(
            dimension_semantics=("parallel", "parallel", "arbitrary"),
            vmem_limit_bytes=V7X_VMEM_LIMIT_BYTES),
        name="moba",
    )(qt, k, vt, kmean)


def _mix_out_kernel(x_ref, attn_ref, u_ref, halo_ref, pw_ref, ps_ref, wo_ref, out_ref, ubuf):
    tm = x_ref.shape[1]
    si = pl.program_id(1)
    ubuf[0:POOL_HALO, :] = jnp.where(si > 0, halo_ref[0], 0.0)
    ubuf[POOL_HALO:, :] = u_ref[0]

    t = si * tm + lax.broadcasted_iota(jnp.int32, (tm, 1), 0)
    mix = _dot(attn_ref[0], wo_ref[0:ATTN_WIDTH, :])
    for g, w in enumerate(POOL_WINDOWS):
        cols = slice(g * POOL_GROUP_DIM, (g + 1) * POOL_GROUP_DIM)
        own = ubuf[POOL_HALO:, cols]
        wsum = own
        for dlt in range(1, w):
            wsum = wsum + ubuf[POOL_HALO - dlt:POOL_HALO - dlt + tm, cols]
        count = jnp.minimum(t + 1, w).astype(_F32)
        pooled = wsum / count - own
        y = _dot(pooled.astype(_BF16), pw_ref[g]) * ps_ref[:, cols]
        row0 = ATTN_WIDTH + g * POOL_GROUP_DIM
        mix = mix + _dot(y.astype(_BF16), wo_ref[row0:row0 + POOL_GROUP_DIM, :])
    out_ref[0] = x_ref[0] + mix


def _mix_out(x1, attn, u, pool_w, pool_scale, w_out):
    b, s, d = x1.shape
    tm = MIX_TOKEN_TILE
    assert s % tm == 0 and tm % POOL_HALO == 0
    halo_per_tile = tm // POOL_HALO
    const2 = lambda bi, i: (0, 0)
    return pl.pallas_call(
        _mix_out_kernel,
        grid=(b, s // tm),
        in_specs=[
            pl.BlockSpec((1, tm, d), lambda bi, i: (bi, i, 0)),
            pl.BlockSpec((1, tm, ATTN_WIDTH), lambda bi, i: (bi, i, 0)),
            pl.BlockSpec((1, tm, POOL_WIDTH), lambda bi, i: (bi, i, 0)),
            pl.BlockSpec((1, POOL_HALO, POOL_WIDTH),
                         lambda bi, i: (bi, jnp.maximum(i * halo_per_tile - 1, 0), 0)),
            _resident(pool_w.shape, lambda bi, i: (0, 0, 0)),
            pl.BlockSpec((1, POOL_WIDTH), const2),
            _resident(w_out.shape, const2),
        ],
        out_specs=pl.BlockSpec((1, tm, d), lambda bi, i: (bi, i, 0)),
        out_shape=jax.ShapeDtypeStruct((b, s, d), _F32),
        scratch_shapes=[pltpu.VMEM((POOL_HALO + tm, POOL_WIDTH), _F32)],
        compiler_params=pltpu.CompilerParams(
            dimension_semantics=("parallel", "parallel"),
            vmem_limit_bytes=V7X_VMEM_LIMIT_BYTES),
        name="mix_out",
    )(x1, attn, u, u, pool_w, pool_scale, w_out)


def kernel(x, positions, norm_ffn1, w1_gate, w1_up, w1_down, norm_mix, w_in, pool_w, pool_scale,
           w_out, norm_ffn2, w2_gate, w2_up, w2_down, norm_final):
    b, s, d = x.shape
    depth = w_in.shape[0]
    inv_freq = ROPE_THETA ** (-jnp.arange(0, ROT_DIM, 2, dtype=_F32) / ROT_DIM)
    freq_row = jnp.concatenate([inv_freq, inv_freq, jnp.zeros((HEAD_DIM - ROT_DIM,), _F32)])[None, :]
    freq_col = inv_freq[:, None]
    fgain = norm_final[None, :]
    bf = lambda w: w.astype(_BF16)

    for l in range(depth):
        x1 = _ffn(x.reshape(b * s, d), norm_ffn1[l][None, :], bf(w1_gate[l]), bf(w1_up[l]), bf(w1_down[l]),
                  fgain, final_norm=False, name="ffn_pre").reshape(b, s, d)
        wq, wk, wv, wu = jnp.split(w_in[l], [ATTN_WIDTH, 2 * ATTN_WIDTH, 3 * ATTN_WIDTH], axis=-1)
        qt, k, vt, u, kmean = _in_proj(x1, norm_mix[l][None, :], positions, freq_row, freq_col,
                                       bf(wq.T), bf(wk), bf(wv.T), bf(wu))
        attn = _moba(qt, k, vt, kmean.reshape(b, s // MOBA_BLOCK, ATTN_WIDTH))
        x2 = _mix_out(x1, attn, u, bf(pool_w[l]), pool_scale[l][None, :], bf(w_out[l]))
        last = l == depth - 1
        x = _ffn(x2.reshape(b * s, d), norm_ffn2[l][None, :], bf(w2_gate[l]), bf(w2_up[l]), bf(w2_down[l]),
                 fgain, final_norm=last, name="ffn_post").reshape(b, s, d)
    return x
```

```python
import functools

import jax
import jax.numpy as jnp
from jax import lax
from jax.experimental import pallas as pl
from jax.experimental.pallas import tpu as pltpu

N_ATTN_HEADS = 8
HEAD_DIM = 128
ATTN_WIDTH = N_ATTN_HEADS * HEAD_DIM
POOL_WINDOWS = (2, 4, 8, 16)
POOL_GROUP_DIM = 256
POOL_WIDTH = POOL_GROUP_DIM * len(POOL_WINDOWS)
MOBA_BLOCK = 256
MOBA_TOPK = 3
ROT_DIM = HEAD_DIM // 4
ROT_HALF = ROT_DIM // 2
ROPE_THETA = 500000.0
EPS = 1e-6
NEG_INF = -1e30
LOG2_E = 1.4426950408889634
POOL_HALO = 16

V7X_VMEM_LIMIT_BYTES = 58 * 1024 * 1024

FFN_TOKEN_TILE = 1024
FFN_FF_TILE = 512
PROJ_TOKEN_TILE = 512
MIX_TOKEN_TILE = 512
MOBA_HEADS_PER_STEP = 4

_BF16 = jnp.bfloat16
_F32 = jnp.float32


def _rms(x, gain):
    inv = lax.rsqrt(jnp.mean(x * x, axis=-1, keepdims=True) + EPS)
    return x * inv * gain


def _dot(a, b):
    return jnp.dot(a, b, preferred_element_type=_F32)


def _dot_nt(a, b):
    return lax.dot_general(a, b, (((1,), (1,)), ((), ())), preferred_element_type=_F32)


def _resident(block_shape, index_map):
    return pl.BlockSpec(block_shape, index_map, pipeline_mode=pl.Buffered(1))


def _ffn_kernel(x_ref, gain_ref, wg_ref, wu_ref, wd_ref, fgain_ref, out_ref, h_scr, *, final_norm):
    f = pl.program_id(1)

    @pl.when(f == 0)
    def _():
        x = x_ref[...]
        h_scr[...] = _rms(x, gain_ref[...]).astype(_BF16)
        out_ref[...] = x

    h = h_scr[...]
    g = _dot(h, wg_ref[...])
    u = _dot(h, wu_ref[...])
    a = (g * jax.nn.sigmoid(g) * u).astype(_BF16)
    out_ref[...] += 0.5 * _dot(a, wd_ref[...])

    if final_norm:
        @pl.when(f == pl.num_programs(1) - 1)
        def _():
            out_ref[...] = _rms(out_ref[...], fgain_ref[...])


def _ffn(x2d, gain, wg, wu, wd, fgain, *, final_norm, name):
    t, d = x2d.shape
    d_ff = wg.shape[1]
    tm, tf = FFN_TOKEN_TILE, FFN_FF_TILE
    assert t % tm == 0 and d_ff % tf == 0
    return pl.pallas_call(
        functools.partial(_ffn_kernel, final_norm=final_norm),
        grid=(t // tm, d_ff // tf),
        in_specs=[
            pl.BlockSpec((tm, d), lambda i, f: (i, 0)),
            pl.BlockSpec((1, d), lambda i, f: (0, 0)),
            pl.BlockSpec((d, tf), lambda i, f: (0, f)),
            pl.BlockSpec((d, tf), lambda i, f: (0, f)),
            pl.BlockSpec((tf, d), lambda i, f: (f, 0)),
            pl.BlockSpec((1, d), lambda i, f: (0, 0)),
        ],
        out_specs=pl.BlockSpec((tm, d), lambda i, f: (i, 0)),
        out_shape=jax.ShapeDtypeStruct((t, d), _F32),
        scratch_shapes=[pltpu.VMEM((tm, d), _BF16)],
        compiler_params=pltpu.CompilerParams(
            dimension_semantics=("parallel", "arbitrary"),
            vmem_limit_bytes=V7X_VMEM_LIMIT_BYTES),
        name=name,
    )(x2d, gain, wg, wu, wd, fgain)


def _in_proj_kernel(x_ref, gain_ref, pos_col_ref, pos_row_ref, freq_row_ref, freq_col_ref,
                    wqt_ref, wk_ref, wvt_ref, wu_ref,
                    qt_ref, k_ref, vt_ref, u_ref, kmean_ref):
    tm = x_ref.shape[1]
    h = _rms(x_ref[0], gain_ref[...]).astype(_BF16)

    ang_t = freq_col_ref[...] * pos_row_ref[0].astype(_F32)
    cos_t, sin_t = jnp.cos(ang_t), jnp.sin(ang_t)

    qt = _dot_nt(wqt_ref[...], h)
    for hh in range(N_ATTN_HEADS):
        base = hh * HEAD_DIM
        x1 = qt[base:base + ROT_HALF]
        x2 = qt[base + ROT_HALF:base + ROT_DIM]
        qt_ref[0, hh, 0:ROT_HALF, :] = (x1 * cos_t - x2 * sin_t).astype(_BF16)
        qt_ref[0, hh, ROT_HALF:ROT_DIM, :] = (x2 * cos_t + x1 * sin_t).astype(_BF16)
        qt_ref[0, hh, ROT_DIM:, :] = qt[base + ROT_DIM:base + HEAD_DIM].astype(_BF16)

    vt = _dot_nt(wvt_ref[...], h)
    for hh in range(N_ATTN_HEADS):
        vt_ref[0, hh] = vt[hh * HEAD_DIM:(hh + 1) * HEAD_DIM].astype(_BF16)

    u_ref[0] = _dot(h, wu_ref[...])

    ang = pos_col_ref[0].astype(_F32) * freq_row_ref[...]
    lane = lax.broadcasted_iota(jnp.int32, (1, HEAD_DIM), 1)
    cos_n = jnp.cos(ang)
    sin_n = jnp.where(lane < ROT_HALF, -jnp.sin(ang), jnp.sin(ang))
    k = _dot(h, wk_ref[...])
    for hh in range(N_ATTN_HEADS):
        kh = k[:, hh * HEAD_DIM:(hh + 1) * HEAD_DIM]
        partner = jnp.where(lane < ROT_HALF,
                            pltpu.roll(kh, HEAD_DIM - ROT_HALF, 1),
                            pltpu.roll(kh, ROT_HALF, 1))
        kr = kh * cos_n + partner * sin_n
        k_ref[0, hh] = kr.astype(_BF16)
        for j in range(tm // MOBA_BLOCK):
            kmean_ref[0, j, :, hh * HEAD_DIM:(hh + 1) * HEAD_DIM] = jnp.mean(
                kr[j * MOBA_BLOCK:(j + 1) * MOBA_BLOCK], axis=0, keepdims=True)


def _in_proj(x1, gain, positions, freq_row, freq_col, wqt, wk, wvt, wu):
    b, s, d = x1.shape
    tm = PROJ_TOKEN_TILE
    assert s % tm == 0 and tm % MOBA_BLOCK == 0
    nb = s // MOBA_BLOCK
    bpt = tm // MOBA_BLOCK
    pos_col = positions.reshape(b, s, 1)
    pos_row = positions.reshape(b, 1, s)
    const2 = lambda bi, i: (0, 0)
    return pl.pallas_call(
        _in_proj_kernel,
        grid=(b, s // tm),
        in_specs=[
            pl.BlockSpec((1, tm, d), lambda bi, i: (bi, i, 0)),
            pl.BlockSpec((1, d), const2),
            pl.BlockSpec((1, tm, 1), lambda bi, i: (bi, i, 0)),
            pl.BlockSpec((1, 1, tm), lambda bi, i: (bi, 0, i)),
            pl.BlockSpec((1, HEAD_DIM), const2),
            pl.BlockSpec((ROT_HALF, 1), const2),
            _resident((ATTN_WIDTH, d), const2),
            _resident((d, ATTN_WIDTH), const2),
            _resident((ATTN_WIDTH, d), const2),
            _resident((d, POOL_WIDTH), const2),
        ],
        out_specs=[
            pl.BlockSpec((1, N_ATTN_HEADS, HEAD_DIM, tm), lambda bi, i: (bi, 0, 0, i)),
            pl.BlockSpec((1, N_ATTN_HEADS, tm, HEAD_DIM), lambda bi, i: (bi, 0, i, 0)),
            pl.BlockSpec((1, N_ATTN_HEADS, HEAD_DIM, tm), lambda bi, i: (bi, 0, 0, i)),
            pl.BlockSpec((1, tm, POOL_WIDTH), lambda bi, i: (bi, i, 0)),
            pl.BlockSpec((1, bpt, 1, ATTN_WIDTH), lambda bi, i: (bi, i, 0, 0)),
        ],
        out_shape=[
            jax.ShapeDtypeStruct((b, N_ATTN_HEADS, HEAD_DIM, s), _BF16),
            jax.ShapeDtypeStruct((b, N_ATTN_HEADS, s, HEAD_DIM), _BF16),
            jax.ShapeDtypeStruct((b, N_ATTN_HEADS, HEAD_DIM, s), _BF16),
            jax.ShapeDtypeStruct((b, s, POOL_WIDTH), _F32),
            jax.ShapeDtypeStruct((b, nb, 1, ATTN_WIDTH), _F32),
        ],
        compiler_params=pltpu.CompilerParams(
            dimension_semantics=("parallel", "parallel"),
            vmem_limit_bytes=V7X_VMEM_LIMIT_BYTES),
        name="in_proj",
    )(x1, gain, pos_col, pos_row, freq_row, freq_col, wqt, wk, wvt, wu)


def _moba_kernel(qt_ref, k_ref, vt_ref, kmean_ref, out_ref, cap_scr, acc_scr, s_even, s_odd, *, nb, hg):
    i = pl.program_id(2)
    blk = MOBA_BLOCK
    scale2 = (HEAD_DIM ** -0.5) * LOG2_E
    kb_id = lax.broadcasted_iota(jnp.int32, (nb, blk), 0)
    past = kb_id < i

    def select_blocks(h):
        km = kmean_ref[0, :, h * HEAD_DIM:(h + 1) * HEAD_DIM]
        km_hi = km.astype(_BF16)
        km_lo = (km - km_hi.astype(_F32)).astype(_BF16)
        gate = _dot(km_hi, qt_ref[0, h]) + _dot(km_lo, qt_ref[0, h])
        gate = jnp.where(past, gate, NEG_INF)
        rank = jnp.zeros((nb, blk), jnp.int32)
        for m in range(nb):
            gm = gate[m:m + 1, :]
            beats = (gm > gate) | ((gm == gate) & (kb_id > m))
            rank = rank + beats.astype(jnp.int32)
        selected = (rank < MOBA_TOPK) & past
        cap_scr[h] = jnp.where(selected, jnp.inf, NEG_INF).astype(_F32)

    def scores(h, n):
        kblk = k_ref[0, h, pl.ds(pl.multiple_of(n * blk, blk), blk), :]
        return _dot(kblk, qt_ref[0, h]) * scale2

    def values(h, n):
        return vt_ref[0, h, :, pl.ds(pl.multiple_of(n * blk, blk), blk)]

    key_pos = lax.broadcasted_iota(jnp.int32, (blk, blk), 0)
    qry_pos = lax.broadcasted_iota(jnp.int32, (blk, blk), 1)
    causal = key_pos <= qry_pos
    m_runs, l_runs, ps = [], [], []
    ss = [jnp.where(causal, scores(h, i), NEG_INF) for h in range(hg)]
    for h in range(hg):
        select_blocks(h)
    for h in range(hg):
        m_own = jnp.max(ss[h], axis=0, keepdims=True)
        p = jnp.exp2(ss[h] - m_own)
        m_runs.append(m_own)
        l_runs.append(jnp.sum(p, axis=0, keepdims=True))
        ps.append(p.astype(_BF16))
    for h in range(hg):
        acc_scr[h] = _dot(values(h, i), ps[h])
        s_even[h] = scores(h, 0)

    def step(n, s_cur, s_nxt, n_nxt, m_runs, l_runs):
        for h in range(hg):
            s_nxt[h] = scores(h, n_nxt)
        m_out, l_out, alphas, ps = [], [], [], []
        for h in range(hg):
            s = jnp.minimum(s_cur[h], cap_scr[h, pl.ds(n, 1), :])
            m_new = jnp.maximum(m_runs[h], jnp.max(s, axis=0, keepdims=True))
            alpha = jnp.exp2(m_runs[h] - m_new)
            p = jnp.exp2(s - m_new)
            m_out.append(m_new)
            l_out.append(alpha * l_runs[h] + jnp.sum(p, axis=0, keepdims=True))
            alphas.append(alpha)
            ps.append(p.astype(_BF16))
        for h in range(hg):
            acc_scr[h] = alphas[h] * acc_scr[h] + _dot(values(h, n), ps[h])
        return tuple(m_out), tuple(l_out)

    def body(j, carry):
        carry = step(2 * j, s_even, s_odd, 2 * j + 1, *carry)
        return step(2 * j + 1, s_odd, s_even, jnp.minimum(2 * j + 2, nb - 1), *carry)

    _, l_fin = lax.fori_loop(0, (i + 1) // 2, body, (tuple(m_runs), tuple(l_runs)))
    for h in range(hg):
        out_ref[0, :, h * HEAD_DIM:(h + 1) * HEAD_DIM] = jnp.transpose(
            acc_scr[h] / l_fin[h]).astype(out_ref.dtype)


def _moba(qt, k, vt, kmean):
    b, nh, hd, s = qt.shape
    nb = s // MOBA_BLOCK
    hg = MOBA_HEADS_PER_STEP
    assert nh % hg == 0
    return pl.pallas_call(
        functools.partial(_moba_kernel, nb=nb, hg=hg),
        grid=(b, nh // hg, nb),
        in_specs=[
            pl.BlockSpec((1, hg, hd, MOBA_BLOCK), lambda bi, g, i: (bi, g, 0, i)),
            pl.BlockSpec((1, hg, s, hd), lambda bi, g, i: (bi, g, 0, 0)),
            pl.BlockSpec((1, hg, hd, s), lambda bi, g, i: (bi, g, 0, 0)),
            pl.BlockSpec((1, nb, hg * hd), lambda bi, g, i: (bi, 0, g)),
        ],
        out_specs=pl.BlockSpec((1, MOBA_BLOCK, hg * hd), lambda bi, g, i: (bi, i, g)),
        out_shape=jax.ShapeDtypeStruct((b, s, nh * hd), _BF16),
        scratch_shapes=[pltpu.VMEM((hg, nb, MOBA_BLOCK), _F32),
                        pltpu.VMEM((hg, hd, MOBA_BLOCK), _F32),
                        pltpu.VMEM((hg, MOBA_BLOCK, MOBA_BLOCK), _F32),
                        pltpu.VMEM((hg, MOBA_BLOCK, MOBA_BLOCK), _F32)],
        compiler_params=pltpu.CompilerParams(
            dimension_semantics=("parallel", "parallel", "arbitrary"),
            vmem_limit_bytes=V7X_VMEM_LIMIT_BYTES),
        name="moba",
    )(qt, k, vt, kmean)


def _mix_out_kernel(x_ref, attn_ref, u_ref, halo_ref, pw_ref, ps_ref, wo_ref, out_ref, ubuf):
    tm = x_ref.shape[1]
    si = pl.program_id(1)
    ubuf[0:POOL_HALO, :] = jnp.where(si > 0, halo_ref[0], 0.0)
    ubuf[POOL_HALO:, :] = u_ref[0]

    t = si * tm + lax.broadcasted_iota(jnp.int32, (tm, 1), 0)
    mix = _dot(attn_ref[0], wo_ref[0:ATTN_WIDTH, :])
    for g, w in enumerate(POOL_WINDOWS):
        cols = slice(g * POOL_GROUP_DIM, (g + 1) * POOL_GROUP_DIM)
        own = ubuf[POOL_HALO:, cols]
        wsum = own
        for dlt in range(1, w):
            wsum = wsum + ubuf[POOL_HALO - dlt:POOL_HALO - dlt + tm, cols]
        count = jnp.minimum(t + 1, w).astype(_F32)
        pooled = wsum / count - own
        y = _dot(pooled.astype(_BF16), pw_ref[g]) * ps_ref[:, cols]
        row0 = ATTN_WIDTH + g * POOL_GROUP_DIM
        mix = mix + _dot(y.astype(_BF16), wo_ref[row0:row0 + POOL_GROUP_DIM, :])
    out_ref[0] = x_ref[0] + mix


def _mix_out(x1, attn, u, pool_w, pool_scale, w_out):
    b, s, d = x1.shape
    tm = MIX_TOKEN_TILE
    assert s % tm == 0 and tm % POOL_HALO == 0
    halo_per_tile = tm // POOL_HALO
    const2 = lambda bi, i: (0, 0)
    return pl.pallas_call(
        _mix_out_kernel,
        grid=(b, s // tm),
        in_specs=[
            pl.BlockSpec((1, tm, d), lambda bi, i: (bi, i, 0)),
            pl.BlockSpec((1, tm, ATTN_WIDTH), lambda bi, i: (bi, i, 0)),
            pl.BlockSpec((1, tm, POOL_WIDTH), lambda bi, i: (bi, i, 0)),
            pl.BlockSpec((1, POOL_HALO, POOL_WIDTH),
                         lambda bi, i: (bi, jnp.maximum(i * halo_per_tile - 1, 0), 0)),
            _resident(pool_w.shape, lambda bi, i: (0, 0, 0)),
            pl.BlockSpec((1, POOL_WIDTH), const2),
            _resident(w_out.shape, const2),
        ],
        out_specs=pl.BlockSpec((1, tm, d), lambda bi, i: (bi, i, 0)),
        out_shape=jax.ShapeDtypeStruct((b, s, d), _F32),
        scratch_shapes=[pltpu.VMEM((POOL_HALO + tm, POOL_WIDTH), _F32)],
        compiler_params=pltpu.CompilerParams(
            dimension_semantics=("parallel", "parallel"),
            vmem_limit_bytes=V7X_VMEM_LIMIT_BYTES),
        name="mix_out",
    )(x1, attn, u, u, pool_w, pool_scale, w_out)


def kernel(x, positions, norm_ffn1, w1_gate, w1_up, w1_down, norm_mix, w_in, pool_w, pool_scale,
           w_out, norm_ffn2, w2_gate, w2_up, w2_down, norm_final):
    b, s, d = x.shape
    depth = w_in.shape[0]
    inv_freq = ROPE_THETA ** (-jnp.arange(0, ROT_DIM, 2, dtype=_F32) / ROT_DIM)
    freq_row = jnp.concatenate([inv_freq, inv_freq, jnp.zeros((HEAD_DIM - ROT_DIM,), _F32)])[None, :]
    freq_col = inv_freq[:, None]
    fgain = norm_final[None, :]
    bf = lambda w: w.astype(_BF16)

    for l in range(depth):
        x1 = _ffn(x.reshape(b * s, d), norm_ffn1[l][None, :], bf(w1_gate[l]), bf(w1_up[l]), bf(w1_down[l]),
                  fgain, final_norm=False, name="ffn_pre").reshape(b, s, d)
        wq, wk, wv, wu = jnp.split(w_in[l], [ATTN_WIDTH, 2 * ATTN_WIDTH, 3 * ATTN_WIDTH], axis=-1)
        qt, k, vt, u, kmean = _in_proj(x1, norm_mix[l][None, :], positions, freq_row, freq_col,
                                       bf(wq.T), bf(wk), bf(wv.T), bf(wu))
        attn = _moba(qt, k, vt, kmean.reshape(b, s // MOBA_BLOCK, ATTN_WIDTH))
        x2 = _mix_out(x1, attn, u, bf(pool_w[l]), pool_scale[l][None, :], bf(w_out[l]))
        last = l == depth - 1
        x = _ffn(x2.reshape(b * s, d), norm_ffn2[l][None, :], bf(w2_gate[l]), bf(w2_up[l]), bf(w2_down[l]),
                 fgain, final_norm=last, name="ffn_post").reshape(b, s, d)
    return x
```

```python
import functools

import jax
import jax.numpy as jnp
from jax import lax
from jax.experimental import pallas as pl
from jax.experimental.pallas import tpu as pltpu

N_ATTN_HEADS = 8
HEAD_DIM = 128
ATTN_WIDTH = N_ATTN_HEADS * HEAD_DIM
POOL_WINDOWS = (2, 4, 8, 16)
POOL_GROUP_DIM = 256
POOL_WIDTH = POOL_GROUP_DIM * len(POOL_WINDOWS)
MOBA_BLOCK = 256
MOBA_TOPK = 3
ROT_DIM = HEAD_DIM // 4
ROT_HALF = ROT_DIM // 2
ROPE_THETA = 500000.0
EPS = 1e-6
NEG_INF = -1e30
LOG2_E = 1.4426950408889634
QUERY_SCALE = (HEAD_DIM ** -0.5) * LOG2_E
V_ROWS = HEAD_DIM + 16
POOL_HALO = 16

V7X_VMEM_LIMIT_BYTES = 58 * 1024 * 1024

FFN_TOKEN_TILE = 1024
FFN_FF_TILE = 512
FFN_FIRST_FF_TILE = 256
PROJ_TOKEN_TILE = 512
MIX_TOKEN_TILE = 512
MOBA_HEADS_PER_STEP = 4

_BF16 = jnp.bfloat16
_F32 = jnp.float32


def _rms(x, gain):
    inv = lax.rsqrt(jnp.mean(x * x, axis=-1, keepdims=True) + EPS)
    return x * inv * gain


def _dot(a, b):
    return jnp.dot(a, b, preferred_element_type=_F32)


def _dot_nt(a, b):
    return lax.dot_general(a, b, (((1,), (1,)), ((), ())), preferred_element_type=_F32)


def _resident(block_shape, index_map):
    return pl.BlockSpec(block_shape, index_map, pipeline_mode=pl.Buffered(1))


def _ffn_step(x_ref, gain_ref, load_weights, fgain_ref, out_ref, h_scr, final_norm):
    f = pl.program_id(1)

    @pl.when(f == 0)
    def _():
        x = x_ref[...]
        h_scr[...] = _rms(x, gain_ref[...]).astype(_BF16)
        out_ref[...] = x

    wg, wu, wd = load_weights()
    h = h_scr[...]
    g = _dot(h, wg)
    u = _dot(h, wu)
    a = (g * jax.nn.sigmoid(g) * u).astype(_BF16)
    out_ref[...] += 0.5 * _dot(a, wd)

    if final_norm:
        @pl.when(f == pl.num_programs(1) - 1)
        def _():
            out_ref[...] = _rms(out_ref[...], fgain_ref[...])


def _ffn_first_tile_kernel(x_ref, gain_ref, wg32_ref, wu32_ref, wd32_ref, fgain_ref,
                           out_ref, wg_ref, wu_ref, wd_ref, h_scr, *, final_norm):
    def load_weights():
        wg = wg32_ref[...].astype(_BF16)
        wu = wu32_ref[...].astype(_BF16)
        wd = wd32_ref[...].astype(_BF16)
        wg_ref[...] = wg
        wu_ref[...] = wu
        wd_ref[...] = wd
        return wg, wu, wd

    _ffn_step(x_ref, gain_ref, load_weights, fgain_ref, out_ref, h_scr, final_norm)


def _ffn_rest_kernel(x_ref, gain_ref, wg_ref, wu_ref, wd_ref, fgain_ref, first_ref, out_ref, h_scr, *, final_norm):
    del first_ref
    load_weights = lambda: (wg_ref[...], wu_ref[...], wd_ref[...])
    _ffn_step(x_ref, gain_ref, load_weights, fgain_ref, out_ref, h_scr, final_norm)


def _ffn(x2d, gain, wg32, wu32, wd32, fgain, *, final_norm, name):
    t, d = x2d.shape
    d_ff = wg32.shape[1]
    tm, tf, tf0 = FFN_TOKEN_TILE, FFN_FF_TILE, FFN_FIRST_FF_TILE
    assert t % tm == 0 and d_ff % tf == 0 and d_ff % tf0 == 0
    params = pltpu.CompilerParams(dimension_semantics=("parallel", "arbitrary"),
                                  vmem_limit_bytes=V7X_VMEM_LIMIT_BYTES)
    row = lambda i, f: (0, 0)
    first, wg, wu, wd = pl.pallas_call(
        functools.partial(_ffn_first_tile_kernel, final_norm=final_norm),
        grid=(1, d_ff // tf0),
        in_specs=[
            _resident((tm, d), lambda i, f: (0, 0)),
            pl.BlockSpec((1, d), row),
            pl.BlockSpec((d, tf0), lambda i, f: (0, f)),
            pl.BlockSpec((d, tf0), lambda i, f: (0, f)),
            pl.BlockSpec((tf0, d), lambda i, f: (f, 0)),
            pl.BlockSpec((1, d), row),
        ],
        out_specs=[
            pl.BlockSpec((tm, d), lambda i, f: (0, 0)),
            pl.BlockSpec((d, tf0), lambda i, f: (0, f)),
            pl.BlockSpec((d, tf0), lambda i, f: (0, f)),
            pl.BlockSpec((tf0, d), lambda i, f: (f, 0)),
        ],
        out_shape=[
            jax.ShapeDtypeStruct((t, d), _F32),
            jax.ShapeDtypeStruct(wg32.shape, _BF16),
            jax.ShapeDtypeStruct(wu32.shape, _BF16),
            jax.ShapeDtypeStruct(wd32.shape, _BF16),
        ],
        scratch_shapes=[pltpu.VMEM((tm, d), _BF16)],
        compiler_params=params,
        name=name + "_first",
    )(x2d, gain, wg32, wu32, wd32, fgain)
    return pl.pallas_call(
        functools.partial(_ffn_rest_kernel, final_norm=final_norm),
        grid=(t // tm - 1, d_ff // tf),
        in_specs=[
            pl.BlockSpec((tm, d), lambda i, f: (i + 1, 0)),
            pl.BlockSpec((1, d), row),
            pl.BlockSpec((d, tf), lambda i, f: (0, f)),
            pl.BlockSpec((d, tf), lambda i, f: (0, f)),
            pl.BlockSpec((tf, d), lambda i, f: (f, 0)),
            pl.BlockSpec((1, d), row),
            pl.BlockSpec(memory_space=pl.ANY),
        ],
        out_specs=pl.BlockSpec((tm, d), lambda i, f: (i + 1, 0)),
        out_shape=jax.ShapeDtypeStruct((t, d), _F32),
        input_output_aliases={6: 0},
        scratch_shapes=[pltpu.VMEM((tm, d), _BF16)],
        compiler_params=params,
        name=name + "_rest",
    )(x2d, gain, wg, wu, wd, fgain, first)


def _in_proj_kernel(x_ref, gain_ref, pos_col_ref, pos_row_ref, freq_row_ref, freq_col_ref,
                    wqt_ref, wk_ref, wvt_ref, wu_ref,
                    qt_ref, k_ref, vt_ref, u_ref, kmean_ref):
    tm = x_ref.shape[1]
    h = _rms(x_ref[0], gain_ref[...]).astype(_BF16)

    ang_t = freq_col_ref[...] * pos_row_ref[0].astype(_F32)
    cos_t, sin_t = jnp.cos(ang_t), jnp.sin(ang_t)

    qt = _dot_nt(wqt_ref[...], h) * QUERY_SCALE
    for hh in range(N_ATTN_HEADS):
        base = hh * HEAD_DIM
        x1 = qt[base:base + ROT_HALF]
        x2 = qt[base + ROT_HALF:base + ROT_DIM]
        qt_ref[0, hh, 0:ROT_HALF, :] = (x1 * cos_t - x2 * sin_t).astype(_BF16)
        qt_ref[0, hh, ROT_HALF:ROT_DIM, :] = (x2 * cos_t + x1 * sin_t).astype(_BF16)
        qt_ref[0, hh, ROT_DIM:, :] = qt[base + ROT_DIM:base + HEAD_DIM].astype(_BF16)

    vt = _dot_nt(wvt_ref[...], h)
    ones = jnp.ones((V_ROWS - HEAD_DIM, tm), _BF16)
    for hh in range(N_ATTN_HEADS):
        vt_ref[0, hh, 0:HEAD_DIM, :] = vt[hh * HEAD_DIM:(hh + 1) * HEAD_DIM].astype(_BF16)
        vt_ref[0, hh, HEAD_DIM:, :] = ones

    u_ref[0] = _dot(h, wu_ref[...])

    ang = pos_col_ref[0].astype(_F32) * freq_row_ref[...]
    lane = lax.broadcasted_iota(jnp.int32, (1, HEAD_DIM), 1)
    cos_n = jnp.cos(ang)
    sin_n = jnp.where(lane < ROT_HALF, -jnp.sin(ang), jnp.sin(ang))
    k = _dot(h, wk_ref[...])
    for hh in range(N_ATTN_HEADS):
        kh = k[:, hh * HEAD_DIM:(hh + 1) * HEAD_DIM]
        partner = jnp.where(lane < ROT_HALF,
                            pltpu.roll(kh, HEAD_DIM - ROT_HALF, 1),
                            pltpu.roll(kh, ROT_HALF, 1))
        kr = kh * cos_n + partner * sin_n
        k_ref[0, hh] = kr.astype(_BF16)
        for j in range(tm // MOBA_BLOCK):
            kmean_ref[0, j, :, hh * HEAD_DIM:(hh + 1) * HEAD_DIM] = jnp.mean(
                kr[j * MOBA_BLOCK:(j + 1) * MOBA_BLOCK], axis=0, keepdims=True)


def _in_proj(x1, gain, positions, freq_row, freq_col, wqt, wk, wvt, wu):
    b, s, d = x1.shape
    tm = PROJ_TOKEN_TILE
    assert s % tm == 0 and tm % MOBA_BLOCK == 0
    nb = s // MOBA_BLOCK
    bpt = tm // MOBA_BLOCK
    pos_col = positions.reshape(b, s, 1)
    pos_row = positions.reshape(b, 1, s)
    const2 = lambda bi, i: (0, 0)
    return pl.pallas_call(
        _in_proj_kernel,
        grid=(b, s // tm),
        in_specs=[
            pl.BlockSpec((1, tm, d), lambda bi, i: (bi, i, 0)),
            pl.BlockSpec((1, d), const2),
            pl.BlockSpec((1, tm, 1), lambda bi, i: (bi, i, 0)),
            pl.BlockSpec((1, 1, tm), lambda bi, i: (bi, 0, i)),
            pl.BlockSpec((1, HEAD_DIM), const2),
            pl.BlockSpec((ROT_HALF, 1), const2),
            _resident((ATTN_WIDTH, d), const2),
            _resident((d, ATTN_WIDTH), const2),
            _resident((ATTN_WIDTH, d), const2),
            _resident((d, POOL_WIDTH), const2),
        ],
        out_specs=[
            pl.BlockSpec((1, N_ATTN_HEADS, HEAD_DIM, tm), lambda bi, i: (bi, 0, 0, i)),
            pl.BlockSpec((1, N_ATTN_HEADS, tm, HEAD_DIM), lambda bi, i: (bi, 0, i, 0)),
            pl.BlockSpec((1, N_ATTN_HEADS, V_ROWS, tm), lambda bi, i: (bi, 0, 0, i)),
            pl.BlockSpec((1, tm, POOL_WIDTH), lambda bi, i: (bi, i, 0)),
            pl.BlockSpec((1, bpt, 1, ATTN_WIDTH), lambda bi, i: (bi, i, 0, 0)),
        ],
        out_shape=[
            jax.ShapeDtypeStruct((b, N_ATTN_HEADS, HEAD_DIM, s), _BF16),
            jax.ShapeDtypeStruct((b, N_ATTN_HEADS, s, HEAD_DIM), _BF16),
            jax.ShapeDtypeStruct((b, N_ATTN_HEADS, V_ROWS, s), _BF16),
            jax.ShapeDtypeStruct((b, s, POOL_WIDTH), _F32),
            jax.ShapeDtypeStruct((b, nb, 1, ATTN_WIDTH), _F32),
        ],
        compiler_params=pltpu.CompilerParams(
            dimension_semantics=("parallel", "parallel"),
            vmem_limit_bytes=V7X_VMEM_LIMIT_BYTES),
        name="in_proj",
    )(x1, gain, pos_col, pos_row, freq_row, freq_col, wqt, wk, wvt, wu)


def _moba_kernel(qt_ref, k_ref, vt_ref, kmean_ref, out_ref, cap_scr, acc_scr, s_even, s_odd, *, nb, hg):
    i = pl.program_id(2)
    blk = MOBA_BLOCK
    kb_id = lax.broadcasted_iota(jnp.int32, (nb, blk), 0)
    past = kb_id < i

    def select_blocks(h):
        km = kmean_ref[0, :, h * HEAD_DIM:(h + 1) * HEAD_DIM]
        km_hi = km.astype(_BF16)
        km_lo = (km - km_hi.astype(_F32)).astype(_BF16)
        gate = _dot(km_hi, qt_ref[0, h]) + _dot(km_lo, qt_ref[0, h])
        gate = jnp.where(past, gate, NEG_INF)
        rank = jnp.zeros((nb, blk), jnp.int32)
        for m in range(nb):
            gm = gate[m:m + 1, :]
            beats = (gm > gate) | ((gm == gate) & (kb_id > m))
            rank = rank + beats.astype(jnp.int32)
        selected = (rank < MOBA_TOPK) & past
        cap_scr[h] = jnp.where(selected, jnp.inf, NEG_INF).astype(_F32)

    def scores(h, n):
        kblk = k_ref[0, h, pl.ds(pl.multiple_of(n * blk, blk), blk), :]
        return _dot(kblk, qt_ref[0, h])

    def weighted_values(h, n, p):
        vblk = vt_ref[0, h, :, pl.ds(pl.multiple_of(n * blk, blk), blk)]
        return _dot(vblk, p)

    key_pos = lax.broadcasted_iota(jnp.int32, (blk, blk), 0)
    qry_pos = lax.broadcasted_iota(jnp.int32, (blk, blk), 1)
    causal = key_pos <= qry_pos
    m_runs, ps = [], []
    ss = [jnp.where(causal, scores(h, i), NEG_INF) for h in range(hg)]
    for h in range(hg):
        select_blocks(h)
    for h in range(hg):
        m_own = jnp.max(ss[h], axis=0, keepdims=True)
        m_runs.append(m_own)
        ps.append(jnp.exp2(ss[h] - m_own).astype(_BF16))
    for h in range(hg):
        acc_scr[h] = weighted_values(h, i, ps[h])
        s_even[h] = scores(h, 0)

    def step(n, s_cur, s_nxt, n_nxt, m_runs):
        for h in range(hg):
            s_nxt[h] = scores(h, n_nxt)
        m_out, alphas, ps = [], [], []
        for h in range(hg):
            s = jnp.minimum(s_cur[h], cap_scr[h, pl.ds(n, 1), :])
            m_new = jnp.maximum(m_runs[h], jnp.max(s, axis=0, keepdims=True))
            m_out.append(m_new)
            alphas.append(jnp.exp2(m_runs[h] - m_new))
            ps.append(jnp.exp2(s - m_new).astype(_BF16))
        for h in range(hg):
            acc_scr[h] = alphas[h] * acc_scr[h] + weighted_values(h, n, ps[h])
        return tuple(m_out)

    def body(j, m_runs):
        m_runs = step(2 * j, s_even, s_odd, 2 * j + 1, m_runs)
        return step(2 * j + 1, s_odd, s_even, jnp.minimum(2 * j + 2, nb - 1), m_runs)

    lax.fori_loop(0, (i + 1) // 2, body, tuple(m_runs))
    for h in range(hg):
        acc = acc_scr[h]
        out_ref[0, :, h * HEAD_DIM:(h + 1) * HEAD_DIM] = jnp.transpose(
            acc[0:HEAD_DIM] / acc[HEAD_DIM:HEAD_DIM + 1]).astype(out_ref.dtype)


def _moba(qt, k, vt, kmean):
    b, nh, hd, s = qt.shape
    nb = s // MOBA_BLOCK
    hg = MOBA_HEADS_PER_STEP
    assert nh % hg == 0
    return pl.pallas_call(
        functools.partial(_moba_kernel, nb=nb, hg=hg),
        grid=(b, nh // hg, nb),
        in_specs=[
            pl.BlockSpec((1, hg, hd, MOBA_BLOCK), lambda bi, g, i: (bi, g, 0, i)),
            pl.BlockSpec((1, hg, s, hd), lambda bi, g, i: (bi, g, 0, 0)),
            pl.BlockSpec((1, hg, V_ROWS, s), lambda bi, g, i: (bi, g, 0, 0)),
            pl.BlockSpec((1, nb, hg * hd), lambda bi, g, i: (bi, 0, g)),
        ],
        out_specs=pl.BlockSpec((1, MOBA_BLOCK, hg * hd), lambda bi, g, i: (bi, i, g)),
        out_shape=jax.ShapeDtypeStruct((b, s, nh * hd), _BF16),
        scratch_shapes=[pltpu.VMEM((hg, nb, MOBA_BLOCK), _F32),
                        pltpu.VMEM((hg, V_ROWS, MOBA_BLOCK), _F32),
                        pltpu.VMEM((hg, MOBA_BLOCK, MOBA_BLOCK), _F32),
                        pltpu.VMEM((hg, MOBA_BLOCK, MOBA_BLOCK), _F32)],
        compiler_params=pltpu.CompilerParams(
            dimension_semantics=("parallel", "parallel", "arbitrary"),
            vmem_limit_bytes=V7X_VMEM_LIMIT_BYTES),
        name="moba",
    )(qt, k, vt, kmean)


def _mix_out_kernel(x_ref, attn_ref, u_ref, halo_ref, pw_ref, ps_ref, wo_ref, out_ref, ubuf):
    tm = x_ref.shape[1]
    si = pl.program_id(1)
    ubuf[0:POOL_HALO, :] = jnp.where(si > 0, halo_ref[0], 0.0)
    ubuf[POOL_HALO:, :] = u_ref[0]

    t = si * tm + lax.broadcasted_iota(jnp.int32, (tm, 1), 0)
    mix = _dot(attn_ref[0], wo_ref[0:ATTN_WIDTH, :])
    for g, w in enumerate(POOL_WINDOWS):
        cols = slice(g * POOL_GROUP_DIM, (g + 1) * POOL_GROUP_DIM)
        own = ubuf[POOL_HALO:, cols]
        wsum = own
        for dlt in range(1, w):
            wsum = wsum + ubuf[POOL_HALO - dlt:POOL_HALO - dlt + tm, cols]
        count = jnp.minimum(t + 1, w).astype(_F32)
        pooled = wsum / count - own
        y = _dot(pooled.astype(_BF16), pw_ref[g]) * ps_ref[:, cols]
        row0 = ATTN_WIDTH + g * POOL_GROUP_DIM
        mix = mix + _dot(y.astype(_BF16), wo_ref[row0:row0 + POOL_GROUP_DIM, :])
    out_ref[0] = x_ref[0] + mix


def _mix_out(x1, attn, u, pool_w, pool_scale, w_out):
    b, s, d = x1.shape
    tm = MIX_TOKEN_TILE
    assert s % tm == 0 and tm % POOL_HALO == 0
    halo_per_tile = tm // POOL_HALO
    const2 = lambda bi, i: (0, 0)
    return pl.pallas_call(
        _mix_out_kernel,
        grid=(b, s // tm),
        in_specs=[
            pl.BlockSpec((1, tm, d), lambda bi, i: (bi, i, 0)),
            pl.BlockSpec((1, tm, ATTN_WIDTH), lambda bi, i: (bi, i, 0)),
            pl.BlockSpec((1, tm, POOL_WIDTH), lambda bi, i: (bi, i, 0)),
            pl.BlockSpec((1, POOL_HALO, POOL_WIDTH),
                         lambda bi, i: (bi, jnp.maximum(i * halo_per_tile - 1, 0), 0)),
            _resident(pool_w.shape, lambda bi, i: (0, 0, 0)),
            pl.BlockSpec((1, POOL_WIDTH), const2),
            _resident(w_out.shape, const2),
        ],
        out_specs=pl.BlockSpec((1, tm, d), lambda bi, i: (bi, i, 0)),
        out_shape=jax.ShapeDtypeStruct((b, s, d), _F32),
        scratch_shapes=[pltpu.VMEM((POOL_HALO + tm, POOL_WIDTH), _F32)],
        compiler_params=pltpu.CompilerParams(
            dimension_semantics=("parallel", "parallel"),
            vmem_limit_bytes=V7X_VMEM_LIMIT_BYTES),
        name="mix_out",
    )(x1, attn, u, u, pool_w, pool_scale, w_out)


def kernel(x, positions, norm_ffn1, w1_gate, w1_up, w1_down, norm_mix, w_in, pool_w, pool_scale,
           w_out, norm_ffn2, w2_gate, w2_up, w2_down, norm_final):
    b, s, d = x.shape
    depth = w_in.shape[0]
    inv_freq = ROPE_THETA ** (-jnp.arange(0, ROT_DIM, 2, dtype=_F32) / ROT_DIM)
    freq_row = jnp.concatenate([inv_freq, inv_freq, jnp.zeros((HEAD_DIM - ROT_DIM,), _F32)])[None, :]
    freq_col = inv_freq[:, None]
    fgain = norm_final[None, :]
    bf = lambda w: w.astype(_BF16)

    for l in range(depth):
        x1 = _ffn(x.reshape(b * s, d), norm_ffn1[l][None, :], w1_gate[l], w1_up[l], w1_down[l],
                  fgain, final_norm=False, name="ffn_pre").reshape(b, s, d)
        wq, wk, wv, wu = jnp.split(w_in[l], [ATTN_WIDTH, 2 * ATTN_WIDTH, 3 * ATTN_WIDTH], axis=-1)
        qt, k, vt, u, kmean = _in_proj(x1, norm_mix[l][None, :], positions, freq_row, freq_col,
                                       bf(wq.T), bf(wk), bf(wv.T), bf(wu))
        attn = _moba(qt, k, vt, kmean.reshape(b, s // MOBA_BLOCK, ATTN_WIDTH))
        x2 = _mix_out(x1, attn, u, bf(pool_w[l]), pool_scale[l][None, :], bf(w_out[l]))
        last = l == depth - 1
        x = _ffn(x2.reshape(b * s, d), norm_ffn2[l][None, :], w2_gate[l], w2_up[l], w2_down[l],
                 fgain, final_norm=last, name="ffn_post").reshape(b, s, d)
    return x
```

```python
import functools

import jax
import jax.numpy as jnp
from jax import lax
from jax.experimental import pallas as pl
from jax.experimental.pallas import tpu as pltpu

N_ATTN_HEADS = 8
HEAD_DIM = 128
ATTN_WIDTH = N_ATTN_HEADS * HEAD_DIM
POOL_WINDOWS = (2, 4, 8, 16)
POOL_GROUP_DIM = 256
POOL_WIDTH = POOL_GROUP_DIM * len(POOL_WINDOWS)
MOBA_BLOCK = 256
MOBA_TOPK = 3
ROT_DIM = HEAD_DIM // 4
ROT_HALF = ROT_DIM // 2
ROPE_THETA = 500000.0
EPS = 1e-6
NEG_INF = -1e30
LOG2_E = 1.4426950408889634
QUERY_SCALE = (HEAD_DIM ** -0.5) * LOG2_E
V_ROWS = HEAD_DIM + 16
POOL_HALO = 16

V7X_VMEM_LIMIT_BYTES = 58 * 1024 * 1024

FFN_TOKEN_TILE = 1024
FFN_FF_TILE = 512
FFN_FIRST_FF_TILE = 256
PROJ_TOKEN_TILE = 512
MIX_TOKEN_TILE = 512
MOBA_HEADS_PER_STEP = 8

_BF16 = jnp.bfloat16
_F32 = jnp.float32


def _rms(x, gain):
    inv = lax.rsqrt(jnp.mean(x * x, axis=-1, keepdims=True) + EPS)
    return x * inv * gain


def _dot(a, b):
    return jnp.dot(a, b, preferred_element_type=_F32)


def _dot_nt(a, b):
    return lax.dot_general(a, b, (((1,), (1,)), ((), ())), preferred_element_type=_F32)


def _resident(block_shape, index_map):
    return pl.BlockSpec(block_shape, index_map, pipeline_mode=pl.Buffered(1))


def _ffn_step(x_ref, gain_ref, load_weights, fgain_ref, out_ref, h_scr, final_norm):
    f = pl.program_id(1)

    @pl.when(f == 0)
    def _():
        x = x_ref[...]
        h_scr[...] = _rms(x, gain_ref[...]).astype(_BF16)
        out_ref[...] = x

    wg, wu, wd = load_weights()
    h = h_scr[...]
    g = _dot(h, wg)
    u = _dot(h, wu)
    a = (g * jax.nn.sigmoid(g) * u).astype(_BF16)
    out_ref[...] += 0.5 * _dot(a, wd)

    if final_norm:
        @pl.when(f == pl.num_programs(1) - 1)
        def _():
            out_ref[...] = _rms(out_ref[...], fgain_ref[...])


def _ffn_first_tile_kernel(x_ref, gain_ref, wg32_ref, wu32_ref, wd32_ref, fgain_ref,
                           out_ref, wg_ref, wu_ref, wd_ref, h_scr, *, final_norm):
    def load_weights():
        wg = wg32_ref[...].astype(_BF16)
        wu = wu32_ref[...].astype(_BF16)
        wd = wd32_ref[...].astype(_BF16)
        wg_ref[...] = wg
        wu_ref[...] = wu
        wd_ref[...] = wd
        return wg, wu, wd

    _ffn_step(x_ref, gain_ref, load_weights, fgain_ref, out_ref, h_scr, final_norm)


def _ffn_rest_kernel(x_ref, gain_ref, wg_ref, wu_ref, wd_ref, fgain_ref, first_ref, out_ref, h_scr, *, final_norm):
    del first_ref
    load_weights = lambda: (wg_ref[...], wu_ref[...], wd_ref[...])
    _ffn_step(x_ref, gain_ref, load_weights, fgain_ref, out_ref, h_scr, final_norm)


def _ffn(x2d, gain, wg32, wu32, wd32, fgain, *, final_norm, name):
    t, d = x2d.shape
    d_ff = wg32.shape[1]
    tm, tf, tf0 = FFN_TOKEN_TILE, FFN_FF_TILE, FFN_FIRST_FF_TILE
    assert t % tm == 0 and d_ff % tf == 0 and d_ff % tf0 == 0
    params = pltpu.CompilerParams(dimension_semantics=("parallel", "arbitrary"),
                                  vmem_limit_bytes=V7X_VMEM_LIMIT_BYTES)
    row = lambda i, f: (0, 0)
    first, wg, wu, wd = pl.pallas_call(
        functools.partial(_ffn_first_tile_kernel, final_norm=final_norm),
        grid=(1, d_ff // tf0),
        in_specs=[
            _resident((tm, d), lambda i, f: (0, 0)),
            pl.BlockSpec((1, d), row),
            pl.BlockSpec((d, tf0), lambda i, f: (0, f)),
            pl.BlockSpec((d, tf0), lambda i, f: (0, f)),
            pl.BlockSpec((tf0, d), lambda i, f: (f, 0)),
            pl.BlockSpec((1, d), row),
        ],
        out_specs=[
            pl.BlockSpec((tm, d), lambda i, f: (0, 0)),
            pl.BlockSpec((d, tf0), lambda i, f: (0, f)),
            pl.BlockSpec((d, tf0), lambda i, f: (0, f)),
            pl.BlockSpec((tf0, d), lambda i, f: (f, 0)),
        ],
        out_shape=[
            jax.ShapeDtypeStruct((t, d), _F32),
            jax.ShapeDtypeStruct(wg32.shape, _BF16),
            jax.ShapeDtypeStruct(wu32.shape, _BF16),
            jax.ShapeDtypeStruct(wd32.shape, _BF16),
        ],
        scratch_shapes=[pltpu.VMEM((tm, d), _BF16)],
        compiler_params=params,
        name=name + "_first",
    )(x2d, gain, wg32, wu32, wd32, fgain)
    return pl.pallas_call(
        functools.partial(_ffn_rest_kernel, final_norm=final_norm),
        grid=(t // tm - 1, d_ff // tf),
        in_specs=[
            pl.BlockSpec((tm, d), lambda i, f: (i + 1, 0)),
            pl.BlockSpec((1, d), row),
            pl.BlockSpec((d, tf), lambda i, f: (0, f)),
            pl.BlockSpec((d, tf), lambda i, f: (0, f)),
            pl.BlockSpec((tf, d), lambda i, f: (f, 0)),
            pl.BlockSpec((1, d), row),
            pl.BlockSpec(memory_space=pl.ANY),
        ],
        out_specs=pl.BlockSpec((tm, d), lambda i, f: (i + 1, 0)),
        out_shape=jax.ShapeDtypeStruct((t, d), _F32),
        input_output_aliases={6: 0},
        scratch_shapes=[pltpu.VMEM((tm, d), _BF16)],
        compiler_params=params,
        name=name + "_rest",
    )(x2d, gain, wg, wu, wd, fgain, first)


def _in_proj_kernel(x_ref, gain_ref, pos_col_ref, pos_row_ref, freq_row_ref, freq_col_ref,
                    wqt_ref, wk_ref, wvt_ref, wu_ref,
                    qt_ref, k_ref, vt_ref, u_ref, kmean_ref):
    tm = x_ref.shape[1]
    h = _rms(x_ref[0], gain_ref[...]).astype(_BF16)

    ang_t = freq_col_ref[...] * pos_row_ref[0].astype(_F32)
    cos_t, sin_t = jnp.cos(ang_t), jnp.sin(ang_t)

    qt = _dot_nt(wqt_ref[...], h) * QUERY_SCALE
    for hh in range(N_ATTN_HEADS):
        base = hh * HEAD_DIM
        x1 = qt[base:base + ROT_HALF]
        x2 = qt[base + ROT_HALF:base + ROT_DIM]
        qt_ref[0, hh, 0:ROT_HALF, :] = (x1 * cos_t - x2 * sin_t).astype(_BF16)
        qt_ref[0, hh, ROT_HALF:ROT_DIM, :] = (x2 * cos_t + x1 * sin_t).astype(_BF16)
        qt_ref[0, hh, ROT_DIM:, :] = qt[base + ROT_DIM:base + HEAD_DIM].astype(_BF16)

    vt = _dot_nt(wvt_ref[...], h)
    ones = jnp.ones((V_ROWS - HEAD_DIM, tm), _BF16)
    for hh in range(N_ATTN_HEADS):
        vt_ref[0, hh, 0:HEAD_DIM, :] = vt[hh * HEAD_DIM:(hh + 1) * HEAD_DIM].astype(_BF16)
        vt_ref[0, hh, HEAD_DIM:, :] = ones

    u_ref[0] = _dot(h, wu_ref[...])

    ang = pos_col_ref[0].astype(_F32) * freq_row_ref[...]
    lane = lax.broadcasted_iota(jnp.int32, (1, HEAD_DIM), 1)
    cos_n = jnp.cos(ang)
    sin_n = jnp.where(lane < ROT_HALF, -jnp.sin(ang), jnp.sin(ang))
    k = _dot(h, wk_ref[...])
    nb = k_ref.shape[3] - HEAD_DIM
    row_block = (pl.program_id(1) * tm + lax.broadcasted_iota(jnp.int32, (tm, nb), 0)) // MOBA_BLOCK
    block_onehot = (row_block == lax.broadcasted_iota(jnp.int32, (tm, nb), 1)).astype(_F32).astype(_BF16)
    for hh in range(N_ATTN_HEADS):
        kh = k[:, hh * HEAD_DIM:(hh + 1) * HEAD_DIM]
        partner = jnp.where(lane < ROT_HALF,
                            pltpu.roll(kh, HEAD_DIM - ROT_HALF, 1),
                            pltpu.roll(kh, ROT_HALF, 1))
        kr = kh * cos_n + partner * sin_n
        k_ref[0, hh, :, 0:HEAD_DIM] = kr.astype(_BF16)
        k_ref[0, hh, :, HEAD_DIM:] = block_onehot
        for j in range(tm // MOBA_BLOCK):
            kmean_ref[0, j, :, hh * HEAD_DIM:(hh + 1) * HEAD_DIM] = jnp.mean(
                kr[j * MOBA_BLOCK:(j + 1) * MOBA_BLOCK], axis=0, keepdims=True)


def _in_proj(x1, gain, positions, freq_row, freq_col, wqt, wk, wvt, wu):
    b, s, d = x1.shape
    tm = PROJ_TOKEN_TILE
    assert s % tm == 0 and tm % MOBA_BLOCK == 0
    nb = s // MOBA_BLOCK
    bpt = tm // MOBA_BLOCK
    pos_col = positions.reshape(b, s, 1)
    pos_row = positions.reshape(b, 1, s)
    const2 = lambda bi, i: (0, 0)
    return pl.pallas_call(
        _in_proj_kernel,
        grid=(b, s // tm),
        in_specs=[
            pl.BlockSpec((1, tm, d), lambda bi, i: (bi, i, 0)),
            pl.BlockSpec((1, d), const2),
            pl.BlockSpec((1, tm, 1), lambda bi, i: (bi, i, 0)),
            pl.BlockSpec((1, 1, tm), lambda bi, i: (bi, 0, i)),
            pl.BlockSpec((1, HEAD_DIM), const2),
            pl.BlockSpec((ROT_HALF, 1), const2),
            _resident((ATTN_WIDTH, d), const2),
            _resident((d, ATTN_WIDTH), const2),
            _resident((ATTN_WIDTH, d), const2),
            _resident((d, POOL_WIDTH), const2),
        ],
        out_specs=[
            pl.BlockSpec((1, N_ATTN_HEADS, HEAD_DIM, tm), lambda bi, i: (bi, 0, 0, i)),
            pl.BlockSpec((1, N_ATTN_HEADS, tm, HEAD_DIM + nb), lambda bi, i: (bi, 0, i, 0)),
            pl.BlockSpec((1, N_ATTN_HEADS, V_ROWS, tm), lambda bi, i: (bi, 0, 0, i)),
            pl.BlockSpec((1, tm, POOL_WIDTH), lambda bi, i: (bi, i, 0)),
            pl.BlockSpec((1, bpt, 1, ATTN_WIDTH), lambda bi, i: (bi, i, 0, 0)),
        ],
        out_shape=[
            jax.ShapeDtypeStruct((b, N_ATTN_HEADS, HEAD_DIM, s), _BF16),
            jax.ShapeDtypeStruct((b, N_ATTN_HEADS, s, HEAD_DIM + nb), _BF16),
            jax.ShapeDtypeStruct((b, N_ATTN_HEADS, V_ROWS, s), _BF16),
            jax.ShapeDtypeStruct((b, s, POOL_WIDTH), _F32),
            jax.ShapeDtypeStruct((b, nb, 1, ATTN_WIDTH), _F32),
        ],
        compiler_params=pltpu.CompilerParams(
            dimension_semantics=("parallel", "parallel"),
            vmem_limit_bytes=V7X_VMEM_LIMIT_BYTES),
        name="in_proj",
    )(x1, gain, pos_col, pos_row, freq_row, freq_col, wqt, wk, wvt, wu)


def _moba_kernel(qt_ref, k_ref, vt_ref, kmean_ref, out_ref, qb_scr, acc_scr, s_even, s_odd, *, nb, hg):
    i = pl.program_id(2)
    blk = MOBA_BLOCK
    kb_id = lax.broadcasted_iota(jnp.int32, (nb, blk), 0)
    past = kb_id < i
    kb_f = kb_id.astype(_F32)

    def select_blocks(h):
        km = kmean_ref[0, :, h * HEAD_DIM:(h + 1) * HEAD_DIM]
        km_hi = km.astype(_BF16)
        km_lo = (km - km_hi.astype(_F32)).astype(_BF16)
        gate = _dot(km_hi, qt_ref[0, h]) + _dot(km_lo, qt_ref[0, h])
        gate = jnp.where(past, gate, NEG_INF)
        bias = jnp.full((nb, blk), NEG_INF, _F32)
        for _ in range(MOBA_TOPK):
            top = jnp.max(gate, axis=0, keepdims=True)
            first = jnp.min(jnp.where(gate == top, kb_f, float(nb)), axis=0, keepdims=True)
            hit = kb_f == first
            bias = jnp.where(hit, 0.0, bias)
            gate = jnp.where(hit, -jnp.inf, gate)
        bias = jnp.where(past, bias, NEG_INF)
        qb_scr[h, 0:HEAD_DIM, :] = qt_ref[0, h]
        qb_scr[h, HEAD_DIM:, :] = bias.astype(_BF16)

    def scores(h, n):
        kblk = k_ref[0, h, pl.ds(pl.multiple_of(n * blk, blk), blk), :]
        return _dot(kblk, qb_scr[h])

    def own_scores(h):
        kblk = k_ref[0, h, pl.ds(pl.multiple_of(i * blk, blk), blk), 0:HEAD_DIM]
        return _dot(kblk, qt_ref[0, h])

    def weighted_values(h, n, p):
        vblk = vt_ref[0, h, :, pl.ds(pl.multiple_of(n * blk, blk), blk)]
        return _dot(vblk, p)

    key_pos = lax.broadcasted_iota(jnp.int32, (blk, blk), 0)
    qry_pos = lax.broadcasted_iota(jnp.int32, (blk, blk), 1)
    causal = key_pos <= qry_pos
    m_runs, ps = [], []
    ss = [jnp.where(causal, own_scores(h), NEG_INF) for h in range(hg)]
    for h in range(hg):
        select_blocks(h)
    for h in range(hg):
        m_own = jnp.max(ss[h], axis=0, keepdims=True)
        m_runs.append(m_own)
        ps.append(jnp.exp2(ss[h] - m_own).astype(_BF16))
    for h in range(hg):
        acc_scr[h] = weighted_values(h, i, ps[h])
        s_even[h] = scores(h, 0)

    def step(n, s_cur, s_nxt, n_nxt, m_runs):
        m_out = []
        for h in range(hg):
            s_nxt[h] = scores(h, n_nxt)
            s = s_cur[h]
            m_new = jnp.maximum(m_runs[h], jnp.max(s, axis=0, keepdims=True))
            m_out.append(m_new)
            alpha = jnp.exp2(m_runs[h] - m_new)
            p = jnp.exp2(s - m_new).astype(_BF16)
            acc_scr[h] = alpha * acc_scr[h] + weighted_values(h, n, p)
        return tuple(m_out)

    def body(j, m_runs):
        m_runs = step(2 * j, s_even, s_odd, 2 * j + 1, m_runs)
        return step(2 * j + 1, s_odd, s_even, jnp.minimum(2 * j + 2, nb - 1), m_runs)

    lax.fori_loop(0, (i + 1) // 2, body, tuple(m_runs))
    for h in range(hg):
        acc = acc_scr[h]
        out_ref[0, :, h * HEAD_DIM:(h + 1) * HEAD_DIM] = jnp.transpose(
            acc[0:HEAD_DIM] / acc[HEAD_DIM:HEAD_DIM + 1]).astype(out_ref.dtype)


def _moba(qt, k, vt, kmean):
    b, nh, hd, s = qt.shape
    nb = s // MOBA_BLOCK
    hg = MOBA_HEADS_PER_STEP
    assert nh % hg == 0
    return pl.pallas_call(
        functools.partial(_moba_kernel, nb=nb, hg=hg),
        grid=(b, nh // hg, nb),
        in_specs=[
            pl.BlockSpec((1, hg, hd, MOBA_BLOCK), lambda bi, g, i: (bi, g, 0, i)),
            pl.BlockSpec((1, hg, s, hd + nb), lambda bi, g, i: (bi, g, 0, 0), pipeline_mode=pl.Buffered(1)),
            pl.BlockSpec((1, hg, V_ROWS, s), lambda bi, g, i: (bi, g, 0, 0), pipeline_mode=pl.Buffered(1)),
            pl.BlockSpec((1, nb, hg * hd), lambda bi, g, i: (bi, 0, g)),
        ],
        out_specs=pl.BlockSpec((1, MOBA_BLOCK, hg * hd), lambda bi, g, i: (bi, i, g)),
        out_shape=jax.ShapeDtypeStruct((b, s, nh * hd), _BF16),
        scratch_shapes=[pltpu.VMEM((hg, hd + nb, MOBA_BLOCK), _BF16),
                        pltpu.VMEM((hg, V_ROWS, MOBA_BLOCK), _F32),
                        pltpu.VMEM((hg, MOBA_BLOCK, MOBA_BLOCK), _F32),
                        pltpu.VMEM((hg, MOBA_BLOCK, MOBA_BLOCK), _F32)],
        compiler_params=pltpu.CompilerParams(
            dimension_semantics=("parallel", "parallel", "arbitrary"),
            vmem_limit_bytes=V7X_VMEM_LIMIT_BYTES),
        name="moba",
    )(qt, k, vt, kmean)


def _mix_out_kernel(x_ref, attn_ref, u_ref, halo_ref, pw_ref, ps_ref, wo_ref, out_ref, ubuf):
    tm = x_ref.shape[1]
    si = pl.program_id(1)
    ubuf[0:POOL_HALO, :] = jnp.where(si > 0, halo_ref[0], 0.0)
    ubuf[POOL_HALO:, :] = u_ref[0]

    t = si * tm + lax.broadcasted_iota(jnp.int32, (tm, 1), 0)
    mix = _dot(attn_ref[0], wo_ref[0:ATTN_WIDTH, :])
    for g, w in enumerate(POOL_WINDOWS):
        cols = slice(g * POOL_GROUP_DIM, (g + 1) * POOL_GROUP_DIM)
        own = ubuf[POOL_HALO:, cols]
        wsum = own
        for dlt in range(1, w):
            wsum = wsum + ubuf[POOL_HALO - dlt:POOL_HALO - dlt + tm, cols]
        count = jnp.minimum(t + 1, w).astype(_F32)
        pooled = wsum / count - own
        y = _dot(pooled.astype(_BF16), pw_ref[g]) * ps_ref[:, cols]
        row0 = ATTN_WIDTH + g * POOL_GROUP_DIM
        mix = mix + _dot(y.astype(_BF16), wo_ref[row0:row0 + POOL_GROUP_DIM, :])
    out_ref[0] = x_ref[0] + mix


def _mix_out(x1, attn, u, pool_w, pool_scale, w_out):
    b, s, d = x1.shape
    tm = MIX_TOKEN_TILE
    assert s % tm == 0 and tm % POOL_HALO == 0
    halo_per_tile = tm // POOL_HALO
    const2 = lambda bi, i: (0, 0)
    return pl.pallas_call(
        _mix_out_kernel,
        grid=(b, s // tm),
        in_specs=[
            pl.BlockSpec((1, tm, d), lambda bi, i: (bi, i, 0)),
            pl.BlockSpec((1, tm, ATTN_WIDTH), lambda bi, i: (bi, i, 0)),
            pl.BlockSpec((1, tm, POOL_WIDTH), lambda bi, i: (bi, i, 0)),
            pl.BlockSpec((1, POOL_HALO, POOL_WIDTH),
                         lambda bi, i: (bi, jnp.maximum(i * halo_per_tile - 1, 0), 0)),
            _resident(pool_w.shape, lambda bi, i: (0, 0, 0)),
            pl.BlockSpec((1, POOL_WIDTH), const2),
            _resident(w_out.shape, const2),
        ],
        out_specs=pl.BlockSpec((1, tm, d), lambda bi, i: (bi, i, 0)),
        out_shape=jax.ShapeDtypeStruct((b, s, d), _F32),
        scratch_shapes=[pltpu.VMEM((POOL_HALO + tm, POOL_WIDTH), _F32)],
        compiler_params=pltpu.CompilerParams(
            dimension_semantics=("parallel", "parallel"),
            vmem_limit_bytes=V7X_VMEM_LIMIT_BYTES),
        name="mix_out",
    )(x1, attn, u, u, pool_w, pool_scale, w_out)


def kernel(x, positions, norm_ffn1, w1_gate, w1_up, w1_down, norm_mix, w_in, pool_w, pool_scale,
           w_out, norm_ffn2, w2_gate, w2_up, w2_down, norm_final):
    b, s, d = x.shape
    depth = w_in.shape[0]
    inv_freq = ROPE_THETA ** (-jnp.arange(0, ROT_DIM, 2, dtype=_F32) / ROT_DIM)
    freq_row = jnp.concatenate([inv_freq, inv_freq, jnp.zeros((HEAD_DIM - ROT_DIM,), _F32)])[None, :]
    freq_col = inv_freq[:, None]
    fgain = norm_final[None, :]
    bf = lambda w: w.astype(_BF16)

    for l in range(depth):
        x1 = _ffn(x.reshape(b * s, d), norm_ffn1[l][None, :], w1_gate[l], w1_up[l], w1_down[l],
                  fgain, final_norm=False, name="ffn_pre").reshape(b, s, d)
        wq, wk, wv, wu = jnp.split(w_in[l], [ATTN_WIDTH, 2 * ATTN_WIDTH, 3 * ATTN_WIDTH], axis=-1)
        qt, k, vt, u, kmean = _in_proj(x1, norm_mix[l][None, :], positions, freq_row, freq_col,
                                       bf(wq.T), bf(wk), bf(wv.T), bf(wu))
        attn = _moba(qt, k, vt, kmean.reshape(b, s // MOBA_BLOCK, ATTN_WIDTH))
        x2 = _mix_out(x1, attn, u, bf(pool_w[l]), pool_scale[l][None, :], bf(w_out[l]))
        last = l == depth - 1
        x = _ffn(x2.reshape(b * s, d), norm_ffn2[l][None, :], w2_gate[l], w2_up[l], w2_down[l],
                 fgain, final_norm=last, name="ffn_post").reshape(b, s, d)
    return x
```

```python
import functools

import jax
import jax.numpy as jnp
from jax import lax
from jax.experimental import pallas as pl
from jax.experimental.pallas import tpu as pltpu

N_ATTN_HEADS = 8
HEAD_DIM = 128
ATTN_WIDTH = N_ATTN_HEADS * HEAD_DIM
POOL_WINDOWS = (2, 4, 8, 16)
POOL_GROUP_DIM = 256
POOL_WIDTH = POOL_GROUP_DIM * len(POOL_WINDOWS)
MOBA_BLOCK = 256
MOBA_TOPK = 3
ROT_DIM = HEAD_DIM // 4
ROT_HALF = ROT_DIM // 2
ROPE_THETA = 500000.0
EPS = 1e-6
NEG_INF = -1e30
LOG2_E = 1.4426950408889634
QUERY_SCALE = (HEAD_DIM ** -0.5) * LOG2_E
V_ROWS = HEAD_DIM + 16
POOL_HALO = 16

V7X_VMEM_LIMIT_BYTES = 58 * 1024 * 1024

FFN_TOKEN_TILE = 1024
FFN_FF_TILE = 512
FFN_FIRST_FF_TILE = 256
PROJ_TOKEN_TILE = 512
MIX_TOKEN_TILE = 512
MOBA_HEADS_PER_STEP = 8

_BF16 = jnp.bfloat16
_F32 = jnp.float32


def _rms(x, gain):
    inv = lax.rsqrt(jnp.mean(x * x, axis=-1, keepdims=True) + EPS)
    return x * inv * gain


def _dot(a, b):
    return jnp.dot(a, b, preferred_element_type=_F32)


def _dot_nt(a, b):
    return lax.dot_general(a, b, (((1,), (1,)), ((), ())), preferred_element_type=_F32)


def _resident(block_shape, index_map):
    return pl.BlockSpec(block_shape, index_map, pipeline_mode=pl.Buffered(1))


def _ffn_step(x_ref, gain_ref, load_weights, fgain_ref, out_ref, h_scr, final_norm):
    f = pl.program_id(1)

    @pl.when(f == 0)
    def _():
        x = x_ref[...]
        h_scr[...] = _rms(x, gain_ref[...]).astype(_BF16)
        out_ref[...] = x

    wg, wu, wd = load_weights()
    h = h_scr[...]
    g = _dot(h, wg)
    u = _dot(h, wu)
    a = (g * jax.nn.sigmoid(g) * u).astype(_BF16)
    out_ref[...] += 0.5 * _dot(a, wd)

    if final_norm:
        @pl.when(f == pl.num_programs(1) - 1)
        def _():
            out_ref[...] = _rms(out_ref[...], fgain_ref[...])


def _ffn_first_tile_kernel(x_ref, gain_ref, wg32_ref, wu32_ref, wd32_ref, fgain_ref,
                           out_ref, wg_ref, wu_ref, wd_ref, h_scr, *, final_norm):
    def load_weights():
        wg = wg32_ref[...].astype(_BF16)
        wu = wu32_ref[...].astype(_BF16)
        wd = wd32_ref[...].astype(_BF16)
        wg_ref[...] = wg
        wu_ref[...] = wu
        wd_ref[...] = wd
        return wg, wu, wd

    _ffn_step(x_ref, gain_ref, load_weights, fgain_ref, out_ref, h_scr, final_norm)


def _ffn_rest_kernel(x_ref, gain_ref, wg_ref, wu_ref, wd_ref, fgain_ref, first_hbm, out_ref, h_scr, copy_sem,
                     *, final_norm):
    i = pl.program_id(0)

    @pl.when((i == 0) & (pl.program_id(1) == 0))
    def _():
        copy = pltpu.make_async_copy(first_hbm, out_ref, copy_sem)
        copy.start()
        copy.wait()

    @pl.when(i > 0)
    def _():
        load_weights = lambda: (wg_ref[...], wu_ref[...], wd_ref[...])
        _ffn_step(x_ref, gain_ref, load_weights, fgain_ref, out_ref, h_scr, final_norm)


def _ffn(x2d, gain, wg32, wu32, wd32, fgain, *, final_norm, name):
    t, d = x2d.shape
    d_ff = wg32.shape[1]
    tm, tf, tf0 = FFN_TOKEN_TILE, FFN_FF_TILE, FFN_FIRST_FF_TILE
    assert t % tm == 0 and d_ff % tf == 0 and d_ff % tf0 == 0
    params = pltpu.CompilerParams(dimension_semantics=("parallel", "arbitrary"),
                                  vmem_limit_bytes=V7X_VMEM_LIMIT_BYTES)
    row = lambda i, f: (0, 0)
    first, wg, wu, wd = pl.pallas_call(
        functools.partial(_ffn_first_tile_kernel, final_norm=final_norm),
        grid=(1, d_ff // tf0),
        in_specs=[
            _resident((tm, d), lambda i, f: (0, 0)),
            pl.BlockSpec((1, d), row),
            pl.BlockSpec((d, tf0), lambda i, f: (0, f)),
            pl.BlockSpec((d, tf0), lambda i, f: (0, f)),
            pl.BlockSpec((tf0, d), lambda i, f: (f, 0)),
            pl.BlockSpec((1, d), row),
        ],
        out_specs=[
            pl.BlockSpec((tm, d), lambda i, f: (0, 0)),
            pl.BlockSpec((d, tf0), lambda i, f: (0, f)),
            pl.BlockSpec((d, tf0), lambda i, f: (0, f)),
            pl.BlockSpec((tf0, d), lambda i, f: (f, 0)),
        ],
        out_shape=[
            jax.ShapeDtypeStruct((tm, d), _F32),
            jax.ShapeDtypeStruct(wg32.shape, _BF16),
            jax.ShapeDtypeStruct(wu32.shape, _BF16),
            jax.ShapeDtypeStruct(wd32.shape, _BF16),
        ],
        scratch_shapes=[pltpu.VMEM((tm, d), _BF16)],
        compiler_params=params,
        name=name + "_first",
    )(x2d, gain, wg32, wu32, wd32, fgain)
    return pl.pallas_call(
        functools.partial(_ffn_rest_kernel, final_norm=final_norm),
        grid=(t // tm, d_ff // tf),
        in_specs=[
            pl.BlockSpec((tm, d), lambda i, f: (jnp.maximum(i, 1), 0)),
            pl.BlockSpec((1, d), row),
            pl.BlockSpec((d, tf), lambda i, f: (0, jnp.where(i == 0, 0, f))),
            pl.BlockSpec((d, tf), lambda i, f: (0, jnp.where(i == 0, 0, f))),
            pl.BlockSpec((tf, d), lambda i, f: (jnp.where(i == 0, 0, f), 0)),
            pl.BlockSpec((1, d), row),
            pl.BlockSpec(memory_space=pl.ANY),
        ],
        out_specs=pl.BlockSpec((tm, d), lambda i, f: (i, 0)),
        out_shape=jax.ShapeDtypeStruct((t, d), _F32),
        scratch_shapes=[pltpu.VMEM((tm, d), _BF16), pltpu.SemaphoreType.DMA(())],
        compiler_params=params,
        name=name + "_rest",
    )(x2d, gain, wg, wu, wd, fgain, first)


def _in_proj_kernel(x_ref, gain_ref, pos_ref, freq_ref, wqvt_ref, wku_ref,
                    qt_ref, k_ref, vt_ref, u_ref, kmean_ref):
    tm = x_ref.shape[1]
    h = _rms(x_ref[0], gain_ref[...]).astype(_BF16)

    qvt = _dot_nt(wqvt_ref[...], h)
    ku = _dot(h, wku_ref[...])

    ang_t = freq_ref[...] * pos_ref[0].astype(_F32)
    cos_t, sin_t = jnp.cos(ang_t), jnp.sin(ang_t)
    cos_q, sin_q = cos_t * QUERY_SCALE, sin_t * QUERY_SCALE
    for hh in range(N_ATTN_HEADS):
        base = hh * HEAD_DIM
        x1 = qvt[base:base + ROT_HALF]
        x2 = qvt[base + ROT_HALF:base + ROT_DIM]
        qt_ref[0, hh, 0:ROT_HALF, :] = (x1 * cos_q - x2 * sin_q).astype(_BF16)
        qt_ref[0, hh, ROT_HALF:ROT_DIM, :] = (x2 * cos_q + x1 * sin_q).astype(_BF16)
        qt_ref[0, hh, ROT_DIM:, :] = (qvt[base + ROT_DIM:base + HEAD_DIM] * QUERY_SCALE).astype(_BF16)

    ones = jnp.ones((V_ROWS - HEAD_DIM, tm), _BF16)
    for hh in range(N_ATTN_HEADS):
        base = ATTN_WIDTH + hh * HEAD_DIM
        vt_ref[0, hh, 0:HEAD_DIM, :] = qvt[base:base + HEAD_DIM].astype(_BF16)
        vt_ref[0, hh, HEAD_DIM:, :] = ones

    u_ref[0] = ku[:, ATTN_WIDTH:]
    k = ku[:, 0:ATTN_WIDTH]

    pad = HEAD_DIM - ROT_DIM
    cos_n = jnp.transpose(jnp.concatenate([cos_t, cos_t, jnp.ones((pad, tm), _F32)], axis=0))
    sin_n = jnp.transpose(jnp.concatenate([-sin_t, sin_t, jnp.zeros((pad, tm), _F32)], axis=0))
    lane = lax.broadcasted_iota(jnp.int32, (1, HEAD_DIM), 1)
    nb = k_ref.shape[3] - HEAD_DIM
    row_block = (pl.program_id(1) * tm + lax.broadcasted_iota(jnp.int32, (tm, nb), 0)) // MOBA_BLOCK
    block_onehot = (row_block == lax.broadcasted_iota(jnp.int32, (tm, nb), 1)).astype(_F32).astype(_BF16)
    for hh in range(N_ATTN_HEADS):
        kh = k[:, hh * HEAD_DIM:(hh + 1) * HEAD_DIM]
        partner = jnp.where(lane < ROT_HALF,
                            pltpu.roll(kh, HEAD_DIM - ROT_HALF, 1),
                            pltpu.roll(kh, ROT_HALF, 1))
        kr = kh * cos_n + partner * sin_n
        k_ref[0, hh, :, 0:HEAD_DIM] = kr.astype(_BF16)
        k_ref[0, hh, :, HEAD_DIM:] = block_onehot
        for j in range(tm // MOBA_BLOCK):
            kmean_ref[0, j, :, hh * HEAD_DIM:(hh + 1) * HEAD_DIM] = jnp.mean(
                kr[j * MOBA_BLOCK:(j + 1) * MOBA_BLOCK], axis=0, keepdims=True)


def _in_proj(x1, gain, positions, freq, wqvt, wku):
    b, s, d = x1.shape
    tm = PROJ_TOKEN_TILE
    assert s % tm == 0 and tm % MOBA_BLOCK == 0
    nb = s // MOBA_BLOCK
    bpt = tm // MOBA_BLOCK
    pos_row = positions.reshape(b, 1, s)
    const2 = lambda bi, i: (0, 0)
    return pl.pallas_call(
        _in_proj_kernel,
        grid=(b, s // tm),
        in_specs=[
            pl.BlockSpec((1, tm, d), lambda bi, i: (bi, i, 0)),
            pl.BlockSpec((1, d), const2),
            pl.BlockSpec((1, 1, tm), lambda bi, i: (bi, 0, i)),
            pl.BlockSpec((ROT_HALF, 1), const2),
            _resident((2 * ATTN_WIDTH, d), const2),
            _resident((d, ATTN_WIDTH + POOL_WIDTH), const2),
        ],
        out_specs=[
            pl.BlockSpec((1, N_ATTN_HEADS, HEAD_DIM, tm), lambda bi, i: (bi, 0, 0, i)),
            pl.BlockSpec((1, N_ATTN_HEADS, tm, HEAD_DIM + nb), lambda bi, i: (bi, 0, i, 0)),
            pl.BlockSpec((1, N_ATTN_HEADS, V_ROWS, tm), lambda bi, i: (bi, 0, 0, i)),
            pl.BlockSpec((1, tm, POOL_WIDTH), lambda bi, i: (bi, i, 0)),
            pl.BlockSpec((1, bpt, 1, ATTN_WIDTH), lambda bi, i: (bi, i, 0, 0)),
        ],
        out_shape=[
            jax.ShapeDtypeStruct((b, N_ATTN_HEADS, HEAD_DIM, s), _BF16),
            jax.ShapeDtypeStruct((b, N_ATTN_HEADS, s, HEAD_DIM + nb), _BF16),
            jax.ShapeDtypeStruct((b, N_ATTN_HEADS, V_ROWS, s), _BF16),
            jax.ShapeDtypeStruct((b, s, POOL_WIDTH), _F32),
            jax.ShapeDtypeStruct((b, nb, 1, ATTN_WIDTH), _F32),
        ],
        compiler_params=pltpu.CompilerParams(
            dimension_semantics=("parallel", "parallel"),
            vmem_limit_bytes=V7X_VMEM_LIMIT_BYTES),
        name="in_proj",
    )(x1, gain, pos_row, freq, wqvt, wku)


def _moba_kernel(qt_ref, k_ref, vt_ref, kmean_ref, out_ref, qb_scr, acc_scr, s_even, s_odd, *, nb, hg):
    i = pl.program_id(2)
    blk = MOBA_BLOCK
    kb_id = lax.broadcasted_iota(jnp.int32, (nb, blk), 0)
    past = kb_id < i
    kb_f = kb_id.astype(_F32)

    def select_blocks(h):
        km = kmean_ref[0, :, h * HEAD_DIM:(h + 1) * HEAD_DIM]
        km_hi = km.astype(_BF16)
        km_lo = (km - km_hi.astype(_F32)).astype(_BF16)
        gate = _dot(km_hi, qt_ref[0, h]) + _dot(km_lo, qt_ref[0, h])
        gate = jnp.where(past, gate, NEG_INF)
        bias = jnp.full((nb, blk), NEG_INF, _F32)
        for _ in range(MOBA_TOPK):
            top = jnp.max(gate, axis=0, keepdims=True)
            first = jnp.min(jnp.where(gate == top, kb_f, float(nb)), axis=0, keepdims=True)
            hit = kb_f == first
            bias = jnp.where(hit, 0.0, bias)
            gate = jnp.where(hit, -jnp.inf, gate)
        bias = jnp.where(past, bias, NEG_INF)
        qb_scr[h, 0:HEAD_DIM, :] = qt_ref[0, h]
        qb_scr[h, HEAD_DIM:, :] = bias.astype(_BF16)

    def scores(h, n):
        kblk = k_ref[0, h, pl.ds(pl.multiple_of(n * blk, blk), blk), :]
        return _dot(kblk, qb_scr[h])

    def own_scores(h):
        kblk = k_ref[0, h, pl.ds(pl.multiple_of(i * blk, blk), blk), 0:HEAD_DIM]
        return _dot(kblk, qt_ref[0, h])

    def weighted_values(h, n, p):
        vblk = vt_ref[0, h, :, pl.ds(pl.multiple_of(n * blk, blk), blk)]
        return _dot(vblk, p)

    key_pos = lax.broadcasted_iota(jnp.int32, (blk, blk), 0)
    qry_pos = lax.broadcasted_iota(jnp.int32, (blk, blk), 1)
    causal = key_pos <= qry_pos
    m_runs, ps = [], []
    ss = [jnp.where(causal, own_scores(h), NEG_INF) for h in range(hg)]
    for h in range(hg):
        select_blocks(h)
    for h in range(hg):
        m_own = jnp.max(ss[h], axis=0, keepdims=True)
        m_runs.append(m_own)
        ps.append(jnp.exp2(ss[h] - m_own).astype(_BF16))
    for h in range(hg):
        acc_scr[h] = weighted_values(h, i, ps[h])
        s_even[h] = scores(h, 0)

    def step(n, s_cur, s_nxt, n_nxt, m_runs):
        m_out = []
        for h in range(hg):
            s_nxt[h] = scores(h, n_nxt)
            s = s_cur[h]
            m_new = jnp.maximum(m_runs[h], jnp.max(s, axis=0, keepdims=True))
            m_out.append(m_new)
            alpha = jnp.exp2(m_runs[h] - m_new)
            p = jnp.exp2(s - m_new).astype(_BF16)
            acc_scr[h] = alpha * acc_scr[h] + weighted_values(h, n, p)
        return tuple(m_out)

    def body(j, m_runs):
        m_runs = step(2 * j, s_even, s_odd, 2 * j + 1, m_runs)
        return step(2 * j + 1, s_odd, s_even, jnp.minimum(2 * j + 2, nb - 1), m_runs)

    lax.fori_loop(0, (i + 1) // 2, body, tuple(m_runs))
    for h in range(hg):
        acc = acc_scr[h]
        out_ref[0, :, h * HEAD_DIM:(h + 1) * HEAD_DIM] = jnp.transpose(
            acc[0:HEAD_DIM] / acc[HEAD_DIM:HEAD_DIM + 1]).astype(out_ref.dtype)


def _moba(qt, k, vt, kmean):
    b, nh, hd, s = qt.shape
    nb = s // MOBA_BLOCK
    hg = MOBA_HEADS_PER_STEP
    assert nh % hg == 0
    return pl.pallas_call(
        functools.partial(_moba_kernel, nb=nb, hg=hg),
        grid=(b, nh // hg, nb),
        in_specs=[
            pl.BlockSpec((1, hg, hd, MOBA_BLOCK), lambda bi, g, i: (bi, g, 0, i)),
            pl.BlockSpec((1, hg, s, hd + nb), lambda bi, g, i: (bi, g, 0, 0), pipeline_mode=pl.Buffered(1)),
            pl.BlockSpec((1, hg, V_ROWS, s), lambda bi, g, i: (bi, g, 0, 0), pipeline_mode=pl.Buffered(1)),
            pl.BlockSpec((1, nb, hg * hd), lambda bi, g, i: (bi, 0, g)),
        ],
        out_specs=pl.BlockSpec((1, MOBA_BLOCK, hg * hd), lambda bi, g, i: (bi, i, g)),
        out_shape=jax.ShapeDtypeStruct((b, s, nh * hd), _BF16),
        scratch_shapes=[pltpu.VMEM((hg, hd + nb, MOBA_BLOCK), _BF16),
                        pltpu.VMEM((hg, V_ROWS, MOBA_BLOCK), _F32),
                        pltpu.VMEM((hg, MOBA_BLOCK, MOBA_BLOCK), _F32),
                        pltpu.VMEM((hg, MOBA_BLOCK, MOBA_BLOCK), _F32)],
        compiler_params=pltpu.CompilerParams(
            dimension_semantics=("parallel", "parallel", "arbitrary"),
            vmem_limit_bytes=V7X_VMEM_LIMIT_BYTES),
        name="moba",
    )(qt, k, vt, kmean)


def _mix_out_kernel(x_ref, attn_ref, u_ref, halo_ref, pw_ref, ps_ref, wo_ref, out_ref, ubuf, ybuf):
    tm = x_ref.shape[1]
    si = pl.program_id(1)
    ubuf[0:POOL_HALO, :] = jnp.where(si > 0, halo_ref[0], 0.0)
    ubuf[POOL_HALO:, :] = u_ref[0]

    t = si * tm + lax.broadcasted_iota(jnp.int32, (tm, 1), 0)
    mix = _dot(attn_ref[0], wo_ref[0:ATTN_WIDTH, :])
    for g, w in enumerate(POOL_WINDOWS):
        cols = slice(g * POOL_GROUP_DIM, (g + 1) * POOL_GROUP_DIM)
        ext = ubuf[:, cols]
        wsum = ext
        span = 1
        while span < w:
            wsum = wsum + pltpu.roll(wsum, span, 0)
            span *= 2
        own = ext[POOL_HALO:]
        count = jnp.minimum(t + 1, w).astype(_F32)
        pooled = wsum[POOL_HALO:] / count - own
        y = _dot(pooled.astype(_BF16), pw_ref[g]) * ps_ref[:, cols]
        ybuf[:, cols] = y.astype(_BF16)
    mix = mix + _dot(ybuf[...], wo_ref[ATTN_WIDTH:, :])
    out_ref[0] = x_ref[0] + mix


def _mix_out(x1, attn, u, pool_w, pool_scale, w_out):
    b, s, d = x1.shape
    tm = MIX_TOKEN_TILE
    assert s % tm == 0 and tm % POOL_HALO == 0
    halo_per_tile = tm // POOL_HALO
    const2 = lambda bi, i: (0, 0)
    return pl.pallas_call(
        _mix_out_kernel,
        grid=(b, s // tm),
        in_specs=[
            pl.BlockSpec((1, tm, d), lambda bi, i: (bi, i, 0)),
            pl.BlockSpec((1, tm, ATTN_WIDTH), lambda bi, i: (bi, i, 0)),
            pl.BlockSpec((1, tm, POOL_WIDTH), lambda bi, i: (bi, i, 0)),
            pl.BlockSpec((1, POOL_HALO, POOL_WIDTH),
                         lambda bi, i: (bi, jnp.maximum(i * halo_per_tile - 1, 0), 0)),
            _resident(pool_w.shape, lambda bi, i: (0, 0, 0)),
            pl.BlockSpec((1, POOL_WIDTH), const2),
            _resident(w_out.shape, const2),
        ],
        out_specs=pl.BlockSpec((1, tm, d), lambda bi, i: (bi, i, 0)),
        out_shape=jax.ShapeDtypeStruct((b, s, d), _F32),
        scratch_shapes=[pltpu.VMEM((POOL_HALO + tm, POOL_WIDTH), _F32), pltpu.VMEM((tm, POOL_WIDTH), _BF16)],
        compiler_params=pltpu.CompilerParams(
            dimension_semantics=("parallel", "parallel"),
            vmem_limit_bytes=V7X_VMEM_LIMIT_BYTES),
        name="mix_out",
    )(x1, attn, u, u, pool_w, pool_scale, w_out)


def kernel(x, positions, norm_ffn1, w1_gate, w1_up, w1_down, norm_mix, w_in, pool_w, pool_scale,
           w_out, norm_ffn2, w2_gate, w2_up, w2_down, norm_final):
    b, s, d = x.shape
    depth = w_in.shape[0]
    inv_freq = ROPE_THETA ** (-jnp.arange(0, ROT_DIM, 2, dtype=_F32) / ROT_DIM)
    freq = inv_freq[:, None]
    fgain = norm_final[None, :]
    bf = lambda w: w.astype(_BF16)

    for l in range(depth):
        x1 = _ffn(x.reshape(b * s, d), norm_ffn1[l][None, :], w1_gate[l], w1_up[l], w1_down[l],
                  fgain, final_norm=False, name="ffn_pre").reshape(b, s, d)
        wq, wk, wv, wu = jnp.split(w_in[l], [ATTN_WIDTH, 2 * ATTN_WIDTH, 3 * ATTN_WIDTH], axis=-1)
        wqvt = bf(jnp.concatenate([wq, wv], axis=1).T)
        wku = bf(jnp.concatenate([wk, wu], axis=1))
        qt, k, vt, u, kmean = _in_proj(x1, norm_mix[l][None, :], positions, freq, wqvt, wku)
        attn = _moba(qt, k, vt, kmean.reshape(b, s // MOBA_BLOCK, ATTN_WIDTH))
        x2 = _mix_out(x1, attn, u, bf(pool_w[l]), pool_scale[l][None, :], bf(w_out[l]))
        last = l == depth - 1
        x = _ffn(x2.reshape(b * s, d), norm_ffn2[l][None, :], w2_gate[l], w2_up[l], w2_down[l],
                 fgain, final_norm=last, name="ffn_post").reshape(b, s, d)
    return x
```

```python
import functools

import jax
import jax.numpy as jnp
from jax import lax
from jax.experimental import pallas as pl
from jax.experimental.pallas import tpu as pltpu

N_ATTN_HEADS = 8
HEAD_DIM = 128
ATTN_WIDTH = N_ATTN_HEADS * HEAD_DIM
POOL_WINDOWS = (2, 4, 8, 16)
POOL_GROUP_DIM = 256
POOL_WIDTH = POOL_GROUP_DIM * len(POOL_WINDOWS)
MOBA_BLOCK = 256
MOBA_TOPK = 3
ROT_DIM = HEAD_DIM // 4
ROT_HALF = ROT_DIM // 2
ROPE_THETA = 500000.0
EPS = 1e-6
NEG_INF = -1e30
LOG2_E = 1.4426950408889634
QUERY_SCALE = (HEAD_DIM ** -0.5) * LOG2_E
V_ROWS = HEAD_DIM + 16
POOL_HALO = 16

V7X_VMEM_LIMIT_BYTES = 58 * 1024 * 1024

FFN_TOKEN_TILE = 1024
FFN_FF_TILE = 512
FFN_FIRST_FF_TILE = 256
PROJ_TOKEN_TILE = 512
MIX_TOKEN_TILE = 512
MOBA_HEADS_PER_STEP = 8

_BF16 = jnp.bfloat16
_F32 = jnp.float32


def _rms(x, gain):
    inv = lax.rsqrt(jnp.mean(x * x, axis=-1, keepdims=True) + EPS)
    return x * inv * gain


def _dot(a, b):
    return jnp.dot(a, b, preferred_element_type=_F32)


def _dot_nt(a, b):
    return lax.dot_general(a, b, (((1,), (1,)), ((), ())), preferred_element_type=_F32)


def _resident(block_shape, index_map):
    return pl.BlockSpec(block_shape, index_map, pipeline_mode=pl.Buffered(1))


def _ffn_step(x_ref, gain_ref, load_weights, fgain_ref, out_ref, h_scr, final_norm):
    f = pl.program_id(1)

    @pl.when(f == 0)
    def _():
        x = x_ref[...]
        h_scr[...] = _rms(x, gain_ref[...]).astype(_BF16)
        out_ref[...] = x

    wg, wu, wd = load_weights()
    h = h_scr[...]
    g = _dot(h, wg)
    u = _dot(h, wu)
    a = (g * jax.nn.sigmoid(g) * (0.5 * u)).astype(_BF16)
    out_ref[...] += _dot(a, wd)

    if final_norm:
        @pl.when(f == pl.num_programs(1) - 1)
        def _():
            out_ref[...] = _rms(out_ref[...], fgain_ref[...])


def _ffn_first_tile_kernel(x_ref, gain_ref, wg32_ref, wu32_ref, wd32_ref, fgain_ref,
                           out_ref, wg_ref, wu_ref, wd_ref, h_scr, *, final_norm):
    def load_weights():
        wg = wg32_ref[...].astype(_BF16)
        wu = wu32_ref[...].astype(_BF16)
        wd = wd32_ref[...].astype(_BF16)
        wg_ref[...] = wg
        wu_ref[...] = wu
        wd_ref[...] = wd
        return wg, wu, wd

    _ffn_step(x_ref, gain_ref, load_weights, fgain_ref, out_ref, h_scr, final_norm)


def _ffn_rest_kernel(x_ref, gain_ref, wg_ref, wu_ref, wd_ref, fgain_ref, first_hbm, out_ref, h_scr, copy_sem,
                     *, final_norm):
    i = pl.program_id(0)

    @pl.when((i == 0) & (pl.program_id(1) == 0))
    def _():
        copy = pltpu.make_async_copy(first_hbm, out_ref, copy_sem)
        copy.start()
        copy.wait()

    @pl.when(i > 0)
    def _():
        load_weights = lambda: (wg_ref[...], wu_ref[...], wd_ref[...])
        _ffn_step(x_ref, gain_ref, load_weights, fgain_ref, out_ref, h_scr, final_norm)


def _ffn(x2d, gain, wg32, wu32, wd32, fgain, *, final_norm, name):
    t, d = x2d.shape
    d_ff = wg32.shape[1]
    tm, tf, tf0 = FFN_TOKEN_TILE, FFN_FF_TILE, FFN_FIRST_FF_TILE
    assert t % tm == 0 and d_ff % tf == 0 and d_ff % tf0 == 0
    params = pltpu.CompilerParams(dimension_semantics=("parallel", "arbitrary"),
                                  vmem_limit_bytes=V7X_VMEM_LIMIT_BYTES)
    row = lambda i, f: (0, 0)
    first, wg, wu, wd = pl.pallas_call(
        functools.partial(_ffn_first_tile_kernel, final_norm=final_norm),
        grid=(1, d_ff // tf0),
        in_specs=[
            _resident((tm, d), lambda i, f: (0, 0)),
            pl.BlockSpec((1, d), row),
            pl.BlockSpec((d, tf0), lambda i, f: (0, f)),
            pl.BlockSpec((d, tf0), lambda i, f: (0, f)),
            pl.BlockSpec((tf0, d), lambda i, f: (f, 0)),
            pl.BlockSpec((1, d), row),
        ],
        out_specs=[
            pl.BlockSpec((tm, d), lambda i, f: (0, 0)),
            pl.BlockSpec((d, tf0), lambda i, f: (0, f)),
            pl.BlockSpec((d, tf0), lambda i, f: (0, f)),
            pl.BlockSpec((tf0, d), lambda i, f: (f, 0)),
        ],
        out_shape=[
            jax.ShapeDtypeStruct((tm, d), _F32),
            jax.ShapeDtypeStruct(wg32.shape, _BF16),
            jax.ShapeDtypeStruct(wu32.shape, _BF16),
            jax.ShapeDtypeStruct(wd32.shape, _BF16),
        ],
        scratch_shapes=[pltpu.VMEM((tm, d), _BF16)],
        compiler_params=params,
        name=name + "_first",
    )(x2d, gain, wg32, wu32, wd32, fgain)
    return pl.pallas_call(
        functools.partial(_ffn_rest_kernel, final_norm=final_norm),
        grid=(t // tm, d_ff // tf),
        in_specs=[
            pl.BlockSpec((tm, d), lambda i, f: (jnp.maximum(i, 1), 0)),
            pl.BlockSpec((1, d), row),
            pl.BlockSpec((d, tf), lambda i, f: (0, jnp.where(i == 0, 0, f))),
            pl.BlockSpec((d, tf), lambda i, f: (0, jnp.where(i == 0, 0, f))),
            pl.BlockSpec((tf, d), lambda i, f: (jnp.where(i == 0, 0, f), 0)),
            pl.BlockSpec((1, d), row),
            pl.BlockSpec(memory_space=pl.ANY),
        ],
        out_specs=pl.BlockSpec((tm, d), lambda i, f: (i, 0)),
        out_shape=jax.ShapeDtypeStruct((t, d), _F32),
        scratch_shapes=[pltpu.VMEM((tm, d), _BF16), pltpu.SemaphoreType.DMA(())],
        compiler_params=params,
        name=name + "_rest",
    )(x2d, gain, wg, wu, wd, fgain, first)


def _in_proj_kernel(x_ref, gain_ref, pos_ref, freq_ref, wqvt_ref, wku_ref,
                    qt_ref, k_ref, vt_ref, u_ref, kmean_ref):
    tm = x_ref.shape[1]
    h = _rms(x_ref[0], gain_ref[...]).astype(_BF16)

    qvt = _dot_nt(wqvt_ref[...], h)
    ku = _dot(h, wku_ref[...])

    ang_t = freq_ref[...] * pos_ref[0].astype(_F32)
    cos_t, sin_t = jnp.cos(ang_t), jnp.sin(ang_t)
    cos_q, sin_q = cos_t * QUERY_SCALE, sin_t * QUERY_SCALE
    for hh in range(N_ATTN_HEADS):
        base = hh * HEAD_DIM
        x1 = qvt[base:base + ROT_HALF]
        x2 = qvt[base + ROT_HALF:base + ROT_DIM]
        qt_ref[0, hh, 0:ROT_HALF, :] = (x1 * cos_q - x2 * sin_q).astype(_BF16)
        qt_ref[0, hh, ROT_HALF:ROT_DIM, :] = (x2 * cos_q + x1 * sin_q).astype(_BF16)
        qt_ref[0, hh, ROT_DIM:, :] = (qvt[base + ROT_DIM:base + HEAD_DIM] * QUERY_SCALE).astype(_BF16)

    ones = jnp.ones((V_ROWS - HEAD_DIM, tm), _BF16)
    for hh in range(N_ATTN_HEADS):
        base = ATTN_WIDTH + hh * HEAD_DIM
        vt_ref[0, hh, 0:HEAD_DIM, :] = qvt[base:base + HEAD_DIM].astype(_BF16)
        vt_ref[0, hh, HEAD_DIM:, :] = ones

    u_ref[0] = ku[:, ATTN_WIDTH:]
    k = ku[:, 0:ATTN_WIDTH]

    pad = HEAD_DIM - ROT_DIM
    cos_n = jnp.transpose(jnp.concatenate([cos_t, cos_t, jnp.ones((pad, tm), _F32)], axis=0))
    sin_n = jnp.transpose(jnp.concatenate([-sin_t, sin_t, jnp.zeros((pad, tm), _F32)], axis=0))
    lane = lax.broadcasted_iota(jnp.int32, (1, HEAD_DIM), 1)
    for hh in range(N_ATTN_HEADS):
        kh = k[:, hh * HEAD_DIM:(hh + 1) * HEAD_DIM]
        partner = jnp.where(lane < ROT_HALF,
                            pltpu.roll(kh, HEAD_DIM - ROT_HALF, 1),
                            pltpu.roll(kh, ROT_HALF, 1))
        kr = kh * cos_n + partner * sin_n
        k_ref[0, hh] = kr.astype(_BF16)
        for j in range(tm // MOBA_BLOCK):
            kmean_ref[0, j, :, hh * HEAD_DIM:(hh + 1) * HEAD_DIM] = jnp.mean(
                kr[j * MOBA_BLOCK:(j + 1) * MOBA_BLOCK], axis=0, keepdims=True)


def _in_proj(x1, gain, positions, freq, wqvt, wku):
    b, s, d = x1.shape
    tm = PROJ_TOKEN_TILE
    assert s % tm == 0 and tm % MOBA_BLOCK == 0
    nb = s // MOBA_BLOCK
    bpt = tm // MOBA_BLOCK
    pos_row = positions.reshape(b, 1, s)
    const2 = lambda bi, i: (0, 0)
    return pl.pallas_call(
        _in_proj_kernel,
        grid=(b, s // tm),
        in_specs=[
            pl.BlockSpec((1, tm, d), lambda bi, i: (bi, i, 0)),
            pl.BlockSpec((1, d), const2),
            pl.BlockSpec((1, 1, tm), lambda bi, i: (bi, 0, i)),
            pl.BlockSpec((ROT_HALF, 1), const2),
            _resident((2 * ATTN_WIDTH, d), const2),
            _resident((d, ATTN_WIDTH + POOL_WIDTH), const2),
        ],
        out_specs=[
            pl.BlockSpec((1, N_ATTN_HEADS, HEAD_DIM, tm), lambda bi, i: (bi, 0, 0, i)),
            pl.BlockSpec((1, N_ATTN_HEADS, tm, HEAD_DIM), lambda bi, i: (bi, 0, i, 0)),
            pl.BlockSpec((1, N_ATTN_HEADS, V_ROWS, tm), lambda bi, i: (bi, 0, 0, i)),
            pl.BlockSpec((1, tm, POOL_WIDTH), lambda bi, i: (bi, i, 0)),
            pl.BlockSpec((1, bpt, 1, ATTN_WIDTH), lambda bi, i: (bi, i, 0, 0)),
        ],
        out_shape=[
            jax.ShapeDtypeStruct((b, N_ATTN_HEADS, HEAD_DIM, s), _BF16),
            jax.ShapeDtypeStruct((b, N_ATTN_HEADS, s, HEAD_DIM), _BF16),
            jax.ShapeDtypeStruct((b, N_ATTN_HEADS, V_ROWS, s), _BF16),
            jax.ShapeDtypeStruct((b, s, POOL_WIDTH), _F32),
            jax.ShapeDtypeStruct((b, nb, 1, ATTN_WIDTH), _F32),
        ],
        compiler_params=pltpu.CompilerParams(
            dimension_semantics=("parallel", "parallel"),
            vmem_limit_bytes=V7X_VMEM_LIMIT_BYTES),
        name="in_proj",
    )(x1, gain, pos_row, freq, wqvt, wku)


def _moba_kernel(qt_ref, k_ref, vt_ref, kmean_ref, out_ref, bias_scr, acc_scr, s_even, s_odd, *, nb, hg):
    i = pl.program_id(2)
    blk = MOBA_BLOCK
    kb_id = lax.broadcasted_iota(jnp.int32, (nb, blk), 0)
    past = kb_id < i
    kb_f = kb_id.astype(_F32)

    def select_blocks(h):
        km = kmean_ref[0, :, h * HEAD_DIM:(h + 1) * HEAD_DIM]
        km_hi = km.astype(_BF16)
        km_lo = (km - km_hi.astype(_F32)).astype(_BF16)
        gate = _dot(km_hi, qt_ref[0, h]) + _dot(km_lo, qt_ref[0, h])
        gate = jnp.where(past, gate, NEG_INF)
        bias = jnp.full((nb, blk), NEG_INF, _F32)
        for _ in range(MOBA_TOPK):
            top = jnp.max(gate, axis=0, keepdims=True)
            first = jnp.min(jnp.where(gate == top, kb_f, float(nb)), axis=0, keepdims=True)
            hit = kb_f == first
            bias = jnp.where(hit, 0.0, bias)
            gate = jnp.where(hit, -jnp.inf, gate)
        bias_scr[h] = jnp.where(past, bias, NEG_INF)

    def scores(h, n):
        kblk = k_ref[0, h, pl.ds(pl.multiple_of(n * blk, blk), blk), :]
        return _dot(kblk, qt_ref[0, h])

    def weighted_values(h, n, p):
        vblk = vt_ref[0, h, :, pl.ds(pl.multiple_of(n * blk, blk), blk)]
        return _dot(vblk, p)

    key_pos = lax.broadcasted_iota(jnp.int32, (blk, blk), 0)
    qry_pos = lax.broadcasted_iota(jnp.int32, (blk, blk), 1)
    causal = key_pos <= qry_pos
    m_runs, ps = [], []
    ss = [jnp.where(causal, scores(h, i), NEG_INF) for h in range(hg)]
    for h in range(hg):
        select_blocks(h)
    for h in range(hg):
        m_own = jnp.max(ss[h], axis=0, keepdims=True)
        m_runs.append(m_own)
        ps.append(jnp.exp2(ss[h] - m_own).astype(_BF16))
    for h in range(hg):
        acc_scr[h] = weighted_values(h, i, ps[h])
        s_even[h] = scores(h, 0)

    def step(n, s_cur, s_nxt, n_nxt, m_runs):
        m_out = []
        for h in range(hg):
            s_nxt[h] = scores(h, n_nxt)
            s = s_cur[h]
            b_n = bias_scr[h, pl.ds(n, 1), :]
            m_new = jnp.maximum(m_runs[h], jnp.max(s, axis=0, keepdims=True) + b_n)
            m_out.append(m_new)
            alpha = jnp.exp2(m_runs[h] - m_new)
            p = jnp.exp2(s - (m_new - b_n)).astype(_BF16)
            acc_scr[h] = alpha * acc_scr[h] + weighted_values(h, n, p)
        return tuple(m_out)

    def body(j, m_runs):
        m_runs = step(2 * j, s_even, s_odd, 2 * j + 1, m_runs)
        return step(2 * j + 1, s_odd, s_even, jnp.minimum(2 * j + 2, nb - 1), m_runs)

    lax.fori_loop(0, (i + 1) // 2, body, tuple(m_runs))
    for h in range(hg):
        acc = acc_scr[h]
        out_ref[0, :, h * HEAD_DIM:(h + 1) * HEAD_DIM] = jnp.transpose(
            acc[0:HEAD_DIM] / acc[HEAD_DIM:HEAD_DIM + 1]).astype(out_ref.dtype)


def _moba(qt, k, vt, kmean):
    b, nh, hd, s = qt.shape
    nb = s // MOBA_BLOCK
    hg = MOBA_HEADS_PER_STEP
    assert nh % hg == 0
    return pl.pallas_call(
        functools.partial(_moba_kernel, nb=nb, hg=hg),
        grid=(b, nh // hg, nb),
        in_specs=[
            pl.BlockSpec((1, hg, hd, MOBA_BLOCK), lambda bi, g, i: (bi, g, 0, i)),
            pl.BlockSpec((1, hg, s, hd), lambda bi, g, i: (bi, g, 0, 0)),
            pl.BlockSpec((1, hg, V_ROWS, s), lambda bi, g, i: (bi, g, 0, 0)),
            pl.BlockSpec((1, nb, hg * hd), lambda bi, g, i: (bi, 0, g)),
        ],
        out_specs=pl.BlockSpec((1, MOBA_BLOCK, hg * hd), lambda bi, g, i: (bi, i, g)),
        out_shape=jax.ShapeDtypeStruct((b, s, nh * hd), _BF16),
        scratch_shapes=[pltpu.VMEM((hg, nb, MOBA_BLOCK), _F32),
                        pltpu.VMEM((hg, V_ROWS, MOBA_BLOCK), _F32),
                        pltpu.VMEM((hg, MOBA_BLOCK, MOBA_BLOCK), _F32),
                        pltpu.VMEM((hg, MOBA_BLOCK, MOBA_BLOCK), _F32)],
        compiler_params=pltpu.CompilerParams(
            dimension_semantics=("parallel", "parallel", "arbitrary"),
            vmem_limit_bytes=V7X_VMEM_LIMIT_BYTES),
        name="moba",
    )(qt, k, vt, kmean)


def _mix_out_kernel(x_ref, attn_ref, u_ref, halo_ref, pw_ref, ps_ref, wo_ref, out_ref, ubuf, ybuf):
    tm = x_ref.shape[1]
    si = pl.program_id(1)
    ubuf[0:POOL_HALO, :] = jnp.where(si > 0, halo_ref[0], 0.0)
    ubuf[POOL_HALO:, :] = u_ref[0]

    t = si * tm + lax.broadcasted_iota(jnp.int32, (tm, 1), 0)
    mix = _dot(attn_ref[0], wo_ref[0:ATTN_WIDTH, :])
    for g, w in enumerate(POOL_WINDOWS):
        cols = slice(g * POOL_GROUP_DIM, (g + 1) * POOL_GROUP_DIM)
        ext = ubuf[:, cols]
        wsum = ext
        span = 1
        while span < w:
            wsum = wsum + pltpu.roll(wsum, span, 0)
            span *= 2
        own = ext[POOL_HALO:]
        count = jnp.minimum(t + 1, w).astype(_F32)
        pooled = wsum[POOL_HALO:] / count - own
        y = _dot(pooled.astype(_BF16), pw_ref[g]) * ps_ref[:, cols]
        ybuf[:, cols] = y.astype(_BF16)
    mix = mix + _dot(ybuf[...], wo_ref[ATTN_WIDTH:, :])
    out_ref[0] = x_ref[0] + mix


def _mix_out(x1, attn, u, pool_w, pool_scale, w_out):
    b, s, d = x1.shape
    tm = MIX_TOKEN_TILE
    assert s % tm == 0 and tm % POOL_HALO == 0
    halo_per_tile = tm // POOL_HALO
    const2 = lambda bi, i: (0, 0)
    return pl.pallas_call(
        _mix_out_kernel,
        grid=(b, s // tm),
        in_specs=[
            pl.BlockSpec((1, tm, d), lambda bi, i: (bi, i, 0)),
            pl.BlockSpec((1, tm, ATTN_WIDTH), lambda bi, i: (bi, i, 0)),
            pl.BlockSpec((1, tm, POOL_WIDTH), lambda bi, i: (bi, i, 0)),
            pl.BlockSpec((1, POOL_HALO, POOL_WIDTH),
                         lambda bi, i: (bi, jnp.maximum(i * halo_per_tile - 1, 0), 0)),
            _resident(pool_w.shape, lambda bi, i: (0, 0, 0)),
            pl.BlockSpec((1, POOL_WIDTH), const2),
            _resident(w_out.shape, const2),
        ],
        out_specs=pl.BlockSpec((1, tm, d), lambda bi, i: (bi, i, 0)),
        out_shape=jax.ShapeDtypeStruct((b, s, d), _F32),
        scratch_shapes=[pltpu.VMEM((POOL_HALO + tm, POOL_WIDTH), _F32), pltpu.VMEM((tm, POOL_WIDTH), _BF16)],
        compiler_params=pltpu.CompilerParams(
            dimension_semantics=("parallel", "parallel"),
            vmem_limit_bytes=V7X_VMEM_LIMIT_BYTES),
        name="mix_out",
    )(x1, attn, u, u, pool_w, pool_scale, w_out)


def kernel(x, positions, norm_ffn1, w1_gate, w1_up, w1_down, norm_mix, w_in, pool_w, pool_scale,
           w_out, norm_ffn2, w2_gate, w2_up, w2_down, norm_final):
    b, s, d = x.shape
    depth = w_in.shape[0]
    inv_freq = ROPE_THETA ** (-jnp.arange(0, ROT_DIM, 2, dtype=_F32) / ROT_DIM)
    freq = inv_freq[:, None]
    fgain = norm_final[None, :]
    bf = lambda w: w.astype(_BF16)

    for l in range(depth):
        x1 = _ffn(x.reshape(b * s, d), norm_ffn1[l][None, :], w1_gate[l], w1_up[l], w1_down[l],
                  fgain, final_norm=False, name="ffn_pre").reshape(b, s, d)
        wq, wk, wv, wu = jnp.split(w_in[l], [ATTN_WIDTH, 2 * ATTN_WIDTH, 3 * ATTN_WIDTH], axis=-1)
        wqvt = bf(jnp.concatenate([wq, wv], axis=1).T)
        wku = bf(jnp.concatenate([wk, wu], axis=1))
        qt, k, vt, u, kmean = _in_proj(x1, norm_mix[l][None, :], positions, freq, wqvt, wku)
        attn = _moba(qt, k, vt, kmean.reshape(b, s // MOBA_BLOCK, ATTN_WIDTH))
        x2 = _mix_out(x1, attn, u, bf(pool_w[l]), pool_scale[l][None, :], bf(w_out[l]))
        last = l == depth - 1
        x = _ffn(x2.reshape(b * s, d), norm_ffn2[l][None, :], w2_gate[l], w2_up[l], w2_down[l],
                 fgain, final_norm=last, name="ffn_post").reshape(b, s, d)
    return x
```

```python
import functools

import jax
import jax.numpy as jnp
from jax import lax
from jax.experimental import pallas as pl
from jax.experimental.pallas import tpu as pltpu

N_ATTN_HEADS = 8
HEAD_DIM = 128
ATTN_WIDTH = N_ATTN_HEADS * HEAD_DIM
POOL_WINDOWS = (2, 4, 8, 16)
POOL_GROUP_DIM = 256
POOL_WIDTH = POOL_GROUP_DIM * len(POOL_WINDOWS)
MOBA_BLOCK = 256
MOBA_TOPK = 3
ROT_DIM = HEAD_DIM // 4
ROT_HALF = ROT_DIM // 2
ROPE_THETA = 500000.0
EPS = 1e-6
NEG_INF = -1e30
LOG2_E = 1.4426950408889634
QUERY_SCALE = (HEAD_DIM ** -0.5) * LOG2_E
V_ROWS = HEAD_DIM + 16
POOL_HALO = 16

V7X_VMEM_LIMIT_BYTES = 58 * 1024 * 1024

FFN_TOKEN_TILE = 1024
FFN_FF_TILE = 512
FFN_FIRST_FF_TILE = 256
PROJ_TOKEN_TILE = 512
MIX_TOKEN_TILE = 512
MOBA_HEADS_PER_STEP = 8

_BF16 = jnp.bfloat16
_F32 = jnp.float32


def _rms(x, gain):
    inv = lax.rsqrt(jnp.mean(x * x, axis=-1, keepdims=True) + EPS)
    return x * inv * gain


def _dot(a, b):
    return jnp.dot(a, b, preferred_element_type=_F32)


def _dot_nt(a, b):
    return lax.dot_general(a, b, (((1,), (1,)), ((), ())), preferred_element_type=_F32)


def _resident(block_shape, index_map):
    return pl.BlockSpec(block_shape, index_map, pipeline_mode=pl.Buffered(1))


def _ffn_step(x_ref, gain_ref, load_weights, fgain_ref, out_ref, h_scr, final_norm):
    f = pl.program_id(1)

    @pl.when(f == 0)
    def _():
        x = x_ref[...]
        h_scr[...] = _rms(x, gain_ref[...]).astype(_BF16)
        out_ref[...] = x

    wg, wu, wd = load_weights()
    h = h_scr[...]
    g = _dot(h, wg)
    u = _dot(h, wu)
    a = (g * jax.nn.sigmoid(g) * (0.5 * u)).astype(_BF16)
    out_ref[...] += _dot(a, wd)

    if final_norm:
        @pl.when(f == pl.num_programs(1) - 1)
        def _():
            out_ref[...] = _rms(out_ref[...], fgain_ref[...])


def _ffn_first_tile_kernel(x_ref, gain_ref, wg32_ref, wu32_ref, wd32_ref, fgain_ref,
                           out_ref, wg_ref, wu_ref, wd_ref, h_scr, *, final_norm):
    def load_weights():
        wg = wg32_ref[...].astype(_BF16)
        wu = wu32_ref[...].astype(_BF16)
        wd = wd32_ref[...].astype(_BF16)
        wg_ref[...] = wg
        wu_ref[...] = wu
        wd_ref[...] = wd
        return wg, wu, wd

    _ffn_step(x_ref, gain_ref, load_weights, fgain_ref, out_ref, h_scr, final_norm)


def _ffn_rest_kernel(x_ref, gain_ref, wg_ref, wu_ref, wd_ref, fgain_ref, first_hbm, out_ref, h_scr, copy_sem,
                     *, final_norm):
    i = pl.program_id(0)

    @pl.when((i == 0) & (pl.program_id(1) == 0))
    def _():
        copy = pltpu.make_async_copy(first_hbm, out_ref, copy_sem)
        copy.start()
        copy.wait()

    @pl.when(i > 0)
    def _():
        load_weights = lambda: (wg_ref[...], wu_ref[...], wd_ref[...])
        _ffn_step(x_ref, gain_ref, load_weights, fgain_ref, out_ref, h_scr, final_norm)


def _ffn(x2d, gain, wg32, wu32, wd32, fgain, *, final_norm, name):
    t, d = x2d.shape
    d_ff = wg32.shape[1]
    tm, tf, tf0 = FFN_TOKEN_TILE, FFN_FF_TILE, FFN_FIRST_FF_TILE
    assert t % tm == 0 and d_ff % tf == 0 and d_ff % tf0 == 0
    params = pltpu.CompilerParams(dimension_semantics=("parallel", "arbitrary"),
                                  vmem_limit_bytes=V7X_VMEM_LIMIT_BYTES)
    row = lambda i, f: (0, 0)
    first, wg, wu, wd = pl.pallas_call(
        functools.partial(_ffn_first_tile_kernel, final_norm=final_norm),
        grid=(1, d_ff // tf0),
        in_specs=[
            _resident((tm, d), lambda i, f: (0, 0)),
            pl.BlockSpec((1, d), row),
            pl.BlockSpec((d, tf0), lambda i, f: (0, f)),
            pl.BlockSpec((d, tf0), lambda i, f: (0, f)),
            pl.BlockSpec((tf0, d), lambda i, f: (f, 0)),
            pl.BlockSpec((1, d), row),
        ],
        out_specs=[
            pl.BlockSpec((tm, d), lambda i, f: (0, 0)),
            pl.BlockSpec((d, tf0), lambda i, f: (0, f)),
            pl.BlockSpec((d, tf0), lambda i, f: (0, f)),
            pl.BlockSpec((tf0, d), lambda i, f: (f, 0)),
        ],
        out_shape=[
            jax.ShapeDtypeStruct((tm, d), _F32),
            jax.ShapeDtypeStruct(wg32.shape, _BF16),
            jax.ShapeDtypeStruct(wu32.shape, _BF16),
            jax.ShapeDtypeStruct(wd32.shape, _BF16),
        ],
        scratch_shapes=[pltpu.VMEM((tm, d), _BF16)],
        compiler_params=params,
        name=name + "_first",
    )(x2d, gain, wg32, wu32, wd32, fgain)
    return pl.pallas_call(
        functools.partial(_ffn_rest_kernel, final_norm=final_norm),
        grid=(t // tm, d_ff // tf),
        in_specs=[
            pl.BlockSpec((tm, d), lambda i, f: (jnp.maximum(i, 1), 0)),
            pl.BlockSpec((1, d), row),
            pl.BlockSpec((d, tf), lambda i, f: (0, jnp.where(i == 0, 0, f))),
            pl.BlockSpec((d, tf), lambda i, f: (0, jnp.where(i == 0, 0, f))),
            pl.BlockSpec((tf, d), lambda i, f: (jnp.where(i == 0, 0, f), 0)),
            pl.BlockSpec((1, d), row),
            pl.BlockSpec(memory_space=pl.ANY),
        ],
        out_specs=pl.BlockSpec((tm, d), lambda i, f: (i, 0)),
        out_shape=jax.ShapeDtypeStruct((t, d), _F32),
        scratch_shapes=[pltpu.VMEM((tm, d), _BF16), pltpu.SemaphoreType.DMA(())],
        compiler_params=params,
        name=name + "_rest",
    )(x2d, gain, wg, wu, wd, fgain, first)


def _in_proj_kernel(x_ref, gain_ref, pos_ref, freq_ref, wqvt_ref, wku_ref,
                    qt_ref, k_ref, vt_ref, u_ref, kmean_ref):
    tm = x_ref.shape[1]
    h = _rms(x_ref[0], gain_ref[...]).astype(_BF16)

    ku = _dot(h, wku_ref[...])
    qvt = _dot_nt(wqvt_ref[...], h)

    ang_t = freq_ref[...] * pos_ref[0].astype(_F32)
    cos_t, sin_t = jnp.cos(ang_t), jnp.sin(ang_t)
    pad = HEAD_DIM - ROT_DIM
    cos_n = jnp.transpose(jnp.concatenate([cos_t, cos_t, jnp.ones((pad, tm), _F32)], axis=0))
    sin_n = jnp.transpose(jnp.concatenate([-sin_t, sin_t, jnp.zeros((pad, tm), _F32)], axis=0))
    lane = lax.broadcasted_iota(jnp.int32, (1, HEAD_DIM), 1)

    u_ref[0] = ku[:, ATTN_WIDTH:]
    for hh in range(N_ATTN_HEADS):
        kh = ku[:, hh * HEAD_DIM:(hh + 1) * HEAD_DIM]
        partner = jnp.where(lane < ROT_HALF,
                            pltpu.roll(kh, HEAD_DIM - ROT_HALF, 1),
                            pltpu.roll(kh, ROT_HALF, 1))
        kr = kh * cos_n + partner * sin_n
        k_ref[0, hh] = kr.astype(_BF16)
        for j in range(tm // MOBA_BLOCK):
            kmean_ref[0, j, :, hh * HEAD_DIM:(hh + 1) * HEAD_DIM] = jnp.mean(
                kr[j * MOBA_BLOCK:(j + 1) * MOBA_BLOCK], axis=0, keepdims=True)

    cos_q, sin_q = cos_t * QUERY_SCALE, sin_t * QUERY_SCALE
    for hh in range(N_ATTN_HEADS):
        base = hh * HEAD_DIM
        x1 = qvt[base:base + ROT_HALF]
        x2 = qvt[base + ROT_HALF:base + ROT_DIM]
        qt_ref[0, hh, 0:ROT_HALF, :] = (x1 * cos_q - x2 * sin_q).astype(_BF16)
        qt_ref[0, hh, ROT_HALF:ROT_DIM, :] = (x2 * cos_q + x1 * sin_q).astype(_BF16)
        qt_ref[0, hh, ROT_DIM:, :] = (qvt[base + ROT_DIM:base + HEAD_DIM] * QUERY_SCALE).astype(_BF16)

    ones = jnp.ones((V_ROWS - HEAD_DIM, tm), _BF16)
    for hh in range(N_ATTN_HEADS):
        base = ATTN_WIDTH + hh * HEAD_DIM
        vt_ref[0, hh, 0:HEAD_DIM, :] = qvt[base:base + HEAD_DIM].astype(_BF16)
        vt_ref[0, hh, HEAD_DIM:, :] = ones


def _in_proj(x1, gain, positions, freq, wqvt, wku):
    b, s, d = x1.shape
    tm = PROJ_TOKEN_TILE
    assert s % tm == 0 and tm % MOBA_BLOCK == 0
    nb = s // MOBA_BLOCK
    bpt = tm // MOBA_BLOCK
    pos_row = positions.reshape(b, 1, s)
    const2 = lambda bi, i: (0, 0)
    return pl.pallas_call(
        _in_proj_kernel,
        grid=(b, s // tm),
        in_specs=[
            pl.BlockSpec((1, tm, d), lambda bi, i: (bi, i, 0)),
            pl.BlockSpec((1, d), const2),
            pl.BlockSpec((1, 1, tm), lambda bi, i: (bi, 0, i)),
            pl.BlockSpec((ROT_HALF, 1), const2),
            _resident((2 * ATTN_WIDTH, d), const2),
            _resident((d, ATTN_WIDTH + POOL_WIDTH), const2),
        ],
        out_specs=[
            pl.BlockSpec((1, N_ATTN_HEADS, HEAD_DIM, tm), lambda bi, i: (bi, 0, 0, i)),
            pl.BlockSpec((1, N_ATTN_HEADS, tm, HEAD_DIM), lambda bi, i: (bi, 0, i, 0)),
            pl.BlockSpec((1, N_ATTN_HEADS, V_ROWS, tm), lambda bi, i: (bi, 0, 0, i)),
            pl.BlockSpec((1, tm, POOL_WIDTH), lambda bi, i: (bi, i, 0)),
            pl.BlockSpec((1, bpt, 1, ATTN_WIDTH), lambda bi, i: (bi, i, 0, 0)),
        ],
        out_shape=[
            jax.ShapeDtypeStruct((b, N_ATTN_HEADS, HEAD_DIM, s), _BF16),
            jax.ShapeDtypeStruct((b, N_ATTN_HEADS, s, HEAD_DIM), _BF16),
            jax.ShapeDtypeStruct((b, N_ATTN_HEADS, V_ROWS, s), _BF16),
            jax.ShapeDtypeStruct((b, s, POOL_WIDTH), _F32),
            jax.ShapeDtypeStruct((b, nb, 1, ATTN_WIDTH), _F32),
        ],
        compiler_params=pltpu.CompilerParams(
            dimension_semantics=("parallel", "parallel"),
            vmem_limit_bytes=V7X_VMEM_LIMIT_BYTES),
        name="in_proj",
    )(x1, gain, pos_row, freq, wqvt, wku)


def _moba_kernel(qt_ref, k_ref, vt_ref, kmean_ref, out_ref, bias_scr, acc_scr, m_scr, s_even, s_odd,
                 *, nb, hg):
    i = pl.program_id(2)
    blk = MOBA_BLOCK
    kb_id = lax.broadcasted_iota(jnp.int32, (nb, blk), 0)
    past = kb_id < i
    kb_f = kb_id.astype(_F32)

    def select_blocks(h):
        km = kmean_ref[0, :, h * HEAD_DIM:(h + 1) * HEAD_DIM]
        km_hi = km.astype(_BF16)
        km_lo = (km - km_hi.astype(_F32)).astype(_BF16)
        gate = _dot(km_hi, qt_ref[0, h]) + _dot(km_lo, qt_ref[0, h])
        gate = jnp.where(past, gate, NEG_INF)
        bias = jnp.full((nb, blk), NEG_INF, _F32)
        for _ in range(MOBA_TOPK):
            top = jnp.max(gate, axis=0, keepdims=True)
            first = jnp.min(jnp.where(gate == top, kb_f, float(nb)), axis=0, keepdims=True)
            hit = kb_f == first
            bias = jnp.where(hit, 0.0, bias)
            gate = jnp.where(hit, -jnp.inf, gate)
        bias_scr[h] = jnp.where(past, bias, NEG_INF)

    def scores(h, n):
        kblk = k_ref[0, h, pl.ds(pl.multiple_of(n * blk, blk), blk), :]
        return _dot(kblk, qt_ref[0, h])

    def weighted_values(h, n, p):
        vblk = vt_ref[0, h, :, pl.ds(pl.multiple_of(n * blk, blk), blk)]
        return _dot(vblk, p)

    key_pos = lax.broadcasted_iota(jnp.int32, (blk, blk), 0)
    qry_pos = lax.broadcasted_iota(jnp.int32, (blk, blk), 1)
    causal = key_pos <= qry_pos
    ps = []
    ss = [jnp.where(causal, scores(h, i), NEG_INF) for h in range(hg)]
    for h in range(hg):
        select_blocks(h)
    for h in range(hg):
        m_own = jnp.max(ss[h], axis=0, keepdims=True)
        m_scr[h] = m_own
        ps.append(jnp.exp2(ss[h] - m_own).astype(_BF16))
    for h in range(hg):
        acc_scr[h] = weighted_values(h, i, ps[h])
        s_even[h] = scores(h, 0)

    def step(n, s_cur, s_nxt, stage_next):
        n_nxt = jnp.minimum(n + 1, nb - 1)
        for h in range(hg):
            if stage_next:
                s_nxt[h] = scores(h, n_nxt)
            s = s_cur[h]
            b_n = bias_scr[h, pl.ds(n, 1), :]
            m_run = m_scr[h]
            m_new = jnp.maximum(m_run, jnp.max(s, axis=0, keepdims=True) + b_n)
            m_scr[h] = m_new
            alpha = jnp.exp2(m_run - m_new)
            p = jnp.exp2(s - (m_new - b_n)).astype(_BF16)
            acc_scr[h] = alpha * acc_scr[h] + weighted_values(h, n, p)

    def run(n0, count, stage_last=True):
        for c in range(count):
            bufs = (s_even, s_odd) if c % 2 == 0 else (s_odd, s_even)
            step(n0 + c, *bufs, stage_next=stage_last or c + 1 < count)

    def body(j, carry):
        run(4 * j, 4)
        return carry

    lax.fori_loop(0, i // 4, body, 0)

    @pl.when(i % 4 >= 2)
    def _():
        run((i // 4) * 4, 2)

    @pl.when(i % 2 == 1)
    def _():
        run(i - 1, 1, stage_last=False)

    for h in range(hg):
        acc = acc_scr[h]
        out_ref[0, :, h * HEAD_DIM:(h + 1) * HEAD_DIM] = jnp.transpose(
            acc[0:HEAD_DIM] / acc[HEAD_DIM:HEAD_DIM + 1]).astype(out_ref.dtype)


def _moba(qt, k, vt, kmean):
    b, nh, hd, s = qt.shape
    nb = s // MOBA_BLOCK
    hg = MOBA_HEADS_PER_STEP
    assert nh % hg == 0
    return pl.pallas_call(
        functools.partial(_moba_kernel, nb=nb, hg=hg),
        grid=(b, nh // hg, nb),
        in_specs=[
            pl.BlockSpec((1, hg, hd, MOBA_BLOCK), lambda bi, g, i: (bi, g, 0, i)),
            pl.BlockSpec((1, hg, s, hd), lambda bi, g, i: (bi, g, 0, 0)),
            pl.BlockSpec((1, hg, V_ROWS, s), lambda bi, g, i: (bi, g, 0, 0)),
            pl.BlockSpec((1, nb, hg * hd), lambda bi, g, i: (bi, 0, g)),
        ],
        out_specs=pl.BlockSpec((1, MOBA_BLOCK, hg * hd), lambda bi, g, i: (bi, i, g)),
        out_shape=jax.ShapeDtypeStruct((b, s, nh * hd), _BF16),
        scratch_shapes=[pltpu.VMEM((hg, nb, MOBA_BLOCK), _F32),
                        pltpu.VMEM((hg, V_ROWS, MOBA_BLOCK), _F32),
                        pltpu.VMEM((hg, 1, MOBA_BLOCK), _F32),
                        pltpu.VMEM((hg, MOBA_BLOCK, MOBA_BLOCK), _F32),
                        pltpu.VMEM((hg, MOBA_BLOCK, MOBA_BLOCK), _F32)],
        compiler_params=pltpu.CompilerParams(
            dimension_semantics=("parallel", "parallel", "arbitrary"),
            vmem_limit_bytes=V7X_VMEM_LIMIT_BYTES),
        name="moba",
    )(qt, k, vt, kmean)


def _mix_out_kernel(x_ref, attn_ref, u_ref, halo_ref, pw_ref, ps_ref, wo_ref, out_ref, ubuf, ybuf):
    tm = x_ref.shape[1]
    si = pl.program_id(1)
    ubuf[0:POOL_HALO, :] = jnp.where(si > 0, halo_ref[0], 0.0)
    ubuf[POOL_HALO:, :] = u_ref[0]

    t = si * tm + lax.broadcasted_iota(jnp.int32, (tm, 1), 0)
    mix = _dot(attn_ref[0], wo_ref[0:ATTN_WIDTH, :])
    for g, w in enumerate(POOL_WINDOWS):
        cols = slice(g * POOL_GROUP_DIM, (g + 1) * POOL_GROUP_DIM)
        ext = ubuf[:, cols]
        wsum = ext
        span = 1
        while span < w:
            wsum = wsum + pltpu.roll(wsum, span, 0)
            span *= 2
        own = ext[POOL_HALO:]
        count = jnp.minimum(t + 1, w).astype(_F32)
        pooled = wsum[POOL_HALO:] / count - own
        y = _dot(pooled.astype(_BF16), pw_ref[g]) * ps_ref[:, cols]
        ybuf[:, cols] = y.astype(_BF16)
    mix = mix + _dot(ybuf[...], wo_ref[ATTN_WIDTH:, :])
    out_ref[0] = x_ref[0] + mix


def _mix_out(x1, attn, u, pool_w, pool_scale, w_out):
    b, s, d = x1.shape
    tm = MIX_TOKEN_TILE
    assert s % tm == 0 and tm % POOL_HALO == 0
    halo_per_tile = tm // POOL_HALO
    const2 = lambda bi, i: (0, 0)
    return pl.pallas_call(
        _mix_out_kernel,
        grid=(b, s // tm),
        in_specs=[
            pl.BlockSpec((1, tm, d), lambda bi, i: (bi, i, 0)),
            pl.BlockSpec((1, tm, ATTN_WIDTH), lambda bi, i: (bi, i, 0)),
            pl.BlockSpec((1, tm, POOL_WIDTH), lambda bi, i: (bi, i, 0)),
            pl.BlockSpec((1, POOL_HALO, POOL_WIDTH),
                         lambda bi, i: (bi, jnp.maximum(i * halo_per_tile - 1, 0), 0)),
            _resident(pool_w.shape, lambda bi, i: (0, 0, 0)),
            pl.BlockSpec((1, POOL_WIDTH), const2),
            _resident(w_out.shape, const2),
        ],
        out_specs=pl.BlockSpec((1, tm, d), lambda bi, i: (bi, i, 0)),
        out_shape=jax.ShapeDtypeStruct((b, s, d), _F32),
        scratch_shapes=[pltpu.VMEM((POOL_HALO + tm, POOL_WIDTH), _F32), pltpu.VMEM((tm, POOL_WIDTH), _BF16)],
        compiler_params=pltpu.CompilerParams(
            dimension_semantics=("parallel", "parallel"),
            vmem_limit_bytes=V7X_VMEM_LIMIT_BYTES),
        name="mix_out",
    )(x1, attn, u, u, pool_w, pool_scale, w_out)


def kernel(x, positions, norm_ffn1, w1_gate, w1_up, w1_down, norm_mix, w_in, pool_w, pool_scale,
           w_out, norm_ffn2, w2_gate, w2_up, w2_down, norm_final):
    b, s, d = x.shape
    depth = w_in.shape[0]
    inv_freq = ROPE_THETA ** (-jnp.arange(0, ROT_DIM, 2, dtype=_F32) / ROT_DIM)
    freq = inv_freq[:, None]
    fgain = norm_final[None, :]
    bf = lambda w: w.astype(_BF16)

    for l in range(depth):
        x1 = _ffn(x.reshape(b * s, d), norm_ffn1[l][None, :], w1_gate[l], w1_up[l], w1_down[l],
                  fgain, final_norm=False, name="ffn_pre").reshape(b, s, d)
        wq, wk, wv, wu = jnp.split(w_in[l], [ATTN_WIDTH, 2 * ATTN_WIDTH, 3 * ATTN_WIDTH], axis=-1)
        wqvt = bf(jnp.concatenate([wq, wv], axis=1).T)
        wku = bf(jnp.concatenate([wk, wu], axis=1))
        qt, k, vt, u, kmean = _in_proj(x1, norm_mix[l][None, :], positions, freq, wqvt, wku)
        attn = _moba(qt, k, vt, kmean.reshape(b, s // MOBA_BLOCK, ATTN_WIDTH))
        x2 = _mix_out(x1, attn, u, bf(pool_w[l]), pool_scale[l][None, :], bf(w_out[l]))
        last = l == depth - 1
        x = _ffn(x2.reshape(b * s, d), norm_ffn2[l][None, :], w2_gate[l], w2_up[l], w2_down[l],
                 fgain, final_norm=last, name="ffn_post").reshape(b, s, d)
    return x
```

```python
import functools

import jax
import jax.numpy as jnp
from jax import lax
from jax.experimental import pallas as pl
from jax.experimental.pallas import tpu as pltpu

N_ATTN_HEADS = 8
HEAD_DIM = 128
ATTN_WIDTH = N_ATTN_HEADS * HEAD_DIM
POOL_WINDOWS = (2, 4, 8, 16)
POOL_GROUP_DIM = 256
POOL_WIDTH = POOL_GROUP_DIM * len(POOL_WINDOWS)
MOBA_BLOCK = 256
MOBA_TOPK = 3
ROT_DIM = HEAD_DIM // 4
ROT_HALF = ROT_DIM // 2
ROPE_THETA = 500000.0
EPS = 1e-6
NEG_INF = -1e30
LOG2_E = 1.4426950408889634
QUERY_SCALE = (HEAD_DIM ** -0.5) * LOG2_E
V_ROWS = HEAD_DIM + 16
POOL_HALO = 16

V7X_VMEM_LIMIT_BYTES = 58 * 1024 * 1024

FFN_TOKEN_TILE = 1024
FFN_FF_TILE = 512
FFN_FIRST_FF_TILE = 256
PROJ_TOKEN_TILE = 512
MIX_TOKEN_TILE = 512
MOBA_HEADS_PER_STEP = 8

_BF16 = jnp.bfloat16
_F32 = jnp.float32


def _rms(x, gain):
    inv = lax.rsqrt(jnp.mean(x * x, axis=-1, keepdims=True) + EPS)
    return x * inv * gain


def _dot(a, b):
    return jnp.dot(a, b, preferred_element_type=_F32)


def _dot_nt(a, b):
    return lax.dot_general(a, b, (((1,), (1,)), ((), ())), preferred_element_type=_F32)


def _resident(block_shape, index_map):
    return pl.BlockSpec(block_shape, index_map, pipeline_mode=pl.Buffered(1))


def _ffn_step(x_ref, gain_ref, load_weights, fgain_ref, out_ref, h_scr, final_norm):
    f = pl.program_id(1)

    @pl.when(f == 0)
    def _():
        x = x_ref[...]
        h_scr[...] = _rms(x, gain_ref[...]).astype(_BF16)
        out_ref[...] = x

    wg, wu, wd = load_weights()
    h = h_scr[...]
    g = _dot(h, wg)
    u = _dot(h, wu)
    a = (g * jax.nn.sigmoid(g) * (0.5 * u)).astype(_BF16)
    out_ref[...] += _dot(a, wd)

    if final_norm:
        @pl.when(f == pl.num_programs(1) - 1)
        def _():
            out_ref[...] = _rms(out_ref[...], fgain_ref[...])


def _ffn_first_tile_kernel(x_ref, gain_ref, wg32_ref, wu32_ref, wd32_ref, fgain_ref,
                           out_ref, wg_ref, wu_ref, wd_ref, h_scr, *, final_norm):
    def load_weights():
        wg = wg32_ref[...].astype(_BF16)
        wu = wu32_ref[...].astype(_BF16)
        wd = wd32_ref[...].astype(_BF16)
        wg_ref[...] = wg
        wu_ref[...] = wu
        wd_ref[...] = wd
        return wg, wu, wd

    _ffn_step(x_ref, gain_ref, load_weights, fgain_ref, out_ref, h_scr, final_norm)


def _ffn_rest_kernel(x_hbm, gain_ref, wg_hbm, wu_hbm, wd_hbm, fgain_ref, first_hbm, out_ref,
                     xbuf, hbuf, wgbuf, wubuf, wdbuf, x_sem, g_sem, u_sem, d_sem, first_sem,
                     *, final_norm, n_chunks):
    i = pl.program_id(0)
    last = pl.num_programs(0) - 1
    tm, tf = xbuf.shape[1], wgbuf.shape[2]
    xs = i % 2
    xn = 1 - xs
    base = ((i - 1) * n_chunks) % 2

    def x_copy(tile, slot):
        return pltpu.make_async_copy(x_hbm.at[pl.ds(tile * tm, tm), :], xbuf.at[slot], x_sem.at[slot])

    def gate_copy(f, slot):
        return pltpu.make_async_copy(wg_hbm.at[:, pl.ds(f * tf, tf)], wgbuf.at[slot], g_sem.at[slot])

    def up_copy(f, slot):
        return pltpu.make_async_copy(wu_hbm.at[:, pl.ds(f * tf, tf)], wubuf.at[slot], u_sem.at[slot])

    def down_copy(f, slot):
        return pltpu.make_async_copy(wd_hbm.at[pl.ds(f * tf, tf), :], wdbuf.at[slot], d_sem.at[slot])

    def normalise(slot):
        hbuf[slot] = _rms(xbuf[slot], gain_ref[...]).astype(_BF16)

    @pl.when(i == 0)
    def _():
        first = pltpu.make_async_copy(first_hbm, out_ref, first_sem)
        first.start()
        x_copy(1, 1).start()
        for cp in (gate_copy(0, 0), up_copy(0, 0), gate_copy(1, 1), up_copy(1, 1), down_copy(0, 0)):
            cp.start()
        x_copy(1, 1).wait()
        normalise(1)
        gate_copy(0, 0).wait()
        up_copy(0, 0).wait()
        first.wait()

    @pl.when(i > 0)
    def _():
        for f in range(n_chunks):
            slot = (base + f) % 2
            other = 1 - slot
            f1, f2 = (f + 1) % n_chunks, (f + 2) % n_chunks
            g = _dot(hbuf[xs], wgbuf[slot])
            u = _dot(hbuf[xs], wubuf[slot])

            down_copy(f, slot).wait()

            def advance(f1=f1, other=other):
                gate_copy(f1, other).wait()
                up_copy(f1, other).wait()
                down_copy(f1, other).start()

            def refill(f2=f2, slot=slot):
                gate_copy(f2, slot).start()
                up_copy(f2, slot).start()

            if f + 1 < n_chunks:
                advance()
            else:
                pl.when(i < last)(advance)
            if f + 2 < n_chunks:
                refill()
            else:
                pl.when(i < last)(refill)
            if f == 1:
                pl.when(i < last)(lambda: x_copy(i + 1, xn).start())
            if f == n_chunks - 2:
                pl.when(i < last)(lambda: x_copy(i + 1, xn).wait())
            if f == n_chunks - 1:
                normalise(xn)

            a = (g * jax.nn.sigmoid(g) * (0.5 * u)).astype(_BF16)
            if f == 0:
                out_ref[...] = xbuf[xs] + _dot(a, wdbuf[slot])
            else:
                out_ref[...] += _dot(a, wdbuf[slot])

        if final_norm:
            out_ref[...] = _rms(out_ref[...], fgain_ref[...])


def _ffn(x2d, gain, wg32, wu32, wd32, fgain, *, final_norm, name):
    t, d = x2d.shape
    d_ff = wg32.shape[1]
    tm, tf, tf0 = FFN_TOKEN_TILE, FFN_FF_TILE, FFN_FIRST_FF_TILE
    assert t % tm == 0 and d_ff % tf == 0 and d_ff % tf0 == 0
    params = pltpu.CompilerParams(dimension_semantics=("parallel", "arbitrary"),
                                  vmem_limit_bytes=V7X_VMEM_LIMIT_BYTES)
    row = lambda i, f: (0, 0)
    first, wg, wu, wd = pl.pallas_call(
        functools.partial(_ffn_first_tile_kernel, final_norm=final_norm),
        grid=(1, d_ff // tf0),
        in_specs=[
            _resident((tm, d), lambda i, f: (0, 0)),
            pl.BlockSpec((1, d), row),
            pl.BlockSpec((d, tf0), lambda i, f: (0, f)),
            pl.BlockSpec((d, tf0), lambda i, f: (0, f)),
            pl.BlockSpec((tf0, d), lambda i, f: (f, 0)),
            pl.BlockSpec((1, d), row),
        ],
        out_specs=[
            pl.BlockSpec((tm, d), lambda i, f: (0, 0)),
            pl.BlockSpec((d, tf0), lambda i, f: (0, f)),
            pl.BlockSpec((d, tf0), lambda i, f: (0, f)),
            pl.BlockSpec((tf0, d), lambda i, f: (f, 0)),
        ],
        out_shape=[
            jax.ShapeDtypeStruct((tm, d), _F32),
            jax.ShapeDtypeStruct(wg32.shape, _BF16),
            jax.ShapeDtypeStruct(wu32.shape, _BF16),
            jax.ShapeDtypeStruct(wd32.shape, _BF16),
        ],
        scratch_shapes=[pltpu.VMEM((tm, d), _BF16)],
        compiler_params=params,
        name=name + "_first",
    )(x2d, gain, wg32, wu32, wd32, fgain)
    return pl.pallas_call(
        functools.partial(_ffn_rest_kernel, final_norm=final_norm, n_chunks=d_ff // tf),
        grid=(t // tm,),
        in_specs=[
            pl.BlockSpec(memory_space=pl.ANY),
            pl.BlockSpec((1, d), lambda i: (0, 0)),
            pl.BlockSpec(memory_space=pl.ANY),
            pl.BlockSpec(memory_space=pl.ANY),
            pl.BlockSpec(memory_space=pl.ANY),
            pl.BlockSpec((1, d), lambda i: (0, 0)),
            pl.BlockSpec(memory_space=pl.ANY),
        ],
        out_specs=pl.BlockSpec((tm, d), lambda i: (i, 0)),
        out_shape=jax.ShapeDtypeStruct((t, d), _F32),
        scratch_shapes=[
            pltpu.VMEM((2, tm, d), _F32), pltpu.VMEM((2, tm, d), _BF16),
            pltpu.VMEM((2, d, tf), _BF16), pltpu.VMEM((2, d, tf), _BF16), pltpu.VMEM((2, tf, d), _BF16),
            pltpu.SemaphoreType.DMA((2,)), pltpu.SemaphoreType.DMA((2,)), pltpu.SemaphoreType.DMA((2,)),
            pltpu.SemaphoreType.DMA((2,)), pltpu.SemaphoreType.DMA(()),
        ],
        compiler_params=pltpu.CompilerParams(dimension_semantics=("arbitrary",),
                                             vmem_limit_bytes=V7X_VMEM_LIMIT_BYTES),
        name=name + "_rest",
    )(x2d, gain, wg, wu, wd, fgain, first)


def _in_proj_kernel(x_ref, gain_ref, pos_ref, freq_ref, wqvt_ref, wku_ref,
                    qt_ref, k_ref, vt_ref, u_ref, kmean_ref):
    tm = x_ref.shape[1]
    h = _rms(x_ref[0], gain_ref[...]).astype(_BF16)

    ku = _dot(h, wku_ref[...])
    qvt = _dot_nt(wqvt_ref[...], h)

    ang_t = freq_ref[...] * pos_ref[0].astype(_F32)
    cos_t, sin_t = jnp.cos(ang_t), jnp.sin(ang_t)
    pad = HEAD_DIM - ROT_DIM
    cos_n = jnp.transpose(jnp.concatenate([cos_t, cos_t, jnp.ones((pad, tm), _F32)], axis=0))
    sin_n = jnp.transpose(jnp.concatenate([-sin_t, sin_t, jnp.zeros((pad, tm), _F32)], axis=0))
    lane = lax.broadcasted_iota(jnp.int32, (1, HEAD_DIM), 1)

    u_ref[0] = ku[:, ATTN_WIDTH:]
    for hh in range(N_ATTN_HEADS):
        kh = ku[:, hh * HEAD_DIM:(hh + 1) * HEAD_DIM]
        partner = jnp.where(lane < ROT_HALF,
                            pltpu.roll(kh, HEAD_DIM - ROT_HALF, 1),
                            pltpu.roll(kh, ROT_HALF, 1))
        kr = kh * cos_n + partner * sin_n
        k_ref[0, hh] = kr.astype(_BF16)
        for j in range(tm // MOBA_BLOCK):
            kmean_ref[0, j, :, hh * HEAD_DIM:(hh + 1) * HEAD_DIM] = jnp.mean(
                kr[j * MOBA_BLOCK:(j + 1) * MOBA_BLOCK], axis=0, keepdims=True)

    cos_q, sin_q = cos_t * QUERY_SCALE, sin_t * QUERY_SCALE
    for hh in range(N_ATTN_HEADS):
        base = hh * HEAD_DIM
        x1 = qvt[base:base + ROT_HALF]
        x2 = qvt[base + ROT_HALF:base + ROT_DIM]
        qt_ref[0, hh, 0:ROT_HALF, :] = (x1 * cos_q - x2 * sin_q).astype(_BF16)
        qt_ref[0, hh, ROT_HALF:ROT_DIM, :] = (x2 * cos_q + x1 * sin_q).astype(_BF16)
        qt_ref[0, hh, ROT_DIM:, :] = (qvt[base + ROT_DIM:base + HEAD_DIM] * QUERY_SCALE).astype(_BF16)

    ones = jnp.ones((V_ROWS - HEAD_DIM, tm), _BF16)
    for hh in range(N_ATTN_HEADS):
        base = ATTN_WIDTH + hh * HEAD_DIM
        vt_ref[0, hh, 0:HEAD_DIM, :] = qvt[base:base + HEAD_DIM].astype(_BF16)
        vt_ref[0, hh, HEAD_DIM:, :] = ones


def _in_proj(x1, gain, positions, freq, wqvt, wku):
    b, s, d = x1.shape
    tm = PROJ_TOKEN_TILE
    assert s % tm == 0 and tm % MOBA_BLOCK == 0
    nb = s // MOBA_BLOCK
    bpt = tm // MOBA_BLOCK
    pos_row = positions.reshape(b, 1, s)
    const2 = lambda bi, i: (0, 0)
    return pl.pallas_call(
        _in_proj_kernel,
        grid=(b, s // tm),
        in_specs=[
            pl.BlockSpec((1, tm, d), lambda bi, i: (bi, i, 0)),
            pl.BlockSpec((1, d), const2),
            pl.BlockSpec((1, 1, tm), lambda bi, i: (bi, 0, i)),
            pl.BlockSpec((ROT_HALF, 1), const2),
            _resident((2 * ATTN_WIDTH, d), const2),
            _resident((d, ATTN_WIDTH + POOL_WIDTH), const2),
        ],
        out_specs=[
            pl.BlockSpec((1, N_ATTN_HEADS, HEAD_DIM, tm), lambda bi, i: (bi, 0, 0, i)),
            pl.BlockSpec((1, N_ATTN_HEADS, tm, HEAD_DIM), lambda bi, i: (bi, 0, i, 0)),
            pl.BlockSpec((1, N_ATTN_HEADS, V_ROWS, tm), lambda bi, i: (bi, 0, 0, i)),
            pl.BlockSpec((1, tm, POOL_WIDTH), lambda bi, i: (bi, i, 0)),
            pl.BlockSpec((1, bpt, 1, ATTN_WIDTH), lambda bi, i: (bi, i, 0, 0)),
        ],
        out_shape=[
            jax.ShapeDtypeStruct((b, N_ATTN_HEADS, HEAD_DIM, s), _BF16),
            jax.ShapeDtypeStruct((b, N_ATTN_HEADS, s, HEAD_DIM), _BF16),
            jax.ShapeDtypeStruct((b, N_ATTN_HEADS, V_ROWS, s), _BF16),
            jax.ShapeDtypeStruct((b, s, POOL_WIDTH), _F32),
            jax.ShapeDtypeStruct((b, nb, 1, ATTN_WIDTH), _F32),
        ],
        compiler_params=pltpu.CompilerParams(
            dimension_semantics=("parallel", "parallel"),
            vmem_limit_bytes=V7X_VMEM_LIMIT_BYTES),
        name="in_proj",
    )(x1, gain, pos_row, freq, wqvt, wku)


def _moba_kernel(qt_ref, k_ref, vt_ref, kmean_ref, out_ref, bias_scr, acc_scr, m_scr, s_even, s_odd,
                 *, nb, hg):
    i = pl.program_id(2)
    blk = MOBA_BLOCK
    kb_id = lax.broadcasted_iota(jnp.int32, (nb, blk), 0)
    past = kb_id < i
    kb_f = kb_id.astype(_F32)

    def select_blocks(h):
        km = kmean_ref[0, :, h * HEAD_DIM:(h + 1) * HEAD_DIM]
        km_hi = km.astype(_BF16)
        km_lo = (km - km_hi.astype(_F32)).astype(_BF16)
        gate = _dot(km_hi, qt_ref[0, h]) + _dot(km_lo, qt_ref[0, h])
        gate = jnp.where(past, gate, NEG_INF)
        bias = jnp.full((nb, blk), NEG_INF, _F32)
        for _ in range(MOBA_TOPK):
            top = jnp.max(gate, axis=0, keepdims=True)
            first = jnp.min(jnp.where(gate == top, kb_f, float(nb)), axis=0, keepdims=True)
            hit = kb_f == first
            bias = jnp.where(hit, 0.0, bias)
            gate = jnp.where(hit, -jnp.inf, gate)
        bias_scr[h] = jnp.where(past, bias, NEG_INF)

    def scores(h, n):
        kblk = k_ref[0, h, pl.ds(pl.multiple_of(n * blk, blk), blk), :]
        return _dot(kblk, qt_ref[0, h])

    def weighted_values(h, n, p):
        vblk = vt_ref[0, h, :, pl.ds(pl.multiple_of(n * blk, blk), blk)]
        return _dot(vblk, p)

    key_pos = lax.broadcasted_iota(jnp.int32, (blk, blk), 0)
    qry_pos = lax.broadcasted_iota(jnp.int32, (blk, blk), 1)
    causal = key_pos <= qry_pos
    ps = []
    ss = [jnp.where(causal, scores(h, i), NEG_INF) for h in range(hg)]
    for h in range(hg):
        select_blocks(h)
    for h in range(hg):
        m_own = jnp.max(ss[h], axis=0, keepdims=True)
        m_scr[h] = m_own
        ps.append(jnp.exp2(ss[h] - m_own).astype(_BF16))
    for h in range(hg):
        acc_scr[h] = weighted_values(h, i, ps[h])
        s_even[h] = scores(h, 0)

    def step(n, s_cur, s_nxt, stage_next):
        n_nxt = jnp.minimum(n + 1, nb - 1)
        for h in range(hg):
            if stage_next:
                s_nxt[h] = scores(h, n_nxt)
            s = s_cur[h]
            b_n = bias_scr[h, pl.ds(n, 1), :]
            m_run = m_scr[h]
            m_new = jnp.maximum(m_run, jnp.max(s, axis=0, keepdims=True) + b_n)
            m_scr[h] = m_new
            alpha = jnp.exp2(m_run - m_new)
            p = jnp.exp2(s - (m_new - b_n)).astype(_BF16)
            acc_scr[h] = alpha * acc_scr[h] + weighted_values(h, n, p)

    def run(n0, count, stage_last=True):
        for c in range(count):
            bufs = (s_even, s_odd) if c % 2 == 0 else (s_odd, s_even)
            step(n0 + c, *bufs, stage_next=stage_last or c + 1 < count)

    def body(j, carry):
        run(4 * j, 4)
        return carry

    lax.fori_loop(0, i // 4, body, 0)

    @pl.when(i % 4 >= 2)
    def _():
        run((i // 4) * 4, 2)

    @pl.when(i % 2 == 1)
    def _():
        run(i - 1, 1, stage_last=False)

    for h in range(hg):
        acc = acc_scr[h]
        out_ref[0, :, h * HEAD_DIM:(h + 1) * HEAD_DIM] = jnp.transpose(
            acc[0:HEAD_DIM] / acc[HEAD_DIM:HEAD_DIM + 1]).astype(out_ref.dtype)


def _moba(qt, k, vt, kmean):
    b, nh, hd, s = qt.shape
    nb = s // MOBA_BLOCK
    hg = MOBA_HEADS_PER_STEP
    assert nh % hg == 0
    return pl.pallas_call(
        functools.partial(_moba_kernel, nb=nb, hg=hg),
        grid=(b, nh // hg, nb),
        in_specs=[
            pl.BlockSpec((1, hg, hd, MOBA_BLOCK), lambda bi, g, i: (bi, g, 0, i)),
            pl.BlockSpec((1, hg, s, hd), lambda bi, g, i: (bi, g, 0, 0)),
            pl.BlockSpec((1, hg, V_ROWS, s), lambda bi, g, i: (bi, g, 0, 0)),
            pl.BlockSpec((1, nb, hg * hd), lambda bi, g, i: (bi, 0, g)),
        ],
        out_specs=pl.BlockSpec((1, MOBA_BLOCK, hg * hd), lambda bi, g, i: (bi, i, g)),
        out_shape=jax.ShapeDtypeStruct((b, s, nh * hd), _BF16),
        scratch_shapes=[pltpu.VMEM((hg, nb, MOBA_BLOCK), _F32),
                        pltpu.VMEM((hg, V_ROWS, MOBA_BLOCK), _F32),
                        pltpu.VMEM((hg, 1, MOBA_BLOCK), _F32),
                        pltpu.VMEM((hg, MOBA_BLOCK, MOBA_BLOCK), _F32),
                        pltpu.VMEM((hg, MOBA_BLOCK, MOBA_BLOCK), _F32)],
        compiler_params=pltpu.CompilerParams(
            dimension_semantics=("parallel", "parallel", "arbitrary"),
            vmem_limit_bytes=V7X_VMEM_LIMIT_BYTES),
        name="moba",
    )(qt, k, vt, kmean)


def _mix_out_kernel(x_ref, attn_ref, u_ref, halo_ref, pw_ref, ps_ref, wo_ref, out_ref, ubuf, ybuf):
    tm = x_ref.shape[1]
    si = pl.program_id(1)
    ubuf[0:POOL_HALO, :] = jnp.where(si > 0, halo_ref[0], 0.0)
    ubuf[POOL_HALO:, :] = u_ref[0]

    t = si * tm + lax.broadcasted_iota(jnp.int32, (tm, 1), 0)
    mix = _dot(attn_ref[0], wo_ref[0:ATTN_WIDTH, :])
    for g, w in enumerate(POOL_WINDOWS):
        cols = slice(g * POOL_GROUP_DIM, (g + 1) * POOL_GROUP_DIM)
        ext = ubuf[:, cols]
        wsum = ext
        span = 1
        while span < w:
            wsum = wsum + pltpu.roll(wsum, span, 0)
            span *= 2
        own = ext[POOL_HALO:]
        count = jnp.minimum(t + 1, w).astype(_F32)
        pooled = wsum[POOL_HALO:] / count - own
        y = _dot(pooled.astype(_BF16), pw_ref[g]) * ps_ref[:, cols]
        ybuf[:, cols] = y.astype(_BF16)
    mix = mix + _dot(ybuf[...], wo_ref[ATTN_WIDTH:, :])
    out_ref[0] = x_ref[0] + mix


def _mix_out(x1, attn, u, pool_w, pool_scale, w_out):
    b, s, d = x1.shape
    tm = MIX_TOKEN_TILE
    assert s % tm == 0 and tm % POOL_HALO == 0
    halo_per_tile = tm // POOL_HALO
    const2 = lambda bi, i: (0, 0)
    return pl.pallas_call(
        _mix_out_kernel,
        grid=(b, s // tm),
        in_specs=[
            pl.BlockSpec((1, tm, d), lambda bi, i: (bi, i, 0)),
            pl.BlockSpec((1, tm, ATTN_WIDTH), lambda bi, i: (bi, i, 0)),
            pl.BlockSpec((1, tm, POOL_WIDTH), lambda bi, i: (bi, i, 0)),
            pl.BlockSpec((1, POOL_HALO, POOL_WIDTH),
                         lambda bi, i: (bi, jnp.maximum(i * halo_per_tile - 1, 0), 0)),
            _resident(pool_w.shape, lambda bi, i: (0, 0, 0)),
            pl.BlockSpec((1, POOL_WIDTH), const2),
            _resident(w_out.shape, const2),
        ],
        out_specs=pl.BlockSpec((1, tm, d), lambda bi, i: (bi, i, 0)),
        out_shape=jax.ShapeDtypeStruct((b, s, d), _F32),
        scratch_shapes=[pltpu.VMEM((POOL_HALO + tm, POOL_WIDTH), _F32), pltpu.VMEM((tm, POOL_WIDTH), _BF16)],
        compiler_params=pltpu.CompilerParams(
            dimension_semantics=("parallel", "parallel"),
            vmem_limit_bytes=V7X_VMEM_LIMIT_BYTES),
        name="mix_out",
    )(x1, attn, u, u, pool_w, pool_scale, w_out)


def kernel(x, positions, norm_ffn1, w1_gate, w1_up, w1_down, norm_mix, w_in, pool_w, pool_scale,
           w_out, norm_ffn2, w2_gate, w2_up, w2_down, norm_final):
    b, s, d = x.shape
    depth = w_in.shape[0]
    inv_freq = ROPE_THETA ** (-jnp.arange(0, ROT_DIM, 2, dtype=_F32) / ROT_DIM)
    freq = inv_freq[:, None]
    fgain = norm_final[None, :]
    bf = lambda w: w.astype(_BF16)

    for l in range(depth):
        x1 = _ffn(x.reshape(b * s, d), norm_ffn1[l][None, :], w1_gate[l], w1_up[l], w1_down[l],
                  fgain, final_norm=False, name="ffn_pre").reshape(b, s, d)
        wq, wk, wv, wu = jnp.split(w_in[l], [ATTN_WIDTH, 2 * ATTN_WIDTH, 3 * ATTN_WIDTH], axis=-1)
        wqvt = bf(jnp.concatenate([wq, wv], axis=1).T)
        wku = bf(jnp.concatenate([wk, wu], axis=1))
        qt, k, vt, u, kmean = _in_proj(x1, norm_mix[l][None, :], positions, freq, wqvt, wku)
        attn = _moba(qt, k, vt, kmean.reshape(b, s // MOBA_BLOCK, ATTN_WIDTH))
        x2 = _mix_out(x1, attn, u, bf(pool_w[l]), pool_scale[l][None, :], bf(w_out[l]))
        last = l == depth - 1
        x = _ffn(x2.reshape(b * s, d), norm_ffn2[l][None, :], w2_gate[l], w2_up[l], w2_down[l],
                 fgain, final_norm=last, name="ffn_post").reshape(b, s, d)
    return x
```

```python
import functools

import jax
import jax.numpy as jnp
from jax import lax
from jax.experimental import pallas as pl
from jax.experimental.pallas import tpu as pltpu

N_ATTN_HEADS = 8
HEAD_DIM = 128
ATTN_WIDTH = N_ATTN_HEADS * HEAD_DIM
POOL_WINDOWS = (2, 4, 8, 16)
POOL_GROUP_DIM = 256
POOL_WIDTH = POOL_GROUP_DIM * len(POOL_WINDOWS)
MOBA_BLOCK = 256
MOBA_TOPK = 3
ROT_DIM = HEAD_DIM // 4
ROT_HALF = ROT_DIM // 2
ROPE_THETA = 500000.0
EPS = 1e-6
NEG_INF = -1e30
LOG2_E = 1.4426950408889634
QUERY_SCALE = (HEAD_DIM ** -0.5) * LOG2_E
V7X_BF16_ROWS_PER_VREG = 16
V_ROWS = HEAD_DIM + V7X_BF16_ROWS_PER_VREG
POOL_HALO = 16

V7X_VMEM_LIMIT_BYTES = 58 * 1024 * 1024

FFN_TOKEN_TILE = 1024
FFN_FF_TILE = 512
FFN_FIRST_FF_TILE = 256
PROJ_TOKEN_TILE = 512
MIX_TOKEN_TILE = 512
MOBA_HEADS_PER_STEP = 8

_BF16 = jnp.bfloat16
_F32 = jnp.float32


def _rms(x, gain):
    inv = lax.rsqrt(jnp.mean(x * x, axis=-1, keepdims=True) + EPS)
    return x * inv * gain


def _dot(a, b):
    return jnp.dot(a, b, preferred_element_type=_F32)


def _dot_nt(a, b):
    return lax.dot_general(a, b, (((1,), (1,)), ((), ())), preferred_element_type=_F32)


def _resident(block_shape, index_map):
    return pl.BlockSpec(block_shape, index_map, pipeline_mode=pl.Buffered(1))


def _ffn_step(x_ref, gain_ref, load_weights, fgain_ref, out_ref, h_scr, final_norm):
    f = pl.program_id(1)

    @pl.when(f == 0)
    def _():
        x = x_ref[...]
        h_scr[...] = _rms(x, gain_ref[...]).astype(_BF16)
        out_ref[...] = x

    wg, wu, wd = load_weights()
    h = h_scr[...]
    g = _dot(h, wg)
    u = _dot(h, wu)
    a = (g * jax.nn.sigmoid(g) * (0.5 * u)).astype(_BF16)
    out_ref[...] += _dot(a, wd)

    if final_norm:
        @pl.when(f == pl.num_programs(1) - 1)
        def _():
            out_ref[...] = _rms(out_ref[...], fgain_ref[...])


def _ffn_first_tile_kernel(x_ref, gain_ref, wg32_ref, wu32_ref, wd32_ref, fgain_ref,
                           out_ref, wg_ref, wu_ref, wd_ref, h_scr, *, final_norm):
    def load_weights():
        wg = wg32_ref[...].astype(_BF16)
        wu = wu32_ref[...].astype(_BF16)
        wd = wd32_ref[...].astype(_BF16)
        wg_ref[...] = wg
        wu_ref[...] = wu
        wd_ref[...] = wd
        return wg, wu, wd

    _ffn_step(x_ref, gain_ref, load_weights, fgain_ref, out_ref, h_scr, final_norm)


def _ffn_rest_kernel(x_ref, gain_ref, wg_ref, wu_ref, wd_ref, fgain_ref, first_hbm, out_ref, h_scr, copy_sem,
                     *, final_norm):
    i = pl.program_id(0)

    @pl.when((i == 0) & (pl.program_id(1) == 0))
    def _():
        copy = pltpu.make_async_copy(first_hbm, out_ref, copy_sem)
        copy.start()
        copy.wait()

    @pl.when(i > 0)
    def _():
        load_weights = lambda: (wg_ref[...], wu_ref[...], wd_ref[...])
        _ffn_step(x_ref, gain_ref, load_weights, fgain_ref, out_ref, h_scr, final_norm)


def _ffn(x2d, gain, wg32, wu32, wd32, fgain, *, final_norm, name):
    t, d = x2d.shape
    d_ff = wg32.shape[1]
    tm, tf, tf0 = FFN_TOKEN_TILE, FFN_FF_TILE, FFN_FIRST_FF_TILE
    assert t % tm == 0 and d_ff % tf == 0 and d_ff % tf0 == 0
    params = pltpu.CompilerParams(dimension_semantics=("parallel", "arbitrary"),
                                  vmem_limit_bytes=V7X_VMEM_LIMIT_BYTES)
    row = lambda i, f: (0, 0)
    first, wg, wu, wd = pl.pallas_call(
        functools.partial(_ffn_first_tile_kernel, final_norm=final_norm),
        grid=(1, d_ff // tf0),
        in_specs=[
            _resident((tm, d), lambda i, f: (0, 0)),
            pl.BlockSpec((1, d), row),
            pl.BlockSpec((d, tf0), lambda i, f: (0, f)),
            pl.BlockSpec((d, tf0), lambda i, f: (0, f)),
            pl.BlockSpec((tf0, d), lambda i, f: (f, 0)),
            pl.BlockSpec((1, d), row),
        ],
        out_specs=[
            pl.BlockSpec((tm, d), lambda i, f: (0, 0)),
            pl.BlockSpec((d, tf0), lambda i, f: (0, f)),
            pl.BlockSpec((d, tf0), lambda i, f: (0, f)),
            pl.BlockSpec((tf0, d), lambda i, f: (f, 0)),
        ],
        out_shape=[
            jax.ShapeDtypeStruct((tm, d), _F32),
            jax.ShapeDtypeStruct(wg32.shape, _BF16),
            jax.ShapeDtypeStruct(wu32.shape, _BF16),
            jax.ShapeDtypeStruct(wd32.shape, _BF16),
        ],
        scratch_shapes=[pltpu.VMEM((tm, d), _BF16)],
        compiler_params=params,
        name=name + "_first",
    )(x2d, gain, wg32, wu32, wd32, fgain)
    return pl.pallas_call(
        functools.partial(_ffn_rest_kernel, final_norm=final_norm),
        grid=(t // tm, d_ff // tf),
        in_specs=[
            pl.BlockSpec((tm, d), lambda i, f: (jnp.maximum(i, 1), 0)),
            pl.BlockSpec((1, d), row),
            pl.BlockSpec((d, tf), lambda i, f: (0, jnp.where(i == 0, 0, f))),
            pl.BlockSpec((d, tf), lambda i, f: (0, jnp.where(i == 0, 0, f))),
            pl.BlockSpec((tf, d), lambda i, f: (jnp.where(i == 0, 0, f), 0)),
            pl.BlockSpec((1, d), row),
            pl.BlockSpec(memory_space=pl.ANY),
        ],
        out_specs=pl.BlockSpec((tm, d), lambda i, f: (i, 0)),
        out_shape=jax.ShapeDtypeStruct((t, d), _F32),
        scratch_shapes=[pltpu.VMEM((tm, d), _BF16), pltpu.SemaphoreType.DMA(())],
        compiler_params=params,
        name=name + "_rest",
    )(x2d, gain, wg, wu, wd, fgain, first)


def _in_proj_kernel(x_ref, gain_ref, pos_ref, freq_ref, wqvt_ref, wku_ref,
                    qt_ref, k_ref, vt_ref, u_ref, kmean_ref):
    tm = x_ref.shape[1]
    h = _rms(x_ref[0], gain_ref[...]).astype(_BF16)

    ku = _dot(h, wku_ref[...])
    qvt = _dot_nt(wqvt_ref[...], h)

    ang_t = freq_ref[...] * pos_ref[0].astype(_F32)
    cos_t, sin_t = jnp.cos(ang_t), jnp.sin(ang_t)
    pad = HEAD_DIM - ROT_DIM
    cos_n = jnp.transpose(jnp.concatenate([cos_t, cos_t, jnp.ones((pad, tm), _F32)], axis=0))
    sin_n = jnp.transpose(jnp.concatenate([-sin_t, sin_t, jnp.zeros((pad, tm), _F32)], axis=0))
    lane = lax.broadcasted_iota(jnp.int32, (1, HEAD_DIM), 1)

    u_ref[0] = ku[:, ATTN_WIDTH:]
    for hh in range(N_ATTN_HEADS):
        kh = ku[:, hh * HEAD_DIM:(hh + 1) * HEAD_DIM]
        partner = jnp.where(lane < ROT_HALF,
                            pltpu.roll(kh, HEAD_DIM - ROT_HALF, 1),
                            pltpu.roll(kh, ROT_HALF, 1))
        kr = kh * cos_n + partner * sin_n
        k_ref[0, hh] = kr.astype(_BF16)
        for j in range(tm // MOBA_BLOCK):
            kmean_ref[0, j, :, hh * HEAD_DIM:(hh + 1) * HEAD_DIM] = jnp.mean(
                kr[j * MOBA_BLOCK:(j + 1) * MOBA_BLOCK], axis=0, keepdims=True)

    cos_q, sin_q = cos_t * QUERY_SCALE, sin_t * QUERY_SCALE
    for hh in range(N_ATTN_HEADS):
        base = hh * HEAD_DIM
        x1 = qvt[base:base + ROT_HALF]
        x2 = qvt[base + ROT_HALF:base + ROT_DIM]
        qt_ref[0, hh, 0:ROT_HALF, :] = (x1 * cos_q - x2 * sin_q).astype(_BF16)
        qt_ref[0, hh, ROT_HALF:ROT_DIM, :] = (x2 * cos_q + x1 * sin_q).astype(_BF16)
        qt_ref[0, hh, ROT_DIM:, :] = (qvt[base + ROT_DIM:base + HEAD_DIM] * QUERY_SCALE).astype(_BF16)

    ones = jnp.ones((V_ROWS - HEAD_DIM, tm), _BF16)
    for hh in range(N_ATTN_HEADS):
        base = ATTN_WIDTH + hh * HEAD_DIM
        vt_ref[0, hh, 0:HEAD_DIM, :] = qvt[base:base + HEAD_DIM].astype(_BF16)
        vt_ref[0, hh, HEAD_DIM:, :] = ones


def _in_proj(x1, gain, positions, freq, wqvt, wku):
    b, s, d = x1.shape
    tm = PROJ_TOKEN_TILE
    assert s % tm == 0 and tm % MOBA_BLOCK == 0
    nb = s // MOBA_BLOCK
    bpt = tm // MOBA_BLOCK
    pos_row = positions.reshape(b, 1, s)
    const2 = lambda bi, i: (0, 0)
    return pl.pallas_call(
        _in_proj_kernel,
        grid=(b, s // tm),
        in_specs=[
            pl.BlockSpec((1, tm, d), lambda bi, i: (bi, i, 0)),
            pl.BlockSpec((1, d), const2),
            pl.BlockSpec((1, 1, tm), lambda bi, i: (bi, 0, i)),
            pl.BlockSpec((ROT_HALF, 1), const2),
            _resident((2 * ATTN_WIDTH, d), const2),
            _resident((d, ATTN_WIDTH + POOL_WIDTH), const2),
        ],
        out_specs=[
            pl.BlockSpec((1, N_ATTN_HEADS, HEAD_DIM, tm), lambda bi, i: (bi, 0, 0, i)),
            pl.BlockSpec((1, N_ATTN_HEADS, tm, HEAD_DIM), lambda bi, i: (bi, 0, i, 0)),
            pl.BlockSpec((1, N_ATTN_HEADS, V_ROWS, tm), lambda bi, i: (bi, 0, 0, i)),
            pl.BlockSpec((1, tm, POOL_WIDTH), lambda bi, i: (bi, i, 0)),
            pl.BlockSpec((1, bpt, 1, ATTN_WIDTH), lambda bi, i: (bi, i, 0, 0)),
        ],
        out_shape=[
            jax.ShapeDtypeStruct((b, N_ATTN_HEADS, HEAD_DIM, s), _BF16),
            jax.ShapeDtypeStruct((b, N_ATTN_HEADS, s, HEAD_DIM), _BF16),
            jax.ShapeDtypeStruct((b, N_ATTN_HEADS, V_ROWS, s), _BF16),
            jax.ShapeDtypeStruct((b, s, POOL_WIDTH), _F32),
            jax.ShapeDtypeStruct((b, nb, 1, ATTN_WIDTH), _F32),
        ],
        compiler_params=pltpu.CompilerParams(
            dimension_semantics=("parallel", "parallel"),
            vmem_limit_bytes=V7X_VMEM_LIMIT_BYTES),
        name="in_proj",
    )(x1, gain, pos_row, freq, wqvt, wku)


def _moba_kernel(qt_ref, k_ref, vt_ref, kmean_ref, out_ref, bias_scr, acc_scr, m_scr,
                 s_even, s_odd, top_even, top_odd, *, nb, hg):
    i = pl.program_id(2)
    even, odd = (s_even, top_even), (s_odd, top_odd)
    blk = MOBA_BLOCK
    kb_id = lax.broadcasted_iota(jnp.int32, (nb, blk), 0)
    past = kb_id < i
    kb_f = kb_id.astype(_F32)

    def select_blocks(h):
        km = kmean_ref[0, :, h * HEAD_DIM:(h + 1) * HEAD_DIM]
        km_hi = km.astype(_BF16)
        km_lo = (km - km_hi.astype(_F32)).astype(_BF16)
        gate = _dot(km_hi, qt_ref[0, h]) + _dot(km_lo, qt_ref[0, h])
        gate = jnp.where(past, gate, NEG_INF)
        bias = jnp.full((nb, blk), NEG_INF, _F32)
        for _ in range(MOBA_TOPK):
            top = jnp.max(gate, axis=0, keepdims=True)
            first = jnp.min(jnp.where(gate == top, kb_f, float(nb)), axis=0, keepdims=True)
            hit = kb_f == first
            bias = jnp.where(hit, 0.0, bias)
            gate = jnp.where(hit, -jnp.inf, gate)
        bias_scr[h] = jnp.where(past, bias, NEG_INF)

    def scores(h, n):
        kblk = k_ref[0, h, pl.ds(pl.multiple_of(n * blk, blk), blk), :]
        return _dot(kblk, qt_ref[0, h])

    def stage(h, n, staged):
        s_buf, top_buf = staged
        s = scores(h, n)
        s_buf[h] = s
        top_buf[h] = jnp.max(s, axis=0, keepdims=True)

    def weighted_values(h, n, p):
        vblk = vt_ref[0, h, :, pl.ds(pl.multiple_of(n * blk, blk), blk)]
        return _dot(vblk, p)

    key_pos = lax.broadcasted_iota(jnp.int32, (blk, blk), 0)
    qry_pos = lax.broadcasted_iota(jnp.int32, (blk, blk), 1)
    causal = key_pos <= qry_pos
    ps = []
    ss = [jnp.where(causal, scores(h, i), NEG_INF) for h in range(hg)]
    for h in range(hg):
        select_blocks(h)
    for h in range(hg):
        m_own = jnp.max(ss[h], axis=0, keepdims=True)
        m_scr[h] = m_own
        ps.append(jnp.exp2(ss[h] - m_own).astype(_BF16))
    for h in range(hg):
        acc_scr[h] = weighted_values(h, i, ps[h])
        stage(h, 0, even)

    def step(n, cur, nxt, stage_next):
        n_nxt = jnp.minimum(n + 1, nb - 1)
        for h in range(hg):
            if stage_next:
                stage(h, n_nxt, nxt)
            s = cur[0][h]
            b_n = bias_scr[h, pl.ds(n, 1), :]
            m_run = m_scr[h]
            m_new = jnp.maximum(m_run, cur[1][h] + b_n)
            m_scr[h] = m_new
            alpha = jnp.exp2(m_run - m_new)
            p = jnp.exp2(s - (m_new - b_n)).astype(_BF16)
            acc_scr[h] = alpha * acc_scr[h] + weighted_values(h, n, p)

    def run(n0, count, stage_last=True):
        for c in range(count):
            bufs = (even, odd) if c % 2 == 0 else (odd, even)
            step(n0 + c, *bufs, stage_next=stage_last or c + 1 < count)

    def body(j, carry):
        run(4 * j, 4)
        return carry

    lax.fori_loop(0, i // 4, body, 0)

    @pl.when(i % 4 >= 2)
    def _():
        run((i // 4) * 4, 2)

    @pl.when(i % 2 == 1)
    def _():
        run(i - 1, 1, stage_last=False)

    for h in range(hg):
        acc = acc_scr[h]
        out_ref[0, :, h * HEAD_DIM:(h + 1) * HEAD_DIM] = jnp.transpose(
            acc[0:HEAD_DIM] / acc[HEAD_DIM:HEAD_DIM + 1]).astype(out_ref.dtype)


def _moba(qt, k, vt, kmean):
    b, nh, hd, s = qt.shape
    nb = s // MOBA_BLOCK
    hg = MOBA_HEADS_PER_STEP
    assert nh % hg == 0
    return pl.pallas_call(
        functools.partial(_moba_kernel, nb=nb, hg=hg),
        grid=(b, nh // hg, nb),
        in_specs=[
            pl.BlockSpec((1, hg, hd, MOBA_BLOCK), lambda bi, g, i: (bi, g, 0, i)),
            pl.BlockSpec((1, hg, s, hd), lambda bi, g, i: (bi, g, 0, 0)),
            pl.BlockSpec((1, hg, V_ROWS, s), lambda bi, g, i: (bi, g, 0, 0)),
            pl.BlockSpec((1, nb, hg * hd), lambda bi, g, i: (bi, 0, g)),
        ],
        out_specs=pl.BlockSpec((1, MOBA_BLOCK, hg * hd), lambda bi, g, i: (bi, i, g)),
        out_shape=jax.ShapeDtypeStruct((b, s, nh * hd), _BF16),
        scratch_shapes=[pltpu.VMEM((hg, nb, MOBA_BLOCK), _F32),
                        pltpu.VMEM((hg, V_ROWS, MOBA_BLOCK), _F32),
                        pltpu.VMEM((hg, 1, MOBA_BLOCK), _F32),
                        pltpu.VMEM((hg, MOBA_BLOCK, MOBA_BLOCK), _F32),
                        pltpu.VMEM((hg, MOBA_BLOCK, MOBA_BLOCK), _F32),
                        pltpu.VMEM((hg, 1, MOBA_BLOCK), _F32),
                        pltpu.VMEM((hg, 1, MOBA_BLOCK), _F32)],
        compiler_params=pltpu.CompilerParams(
            dimension_semantics=("parallel", "parallel", "arbitrary"),
            vmem_limit_bytes=V7X_VMEM_LIMIT_BYTES),
        name="moba",
    )(qt, k, vt, kmean)


def _mix_out_kernel(x_ref, attn_ref, u_ref, halo_ref, pw_ref, ps_ref, wo_ref, out_ref, ubuf, ybuf):
    tm = x_ref.shape[1]
    si = pl.program_id(1)
    ubuf[0:POOL_HALO, :] = jnp.where(si > 0, halo_ref[0], 0.0)
    ubuf[POOL_HALO:, :] = u_ref[0]

    t = si * tm + lax.broadcasted_iota(jnp.int32, (tm, 1), 0)
    mix = _dot(attn_ref[0], wo_ref[0:ATTN_WIDTH, :])
    for g, w in enumerate(POOL_WINDOWS):
        cols = slice(g * POOL_GROUP_DIM, (g + 1) * POOL_GROUP_DIM)
        ext = ubuf[:, cols]
        wsum = ext
        span = 1
        while span < w:
            wsum = wsum + pltpu.roll(wsum, span, 0)
            span *= 2
        own = ext[POOL_HALO:]
        count = jnp.minimum(t + 1, w).astype(_F32)
        pooled = wsum[POOL_HALO:] / count - own
        y = _dot(pooled.astype(_BF16), pw_ref[g]) * ps_ref[:, cols]
        ybuf[:, cols] = y.astype(_BF16)
    mix = mix + _dot(ybuf[...], wo_ref[ATTN_WIDTH:, :])
    out_ref[0] = x_ref[0] + mix


def _mix_out(x1, attn, u, pool_w, pool_scale, w_out):
    b, s, d = x1.shape
    tm = MIX_TOKEN_TILE
    assert s % tm == 0 and tm % POOL_HALO == 0
    assert all(w <= POOL_HALO and w & (w - 1) == 0 for w in POOL_WINDOWS)
    halo_per_tile = tm // POOL_HALO
    const2 = lambda bi, i: (0, 0)
    return pl.pallas_call(
        _mix_out_kernel,
        grid=(b, s // tm),
        in_specs=[
            pl.BlockSpec((1, tm, d), lambda bi, i: (bi, i, 0)),
            pl.BlockSpec((1, tm, ATTN_WIDTH), lambda bi, i: (bi, i, 0)),
            pl.BlockSpec((1, tm, POOL_WIDTH), lambda bi, i: (bi, i, 0)),
            pl.BlockSpec((1, POOL_HALO, POOL_WIDTH),
                         lambda bi, i: (bi, jnp.maximum(i * halo_per_tile - 1, 0), 0)),
            _resident(pool_w.shape, lambda bi, i: (0, 0, 0)),
            pl.BlockSpec((1, POOL_WIDTH), const2),
            _resident(w_out.shape, const2),
        ],
        out_specs=pl.BlockSpec((1, tm, d), lambda bi, i: (bi, i, 0)),
        out_shape=jax.ShapeDtypeStruct((b, s, d), _F32),
        scratch_shapes=[pltpu.VMEM((POOL_HALO + tm, POOL_WIDTH), _F32), pltpu.VMEM((tm, POOL_WIDTH), _BF16)],
        compiler_params=pltpu.CompilerParams(
            dimension_semantics=("parallel", "parallel"),
            vmem_limit_bytes=V7X_VMEM_LIMIT_BYTES),
        name="mix_out",
    )(x1, attn, u, u, pool_w, pool_scale, w_out)


def kernel(x, positions, norm_ffn1, w1_gate, w1_up, w1_down, norm_mix, w_in, pool_w, pool_scale,
           w_out, norm_ffn2, w2_gate, w2_up, w2_down, norm_final):
    b, s, d = x.shape
    depth = w_in.shape[0]
    inv_freq = ROPE_THETA ** (-jnp.arange(0, ROT_DIM, 2, dtype=_F32) / ROT_DIM)
    freq = inv_freq[:, None]
    fgain = norm_final[None, :]
    bf = lambda w: w.astype(_BF16)

    for l in range(depth):
        x1 = _ffn(x.reshape(b * s, d), norm_ffn1[l][None, :], w1_gate[l], w1_up[l], w1_down[l],
                  fgain, final_norm=False, name="ffn_pre").reshape(b, s, d)
        wq, wk, wv, wu = jnp.split(bf(w_in[l]), [ATTN_WIDTH, 2 * ATTN_WIDTH, 3 * ATTN_WIDTH], axis=-1)
        wqvt = jnp.concatenate([wq, wv], axis=1).T
        wku = jnp.concatenate([wk, wu], axis=1)
        qt, k, vt, u, kmean = _in_proj(x1, norm_mix[l][None, :], positions, freq, wqvt, wku)
        attn = _moba(qt, k, vt, kmean.reshape(b, s // MOBA_BLOCK, ATTN_WIDTH))
        x2 = _mix_out(x1, attn, u, bf(pool_w[l]), pool_scale[l][None, :], bf(w_out[l]))
        last = l == depth - 1
        x = _ffn(x2.reshape(b * s, d), norm_ffn2[l][None, :], w2_gate[l], w2_up[l], w2_down[l],
                 fgain, final_norm=last, name="ffn_post").reshape(b, s, d)
    return x
```

```python
import functools

import jax
import jax.numpy as jnp
from jax import lax
from jax.experimental import pallas as pl
from jax.experimental.pallas import tpu as pltpu

N_ATTN_HEADS = 8
HEAD_DIM = 128
ATTN_WIDTH = N_ATTN_HEADS * HEAD_DIM
POOL_WINDOWS = (2, 4, 8, 16)
POOL_GROUP_DIM = 256
POOL_WIDTH = POOL_GROUP_DIM * len(POOL_WINDOWS)
MOBA_BLOCK = 256
MOBA_TOPK = 3
ROT_DIM = HEAD_DIM // 4
ROT_HALF = ROT_DIM // 2
ROPE_THETA = 500000.0
EPS = 1e-6
NEG_INF = -1e30
LOG2_E = 1.4426950408889634
QUERY_SCALE = (HEAD_DIM ** -0.5) * LOG2_E
V7X_BF16_ROWS_PER_VREG = 16
V_ROWS = HEAD_DIM + V7X_BF16_ROWS_PER_VREG
POOL_HALO = 16

V7X_VMEM_LIMIT_BYTES = 58 * 1024 * 1024

FFN_TOKEN_TILE = 1024
FFN_FF_TILE = 512
FFN_FIRST_FF_TILE = 256
PROJ_TOKEN_TILE = 512
MIX_TOKEN_TILE = 512
MOBA_HEADS_PER_STEP = 8

_BF16 = jnp.bfloat16
_F32 = jnp.float32


def _rms(x, gain):
    inv = lax.rsqrt(jnp.mean(x * x, axis=-1, keepdims=True) + EPS)
    return x * inv * gain


def _dot(a, b):
    return jnp.dot(a, b, preferred_element_type=_F32)


def _dot_nt(a, b):
    return lax.dot_general(a, b, (((1,), (1,)), ((), ())), preferred_element_type=_F32)


def _resident(block_shape, index_map):
    return pl.BlockSpec(block_shape, index_map, pipeline_mode=pl.Buffered(1))


def _ffn_step(x_ref, gain_ref, load_weights, fgain_ref, out_ref, h_scr, final_norm):
    f = pl.program_id(1)

    @pl.when(f == 0)
    def _():
        x = x_ref[...]
        h_scr[...] = _rms(x, gain_ref[...]).astype(_BF16)
        out_ref[...] = x

    wg, wu, wd = load_weights()
    h = h_scr[...]
    g = _dot(h, wg)
    u = _dot(h, wu)
    a = (g * jax.nn.sigmoid(g) * (0.5 * u)).astype(_BF16)
    out_ref[...] += _dot(a, wd)

    if final_norm:
        @pl.when(f == pl.num_programs(1) - 1)
        def _():
            out_ref[...] = _rms(out_ref[...], fgain_ref[...])


def _ffn_first_tile_kernel(x_ref, gain_ref, wg32_ref, wu32_ref, wd32_ref, fgain_ref,
                           out_ref, wg_ref, wu_ref, wd_ref, h_scr, *, final_norm):
    def load_weights():
        wg = wg32_ref[...].astype(_BF16)
        wu = wu32_ref[...].astype(_BF16)
        wd = wd32_ref[...].astype(_BF16)
        wg_ref[...] = wg
        wu_ref[...] = wu
        wd_ref[...] = wd
        return wg, wu, wd

    _ffn_step(x_ref, gain_ref, load_weights, fgain_ref, out_ref, h_scr, final_norm)


def _ffn_rest_kernel(x_ref, gain_ref, wg_ref, wu_ref, wd_ref, fgain_ref, first_hbm, out_ref, h_scr, copy_sem,
                     *, final_norm):
    i = pl.program_id(0)

    @pl.when((i == 0) & (pl.program_id(1) == 0))
    def _():
        copy = pltpu.make_async_copy(first_hbm, out_ref, copy_sem)
        copy.start()
        copy.wait()

    @pl.when(i > 0)
    def _():
        load_weights = lambda: (wg_ref[...], wu_ref[...], wd_ref[...])
        _ffn_step(x_ref, gain_ref, load_weights, fgain_ref, out_ref, h_scr, final_norm)


def _ffn(x2d, gain, wg32, wu32, wd32, fgain, *, final_norm, name):
    t, d = x2d.shape
    d_ff = wg32.shape[1]
    tm, tf, tf0 = FFN_TOKEN_TILE, FFN_FF_TILE, FFN_FIRST_FF_TILE
    assert t % tm == 0 and d_ff % tf == 0 and d_ff % tf0 == 0
    params = pltpu.CompilerParams(dimension_semantics=("parallel", "arbitrary"),
                                  vmem_limit_bytes=V7X_VMEM_LIMIT_BYTES)
    row = lambda i, f: (0, 0)
    first, wg, wu, wd = pl.pallas_call(
        functools.partial(_ffn_first_tile_kernel, final_norm=final_norm),
        grid=(1, d_ff // tf0),
        in_specs=[
            _resident((tm, d), lambda i, f: (0, 0)),
            pl.BlockSpec((1, d), row),
            pl.BlockSpec((d, tf0), lambda i, f: (0, f)),
            pl.BlockSpec((d, tf0), lambda i, f: (0, f)),
            pl.BlockSpec((tf0, d), lambda i, f: (f, 0)),
            pl.BlockSpec((1, d), row),
        ],
        out_specs=[
            pl.BlockSpec((tm, d), lambda i, f: (0, 0)),
            pl.BlockSpec((d, tf0), lambda i, f: (0, f)),
            pl.BlockSpec((d, tf0), lambda i, f: (0, f)),
            pl.BlockSpec((tf0, d), lambda i, f: (f, 0)),
        ],
        out_shape=[
            jax.ShapeDtypeStruct((tm, d), _F32),
            jax.ShapeDtypeStruct(wg32.shape, _BF16),
            jax.ShapeDtypeStruct(wu32.shape, _BF16),
            jax.ShapeDtypeStruct(wd32.shape, _BF16),
        ],
        scratch_shapes=[pltpu.VMEM((tm, d), _BF16)],
        compiler_params=params,
        name=name + "_first",
    )(x2d, gain, wg32, wu32, wd32, fgain)
    return pl.pallas_call(
        functools.partial(_ffn_rest_kernel, final_norm=final_norm),
        grid=(t // tm, d_ff // tf),
        in_specs=[
            pl.BlockSpec((tm, d), lambda i, f: (jnp.maximum(i, 1), 0)),
            pl.BlockSpec((1, d), row),
            pl.BlockSpec((d, tf), lambda i, f: (0, jnp.where(i == 0, 0, f))),
            pl.BlockSpec((d, tf), lambda i, f: (0, jnp.where(i == 0, 0, f))),
            pl.BlockSpec((tf, d), lambda i, f: (jnp.where(i == 0, 0, f), 0)),
            pl.BlockSpec((1, d), row),
            pl.BlockSpec(memory_space=pl.ANY),
        ],
        out_specs=pl.BlockSpec((tm, d), lambda i, f: (i, 0)),
        out_shape=jax.ShapeDtypeStruct((t, d), _F32),
        scratch_shapes=[pltpu.VMEM((tm, d), _BF16), pltpu.SemaphoreType.DMA(())],
        compiler_params=params,
        name=name + "_rest",
    )(x2d, gain, wg, wu, wd, fgain, first)


def _ffn_bf16_kernel(x_ref, gain_ref, wg_ref, wu_ref, wd_ref, fgain_ref, out_ref, h_scr, *, final_norm):
    load_weights = lambda: (wg_ref[...], wu_ref[...], wd_ref[...])
    _ffn_step(x_ref, gain_ref, load_weights, fgain_ref, out_ref, h_scr, final_norm)


def _ffn_bf16(x2d, gain, wg, wu, wd, fgain, *, final_norm, name):
    t, d = x2d.shape
    d_ff = wg.shape[1]
    tm, tf = FFN_TOKEN_TILE, FFN_FF_TILE
    assert t % tm == 0 and d_ff % tf == 0
    row = lambda i, f: (0, 0)
    return pl.pallas_call(
        functools.partial(_ffn_bf16_kernel, final_norm=final_norm),
        grid=(t // tm, d_ff // tf),
        in_specs=[
            pl.BlockSpec((tm, d), lambda i, f: (i, 0)),
            pl.BlockSpec((1, d), row),
            pl.BlockSpec((d, tf), lambda i, f: (0, f)),
            pl.BlockSpec((d, tf), lambda i, f: (0, f)),
            pl.BlockSpec((tf, d), lambda i, f: (f, 0)),
            pl.BlockSpec((1, d), row),
        ],
        out_specs=pl.BlockSpec((tm, d), lambda i, f: (i, 0)),
        out_shape=jax.ShapeDtypeStruct((t, d), _F32),
        scratch_shapes=[pltpu.VMEM((tm, d), _BF16)],
        compiler_params=pltpu.CompilerParams(dimension_semantics=("parallel", "arbitrary"),
                                             vmem_limit_bytes=V7X_VMEM_LIMIT_BYTES),
        name=name,
    )(x2d, gain, wg, wu, wd, fgain)


def _in_proj_kernel(x_ref, gain_ref, pos_ref, freq_ref, wqvt_ref, wku_ref,
                    qt_ref, k_ref, vt_ref, u_ref, kmean_ref):
    tm = x_ref.shape[1]
    h = _rms(x_ref[0], gain_ref[...]).astype(_BF16)

    ku = _dot(h, wku_ref[...])
    qvt = _dot_nt(wqvt_ref[...], h)

    ang_t = freq_ref[...] * pos_ref[0].astype(_F32)
    cos_t, sin_t = jnp.cos(ang_t), jnp.sin(ang_t)
    pad = HEAD_DIM - ROT_DIM
    cos_n = jnp.transpose(jnp.concatenate([cos_t, cos_t, jnp.ones((pad, tm), _F32)], axis=0))
    sin_n = jnp.transpose(jnp.concatenate([-sin_t, sin_t, jnp.zeros((pad, tm), _F32)], axis=0))
    lane = lax.broadcasted_iota(jnp.int32, (1, HEAD_DIM), 1)

    u_ref[0] = ku[:, ATTN_WIDTH:]
    for hh in range(N_ATTN_HEADS):
        kh = ku[:, hh * HEAD_DIM:(hh + 1) * HEAD_DIM]
        partner = jnp.where(lane < ROT_HALF,
                            pltpu.roll(kh, HEAD_DIM - ROT_HALF, 1),
                            pltpu.roll(kh, ROT_HALF, 1))
        kr = kh * cos_n + partner * sin_n
        k_ref[0, hh] = kr.astype(_BF16)
        for j in range(tm // MOBA_BLOCK):
            kmean_ref[0, j, :, hh * HEAD_DIM:(hh + 1) * HEAD_DIM] = jnp.mean(
                kr[j * MOBA_BLOCK:(j + 1) * MOBA_BLOCK], axis=0, keepdims=True)

    cos_q, sin_q = cos_t * QUERY_SCALE, sin_t * QUERY_SCALE
    for hh in range(N_ATTN_HEADS):
        base = hh * HEAD_DIM
        x1 = qvt[base:base + ROT_HALF]
        x2 = qvt[base + ROT_HALF:base + ROT_DIM]
        qt_ref[0, hh, 0:ROT_HALF, :] = (x1 * cos_q - x2 * sin_q).astype(_BF16)
        qt_ref[0, hh, ROT_HALF:ROT_DIM, :] = (x2 * cos_q + x1 * sin_q).astype(_BF16)
        qt_ref[0, hh, ROT_DIM:, :] = (qvt[base + ROT_DIM:base + HEAD_DIM] * QUERY_SCALE).astype(_BF16)

    ones = jnp.ones((V_ROWS - HEAD_DIM, tm), _BF16)
    for hh in range(N_ATTN_HEADS):
        base = ATTN_WIDTH + hh * HEAD_DIM
        vt_ref[0, hh, 0:HEAD_DIM, :] = qvt[base:base + HEAD_DIM].astype(_BF16)
        vt_ref[0, hh, HEAD_DIM:, :] = ones


def _in_proj(x1, gain, positions, freq, wqvt, wku):
    b, s, d = x1.shape
    tm = PROJ_TOKEN_TILE
    assert s % tm == 0 and tm % MOBA_BLOCK == 0
    nb = s // MOBA_BLOCK
    bpt = tm // MOBA_BLOCK
    pos_row = positions.reshape(b, 1, s)
    const2 = lambda bi, i: (0, 0)
    return pl.pallas_call(
        _in_proj_kernel,
        grid=(b, s // tm),
        in_specs=[
            pl.BlockSpec((1, tm, d), lambda bi, i: (bi, i, 0)),
            pl.BlockSpec((1, d), const2),
            pl.BlockSpec((1, 1, tm), lambda bi, i: (bi, 0, i)),
            pl.BlockSpec((ROT_HALF, 1), const2),
            _resident((2 * ATTN_WIDTH, d), const2),
            _resident((d, ATTN_WIDTH + POOL_WIDTH), const2),
        ],
        out_specs=[
            pl.BlockSpec((1, N_ATTN_HEADS, HEAD_DIM, tm), lambda bi, i: (bi, 0, 0, i)),
            pl.BlockSpec((1, N_ATTN_HEADS, tm, HEAD_DIM), lambda bi, i: (bi, 0, i, 0)),
            pl.BlockSpec((1, N_ATTN_HEADS, V_ROWS, tm), lambda bi, i: (bi, 0, 0, i)),
            pl.BlockSpec((1, tm, POOL_WIDTH), lambda bi, i: (bi, i, 0)),
            pl.BlockSpec((1, bpt, 1, ATTN_WIDTH), lambda bi, i: (bi, i, 0, 0)),
        ],
        out_shape=[
            jax.ShapeDtypeStruct((b, N_ATTN_HEADS, HEAD_DIM, s), _BF16),
            jax.ShapeDtypeStruct((b, N_ATTN_HEADS, s, HEAD_DIM), _BF16),
            jax.ShapeDtypeStruct((b, N_ATTN_HEADS, V_ROWS, s), _BF16),
            jax.ShapeDtypeStruct((b, s, POOL_WIDTH), _F32),
            jax.ShapeDtypeStruct((b, nb, 1, ATTN_WIDTH), _F32),
        ],
        compiler_params=pltpu.CompilerParams(
            dimension_semantics=("parallel", "parallel"),
            vmem_limit_bytes=V7X_VMEM_LIMIT_BYTES),
        name="in_proj",
    )(x1, gain, pos_row, freq, wqvt, wku)


def _moba_kernel(qt_ref, k_ref, vt_ref, kmean_ref, *rest, nb, hg, n_riders):
    rider_in, rest = rest[:n_riders], rest[n_riders:]
    out_ref, rider_out, rest = rest[0], rest[1:1 + n_riders], rest[1 + n_riders:]
    bias_scr, acc_scr, m_scr, s_even, s_odd, top_even, top_odd = rest
    for src_ref, dst_ref in zip(rider_in, rider_out):
        dst_ref[...] = src_ref[...].astype(_BF16)

    i = pl.program_id(2)
    even, odd = (s_even, top_even), (s_odd, top_odd)
    blk = MOBA_BLOCK
    kb_id = lax.broadcasted_iota(jnp.int32, (nb, blk), 0)
    past = kb_id < i
    kb_f = kb_id.astype(_F32)

    def select_blocks(h):
        km = kmean_ref[0, :, h * HEAD_DIM:(h + 1) * HEAD_DIM]
        km_hi = km.astype(_BF16)
        km_lo = (km - km_hi.astype(_F32)).astype(_BF16)
        gate = _dot(km_hi, qt_ref[0, h]) + _dot(km_lo, qt_ref[0, h])
        gate = jnp.where(past, gate, NEG_INF)
        bias = jnp.full((nb, blk), NEG_INF, _F32)
        for _ in range(MOBA_TOPK):
            top = jnp.max(gate, axis=0, keepdims=True)
            first = jnp.min(jnp.where(gate == top, kb_f, float(nb)), axis=0, keepdims=True)
            hit = kb_f == first
            bias = jnp.where(hit, 0.0, bias)
            gate = jnp.where(hit, -jnp.inf, gate)
        bias_scr[h] = jnp.where(past, bias, NEG_INF)

    def scores(h, n):
        kblk = k_ref[0, h, pl.ds(pl.multiple_of(n * blk, blk), blk), :]
        return _dot(kblk, qt_ref[0, h])

    def stage(h, n, staged):
        s_buf, top_buf = staged
        s = scores(h, n)
        s_buf[h] = s
        top_buf[h] = jnp.max(s, axis=0, keepdims=True)

    def weighted_values(h, n, p):
        vblk = vt_ref[0, h, :, pl.ds(pl.multiple_of(n * blk, blk), blk)]
        return _dot(vblk, p)

    key_pos = lax.broadcasted_iota(jnp.int32, (blk, blk), 0)
    qry_pos = lax.broadcasted_iota(jnp.int32, (blk, blk), 1)
    causal = key_pos <= qry_pos
    ps = []
    ss = [jnp.where(causal, scores(h, i), NEG_INF) for h in range(hg)]
    for h in range(hg):
        select_blocks(h)
    for h in range(hg):
        m_own = jnp.max(ss[h], axis=0, keepdims=True)
        m_scr[h] = m_own
        ps.append(jnp.exp2(ss[h] - m_own).astype(_BF16))
    for h in range(hg):
        acc_scr[h] = weighted_values(h, i, ps[h])
        stage(h, 0, even)

    def step(n, cur, nxt, stage_next):
        n_nxt = jnp.minimum(n + 1, nb - 1)
        for h in range(hg):
            if stage_next:
                stage(h, n_nxt, nxt)
            s = cur[0][h]
            b_n = bias_scr[h, pl.ds(n, 1), :]
            m_run = m_scr[h]
            m_new = jnp.maximum(m_run, cur[1][h] + b_n)
            m_scr[h] = m_new
            alpha = jnp.exp2(m_run - m_new)
            p = jnp.exp2(s - (m_new - b_n)).astype(_BF16)
            acc_scr[h] = alpha * acc_scr[h] + weighted_values(h, n, p)

    def run(n0, count, stage_last=True):
        for c in range(count):
            bufs = (even, odd) if c % 2 == 0 else (odd, even)
            step(n0 + c, *bufs, stage_next=stage_last or c + 1 < count)

    def body(j, carry):
        run(4 * j, 4)
        return carry

    lax.fori_loop(0, i // 4, body, 0)

    @pl.when(i % 4 >= 2)
    def _():
        run((i // 4) * 4, 2)

    @pl.when(i % 2 == 1)
    def _():
        run(i - 1, 1, stage_last=False)

    for h in range(hg):
        acc = acc_scr[h]
        out_ref[0, :, h * HEAD_DIM:(h + 1) * HEAD_DIM] = jnp.transpose(
            acc[0:HEAD_DIM] / acc[HEAD_DIM:HEAD_DIM + 1]).astype(out_ref.dtype)


def _moba(qt, k, vt, kmean, riders):
    b, nh, hd, s = qt.shape
    nb = s // MOBA_BLOCK
    hg = MOBA_HEADS_PER_STEP
    assert nh % hg == 0
    groups = nh // hg
    n_steps = b * groups * nb
    step = lambda bi, g, i: ((bi * groups + g) * nb + i, 0)
    rider_specs = []
    for w in riders:
        rows = w.shape[0] // n_steps
        assert w.shape[0] % n_steps == 0 and rows % V7X_BF16_ROWS_PER_VREG == 0
        rider_specs.append(pl.BlockSpec((rows, w.shape[1]), step))
    outs = pl.pallas_call(
        functools.partial(_moba_kernel, nb=nb, hg=hg, n_riders=len(riders)),
        grid=(b, groups, nb),
        in_specs=[
            pl.BlockSpec((1, hg, hd, MOBA_BLOCK), lambda bi, g, i: (bi, g, 0, i)),
            pl.BlockSpec((1, hg, s, hd), lambda bi, g, i: (bi, g, 0, 0)),
            pl.BlockSpec((1, hg, V_ROWS, s), lambda bi, g, i: (bi, g, 0, 0)),
            pl.BlockSpec((1, nb, hg * hd), lambda bi, g, i: (bi, 0, g)),
        ] + rider_specs,
        out_specs=[pl.BlockSpec((1, MOBA_BLOCK, hg * hd), lambda bi, g, i: (bi, i, g))] + rider_specs,
        out_shape=[jax.ShapeDtypeStruct((b, s, nh * hd), _BF16)]
                  + [jax.ShapeDtypeStruct(w.shape, _BF16) for w in riders],
        scratch_shapes=[pltpu.VMEM((hg, nb, MOBA_BLOCK), _F32),
                        pltpu.VMEM((hg, V_ROWS, MOBA_BLOCK), _F32),
                        pltpu.VMEM((hg, 1, MOBA_BLOCK), _F32),
                        pltpu.VMEM((hg, MOBA_BLOCK, MOBA_BLOCK), _F32),
                        pltpu.VMEM((hg, MOBA_BLOCK, MOBA_BLOCK), _F32),
                        pltpu.VMEM((hg, 1, MOBA_BLOCK), _F32),
                        pltpu.VMEM((hg, 1, MOBA_BLOCK), _F32)],
        compiler_params=pltpu.CompilerParams(
            dimension_semantics=("parallel", "parallel", "arbitrary"),
            vmem_limit_bytes=V7X_VMEM_LIMIT_BYTES),
        name="moba",
    )(qt, k, vt, kmean, *riders)
    return outs[0], outs[1:]


def _mix_out_kernel(x_ref, attn_ref, u_ref, halo_ref, pw_ref, ps_ref, wo_ref, out_ref, ubuf, ybuf):
    tm = x_ref.shape[1]
    si = pl.program_id(1)
    ubuf[0:POOL_HALO, :] = jnp.where(si > 0, halo_ref[0], 0.0)
    ubuf[POOL_HALO:, :] = u_ref[0]

    t = si * tm + lax.broadcasted_iota(jnp.int32, (tm, 1), 0)
    mix = _dot(attn_ref[0], wo_ref[0:ATTN_WIDTH, :])
    for g, w in enumerate(POOL_WINDOWS):
        cols = slice(g * POOL_GROUP_DIM, (g + 1) * POOL_GROUP_DIM)
        ext = ubuf[:, cols]
        wsum = ext
        span = 1
        while span < w:
            wsum = wsum + pltpu.roll(wsum, span, 0)
            span *= 2
        own = ext[POOL_HALO:]
        count = jnp.minimum(t + 1, w).astype(_F32)
        pooled = wsum[POOL_HALO:] / count - own
        y = _dot(pooled.astype(_BF16), pw_ref[g]) * ps_ref[:, cols]
        ybuf[:, cols] = y.astype(_BF16)
    mix = mix + _dot(ybuf[...], wo_ref[ATTN_WIDTH:, :])
    out_ref[0] = x_ref[0] + mix


def _mix_out(x1, attn, u, pool_w, pool_scale, w_out):
    b, s, d = x1.shape
    tm = MIX_TOKEN_TILE
    assert s % tm == 0 and tm % POOL_HALO == 0
    assert all(w <= POOL_HALO and w & (w - 1) == 0 for w in POOL_WINDOWS)
    halo_per_tile = tm // POOL_HALO
    const2 = lambda bi, i: (0, 0)
    return pl.pallas_call(
        _mix_out_kernel,
        grid=(b, s // tm),
        in_specs=[
            pl.BlockSpec((1, tm, d), lambda bi, i: (bi, i, 0)),
            pl.BlockSpec((1, tm, ATTN_WIDTH), lambda bi, i: (bi, i, 0)),
            pl.BlockSpec((1, tm, POOL_WIDTH), lambda bi, i: (bi, i, 0)),
            pl.BlockSpec((1, POOL_HALO, POOL_WIDTH),
                         lambda bi, i: (bi, jnp.maximum(i * halo_per_tile - 1, 0), 0)),
            _resident(pool_w.shape, lambda bi, i: (0, 0, 0)),
            pl.BlockSpec((1, POOL_WIDTH), const2),
            _resident(w_out.shape, const2),
        ],
        out_specs=pl.BlockSpec((1, tm, d), lambda bi, i: (bi, i, 0)),
        out_shape=jax.ShapeDtypeStruct((b, s, d), _F32),
        scratch_shapes=[pltpu.VMEM((POOL_HALO + tm, POOL_WIDTH), _F32), pltpu.VMEM((tm, POOL_WIDTH), _BF16)],
        compiler_params=pltpu.CompilerParams(
            dimension_semantics=("parallel", "parallel"),
            vmem_limit_bytes=V7X_VMEM_LIMIT_BYTES),
        name="mix_out",
    )(x1, attn, u, u, pool_w, pool_scale, w_out)


def kernel(x, positions, norm_ffn1, w1_gate, w1_up, w1_down, norm_mix, w_in, pool_w, pool_scale,
           w_out, norm_ffn2, w2_gate, w2_up, w2_down, norm_final):
    b, s, d = x.shape
    depth = w_in.shape[0]
    inv_freq = ROPE_THETA ** (-jnp.arange(0, ROT_DIM, 2, dtype=_F32) / ROT_DIM)
    freq = inv_freq[:, None]
    fgain = norm_final[None, :]
    bf = lambda w: w.astype(_BF16)

    for l in range(depth):
        x1 = _ffn(x.reshape(b * s, d), norm_ffn1[l][None, :], w1_gate[l], w1_up[l], w1_down[l],
                  fgain, final_norm=False, name="ffn_pre").reshape(b, s, d)
        wq, wk, wv, wu = jnp.split(bf(w_in[l]), [ATTN_WIDTH, 2 * ATTN_WIDTH, 3 * ATTN_WIDTH], axis=-1)
        wqvt = jnp.concatenate([wq, wv], axis=1).T
        wku = jnp.concatenate([wk, wu], axis=1)
        qt, k, vt, u, kmean = _in_proj(x1, norm_mix[l][None, :], positions, freq, wqvt, wku)
        d_ff = w2_gate.shape[2]
        riders = [w2_gate[l], w2_up[l], w2_down[l].reshape(d, d_ff), w_out[l],
                  pool_w[l].reshape(-1, POOL_GROUP_DIM)]
        attn, (wg2, wu2, wd2, wo, pw) = _moba(qt, k, vt, kmean.reshape(b, s // MOBA_BLOCK, ATTN_WIDTH), riders)
        x2 = _mix_out(x1, attn, u, pw.reshape(pool_w.shape[1:]), pool_scale[l][None, :], wo)
        last = l == depth - 1
        x = _ffn_bf16(x2.reshape(b * s, d), norm_ffn2[l][None, :], wg2, wu2, wd2.reshape(d_ff, d),
                      fgain, final_norm=last, name="ffn_post").reshape(b, s, d)
    return x
```

```python
import functools

import jax
import jax.numpy as jnp
from jax import lax
from jax.experimental import pallas as pl
from jax.experimental.pallas import tpu as pltpu

N_ATTN_HEADS = 8
HEAD_DIM = 128
ATTN_WIDTH = N_ATTN_HEADS * HEAD_DIM
POOL_WINDOWS = (2, 4, 8, 16)
POOL_GROUP_DIM = 256
POOL_WIDTH = POOL_GROUP_DIM * len(POOL_WINDOWS)
MOBA_BLOCK = 256
MOBA_TOPK = 3
ROT_DIM = HEAD_DIM // 4
ROT_HALF = ROT_DIM // 2
ROPE_THETA = 500000.0
EPS = 1e-6
NEG_INF = -1e30
LOG2_E = 1.4426950408889634
QUERY_SCALE = (HEAD_DIM ** -0.5) * LOG2_E
V7X_BF16_ROWS_PER_VREG = 16
V_ROWS = HEAD_DIM + V7X_BF16_ROWS_PER_VREG
POOL_HALO = 16

V7X_VMEM_LIMIT_BYTES = 58 * 1024 * 1024

FFN_TOKEN_TILE = 1024
FFN_FF_TILE = 512
FFN_FIRST_FF_TILE = 256
PROJ_TOKEN_TILE = 512
MIX_TOKEN_TILE = 512
MOBA_HEADS_PER_STEP = 8

_BF16 = jnp.bfloat16
_F32 = jnp.float32


def _rms(x, gain):
    inv = lax.rsqrt(jnp.mean(x * x, axis=-1, keepdims=True) + EPS)
    return x * inv * gain


def _dot(a, b):
    return jnp.dot(a, b, preferred_element_type=_F32)


def _dot_nt(a, b):
    return lax.dot_general(a, b, (((1,), (1,)), ((), ())), preferred_element_type=_F32)


def _resident(block_shape, index_map):
    return pl.BlockSpec(block_shape, index_map, pipeline_mode=pl.Buffered(1))


def _ffn_step(x_ref, gain_ref, load_weights, fgain_ref, out_ref, h_scr, final_norm):
    f = pl.program_id(1)

    @pl.when(f == 0)
    def _():
        x = x_ref[...]
        h_scr[...] = _rms(x, gain_ref[...]).astype(_BF16)
        out_ref[...] = x

    wg, wu, wd = load_weights()
    h = h_scr[...]
    g = _dot(h, wg)
    u = _dot(h, wu)
    a = (g * jax.nn.sigmoid(g) * (0.5 * u)).astype(_BF16)
    out_ref[...] += _dot(a, wd)

    if final_norm:
        @pl.when(f == pl.num_programs(1) - 1)
        def _():
            out_ref[...] = _rms(out_ref[...], fgain_ref[...])


def _ffn_first_tile_kernel(x_ref, gain_ref, wg32_ref, wu32_ref, wd32_ref, fgain_ref,
                           out_ref, wg_ref, wu_ref, wd_ref, h_scr, *, final_norm):
    def load_weights():
        wg = wg32_ref[...].astype(_BF16)
        wu = wu32_ref[...].astype(_BF16)
        wd = wd32_ref[...].astype(_BF16)
        wg_ref[...] = wg
        wu_ref[...] = wu
        wd_ref[...] = wd
        return wg, wu, wd

    _ffn_step(x_ref, gain_ref, load_weights, fgain_ref, out_ref, h_scr, final_norm)


def _ffn_rest_kernel(x_ref, gain_ref, wg_ref, wu_ref, wd_ref, fgain_ref, first_hbm, out_ref, h_scr, copy_sem,
                     *, final_norm):
    i = pl.program_id(0)

    @pl.when((i == 0) & (pl.program_id(1) == 0))
    def _():
        copy = pltpu.make_async_copy(first_hbm, out_ref, copy_sem)
        copy.start()
        copy.wait()

    @pl.when(i > 0)
    def _():
        load_weights = lambda: (wg_ref[...], wu_ref[...], wd_ref[...])
        _ffn_step(x_ref, gain_ref, load_weights, fgain_ref, out_ref, h_scr, final_norm)


def _ffn(x2d, gain, wg32, wu32, wd32, fgain, *, final_norm, name):
    t, d = x2d.shape
    d_ff = wg32.shape[1]
    tm, tf, tf0 = FFN_TOKEN_TILE, FFN_FF_TILE, FFN_FIRST_FF_TILE
    assert t % tm == 0 and d_ff % tf == 0 and d_ff % tf0 == 0
    params = pltpu.CompilerParams(dimension_semantics=("parallel", "arbitrary"),
                                  vmem_limit_bytes=V7X_VMEM_LIMIT_BYTES)
    row = lambda i, f: (0, 0)
    first, wg, wu, wd = pl.pallas_call(
        functools.partial(_ffn_first_tile_kernel, final_norm=final_norm),
        grid=(1, d_ff // tf0),
        in_specs=[
            _resident((tm, d), lambda i, f: (0, 0)),
            pl.BlockSpec((1, d), row),
            pl.BlockSpec((d, tf0), lambda i, f: (0, f)),
            pl.BlockSpec((d, tf0), lambda i, f: (0, f)),
            pl.BlockSpec((tf0, d), lambda i, f: (f, 0)),
            pl.BlockSpec((1, d), row),
        ],
        out_specs=[
            pl.BlockSpec((tm, d), lambda i, f: (0, 0)),
            pl.BlockSpec((d, tf0), lambda i, f: (0, f)),
            pl.BlockSpec((d, tf0), lambda i, f: (0, f)),
            pl.BlockSpec((tf0, d), lambda i, f: (f, 0)),
        ],
        out_shape=[
            jax.ShapeDtypeStruct((tm, d), _F32),
            jax.ShapeDtypeStruct(wg32.shape, _BF16),
            jax.ShapeDtypeStruct(wu32.shape, _BF16),
            jax.ShapeDtypeStruct(wd32.shape, _BF16),
        ],
        scratch_shapes=[pltpu.VMEM((tm, d), _BF16)],
        compiler_params=params,
        name=name + "_first",
    )(x2d, gain, wg32, wu32, wd32, fgain)
    return pl.pallas_call(
        functools.partial(_ffn_rest_kernel, final_norm=final_norm),
        grid=(t // tm, d_ff // tf),
        in_specs=[
            pl.BlockSpec((tm, d), lambda i, f: (jnp.maximum(i, 1), 0)),
            pl.BlockSpec((1, d), row),
            pl.BlockSpec((d, tf), lambda i, f: (0, jnp.where(i == 0, 0, f))),
            pl.BlockSpec((d, tf), lambda i, f: (0, jnp.where(i == 0, 0, f))),
            pl.BlockSpec((tf, d), lambda i, f: (jnp.where(i == 0, 0, f), 0)),
            pl.BlockSpec((1, d), row),
            pl.BlockSpec(memory_space=pl.ANY),
        ],
        out_specs=pl.BlockSpec((tm, d), lambda i, f: (i, 0)),
        out_shape=jax.ShapeDtypeStruct((t, d), _F32),
        scratch_shapes=[pltpu.VMEM((tm, d), _BF16), pltpu.SemaphoreType.DMA(())],
        compiler_params=params,
        name=name + "_rest",
    )(x2d, gain, wg, wu, wd, fgain, first)


def _ffn_bf16_kernel(x_ref, gain_ref, wg_ref, wu_ref, wd_ref, fgain_ref, out_ref, h_scr, *, final_norm):
    load_weights = lambda: (wg_ref[...], wu_ref[...], wd_ref[...])
    _ffn_step(x_ref, gain_ref, load_weights, fgain_ref, out_ref, h_scr, final_norm)


def _ffn_bf16(x2d, gain, wg, wu, wd, fgain, *, final_norm, name):
    t, d = x2d.shape
    d_ff = wg.shape[1]
    tm, tf = FFN_TOKEN_TILE, FFN_FF_TILE
    assert t % tm == 0 and d_ff % tf == 0
    row = lambda i, f: (0, 0)
    return pl.pallas_call(
        functools.partial(_ffn_bf16_kernel, final_norm=final_norm),
        grid=(t // tm, d_ff // tf),
        in_specs=[
            pl.BlockSpec((tm, d), lambda i, f: (i, 0)),
            pl.BlockSpec((1, d), row),
            pl.BlockSpec((d, tf), lambda i, f: (0, f)),
            pl.BlockSpec((d, tf), lambda i, f: (0, f)),
            pl.BlockSpec((tf, d), lambda i, f: (f, 0)),
            pl.BlockSpec((1, d), row),
        ],
        out_specs=pl.BlockSpec((tm, d), lambda i, f: (i, 0)),
        out_shape=jax.ShapeDtypeStruct((t, d), _F32),
        scratch_shapes=[pltpu.VMEM((tm, d), _BF16)],
        compiler_params=pltpu.CompilerParams(dimension_semantics=("parallel", "arbitrary"),
                                             vmem_limit_bytes=V7X_VMEM_LIMIT_BYTES),
        name=name,
    )(x2d, gain, wg, wu, wd, fgain)


def _in_proj_kernel(x_ref, gain_ref, pos_ref, freq_ref, wqvt_ref, wku_ref,
                    qt_ref, k_ref, vt_ref, u_ref, kmean_ref):
    tm = x_ref.shape[1]
    h = _rms(x_ref[0], gain_ref[...]).astype(_BF16)

    ku = _dot(h, wku_ref[...])
    qvt = _dot_nt(wqvt_ref[...], h)

    ang_t = freq_ref[...] * pos_ref[0].astype(_F32)
    cos_t, sin_t = jnp.cos(ang_t), jnp.sin(ang_t)
    pad = HEAD_DIM - ROT_DIM
    cos_n = jnp.transpose(jnp.concatenate([cos_t, cos_t, jnp.ones((pad, tm), _F32)], axis=0))
    sin_n = jnp.transpose(jnp.concatenate([-sin_t, sin_t, jnp.zeros((pad, tm), _F32)], axis=0))
    lane = lax.broadcasted_iota(jnp.int32, (1, HEAD_DIM), 1)

    u_ref[0] = ku[:, ATTN_WIDTH:]
    for hh in range(N_ATTN_HEADS):
        kh = ku[:, hh * HEAD_DIM:(hh + 1) * HEAD_DIM]
        partner = jnp.where(lane < ROT_HALF,
                            pltpu.roll(kh, HEAD_DIM - ROT_HALF, 1),
                            pltpu.roll(kh, ROT_HALF, 1))
        kr = kh * cos_n + partner * sin_n
        k_ref[0, hh] = kr.astype(_BF16)
        for j in range(tm // MOBA_BLOCK):
            kmean_ref[0, j, :, hh * HEAD_DIM:(hh + 1) * HEAD_DIM] = jnp.mean(
                kr[j * MOBA_BLOCK:(j + 1) * MOBA_BLOCK], axis=0, keepdims=True)

    cos_q, sin_q = cos_t * QUERY_SCALE, sin_t * QUERY_SCALE
    for hh in range(N_ATTN_HEADS):
        base = hh * HEAD_DIM
        x1 = qvt[base:base + ROT_HALF]
        x2 = qvt[base + ROT_HALF:base + ROT_DIM]
        qt_ref[0, hh, 0:ROT_HALF, :] = (x1 * cos_q - x2 * sin_q).astype(_BF16)
        qt_ref[0, hh, ROT_HALF:ROT_DIM, :] = (x2 * cos_q + x1 * sin_q).astype(_BF16)
        qt_ref[0, hh, ROT_DIM:, :] = (qvt[base + ROT_DIM:base + HEAD_DIM] * QUERY_SCALE).astype(_BF16)

    ones = jnp.ones((V_ROWS - HEAD_DIM, tm), _BF16)
    for hh in range(N_ATTN_HEADS):
        base = ATTN_WIDTH + hh * HEAD_DIM
        vt_ref[0, hh, 0:HEAD_DIM, :] = qvt[base:base + HEAD_DIM].astype(_BF16)
        vt_ref[0, hh, HEAD_DIM:, :] = ones


def _in_proj(x1, gain, positions, freq, wqvt, wku):
    b, s, d = x1.shape
    tm = PROJ_TOKEN_TILE
    assert s % tm == 0 and tm % MOBA_BLOCK == 0
    nb = s // MOBA_BLOCK
    bpt = tm // MOBA_BLOCK
    pos_row = positions.reshape(b, 1, s)
    const2 = lambda bi, i: (0, 0)
    return pl.pallas_call(
        _in_proj_kernel,
        grid=(b, s // tm),
        in_specs=[
            pl.BlockSpec((1, tm, d), lambda bi, i: (bi, i, 0)),
            pl.BlockSpec((1, d), const2),
            pl.BlockSpec((1, 1, tm), lambda bi, i: (bi, 0, i)),
            pl.BlockSpec((ROT_HALF, 1), const2),
            _resident((2 * ATTN_WIDTH, d), const2),
            _resident((d, ATTN_WIDTH + POOL_WIDTH), const2),
        ],
        out_specs=[
            pl.BlockSpec((1, N_ATTN_HEADS, HEAD_DIM, tm), lambda bi, i: (bi, 0, 0, i)),
            pl.BlockSpec((1, N_ATTN_HEADS, tm, HEAD_DIM), lambda bi, i: (bi, 0, i, 0)),
            pl.BlockSpec((1, N_ATTN_HEADS, V_ROWS, tm), lambda bi, i: (bi, 0, 0, i)),
            pl.BlockSpec((1, tm, POOL_WIDTH), lambda bi, i: (bi, i, 0)),
            pl.BlockSpec((1, bpt, 1, ATTN_WIDTH), lambda bi, i: (bi, i, 0, 0)),
        ],
        out_shape=[
            jax.ShapeDtypeStruct((b, N_ATTN_HEADS, HEAD_DIM, s), _BF16),
            jax.ShapeDtypeStruct((b, N_ATTN_HEADS, s, HEAD_DIM), _BF16),
            jax.ShapeDtypeStruct((b, N_ATTN_HEADS, V_ROWS, s), _BF16),
            jax.ShapeDtypeStruct((b, s, POOL_WIDTH), _F32),
            jax.ShapeDtypeStruct((b, nb, 1, ATTN_WIDTH), _F32),
        ],
        compiler_params=pltpu.CompilerParams(
            dimension_semantics=("parallel", "parallel"),
            vmem_limit_bytes=V7X_VMEM_LIMIT_BYTES),
        name="in_proj",
    )(x1, gain, pos_row, freq, wqvt, wku)


def _moba_kernel(qt_ref, k_ref, vt_ref, kmean_ref, *rest, nb, hg, n_riders):
    rider_in, rest = rest[:n_riders], rest[n_riders:]
    out_ref, rider_out, rest = rest[0], rest[1:1 + n_riders], rest[1 + n_riders:]
    bias_scr, acc_scr, m_scr, s_even, s_odd, top_even, top_odd = rest
    for src_ref, dst_ref in zip(rider_in, rider_out):
        dst_ref[...] = src_ref[...].astype(_BF16)

    i = pl.program_id(2)
    even, odd = (s_even, top_even), (s_odd, top_odd)
    blk = MOBA_BLOCK
    kb_id = lax.broadcasted_iota(jnp.int32, (nb, blk), 0)
    past = kb_id < i
    kb_f = kb_id.astype(_F32)

    def select_blocks(h):
        km = kmean_ref[0, :, h * HEAD_DIM:(h + 1) * HEAD_DIM]
        km_hi = km.astype(_BF16)
        km_lo = (km - km_hi.astype(_F32)).astype(_BF16)
        gate = _dot(km_hi, qt_ref[0, h]) + _dot(km_lo, qt_ref[0, h])
        gate = jnp.where(past, gate, NEG_INF)
        bias = jnp.full((nb, blk), NEG_INF, _F32)
        for _ in range(MOBA_TOPK):
            top = jnp.max(gate, axis=0, keepdims=True)
            first = jnp.min(jnp.where(gate == top, kb_f, float(nb)), axis=0, keepdims=True)
            hit = kb_f == first
            bias = jnp.where(hit, 0.0, bias)
            gate = jnp.where(hit, -jnp.inf, gate)
        bias_scr[h] = jnp.where(past, bias, NEG_INF)

    def scores(h, n):
        kblk = k_ref[0, h, pl.ds(pl.multiple_of(n * blk, blk), blk), :]
        return _dot(kblk, qt_ref[0, h])

    def stage(h, n, staged):
        s_buf, top_buf = staged
        s = scores(h, n)
        s_buf[h] = s
        top_buf[h] = jnp.max(s, axis=0, keepdims=True)

    def weighted_values(h, n, p):
        vblk = vt_ref[0, h, :, pl.ds(pl.multiple_of(n * blk, blk), blk)]
        return _dot(vblk, p)

    key_pos = lax.broadcasted_iota(jnp.int32, (blk, blk), 0)
    qry_pos = lax.broadcasted_iota(jnp.int32, (blk, blk), 1)
    causal = key_pos <= qry_pos
    ps = []
    ss = [jnp.where(causal, scores(h, i), NEG_INF) for h in range(hg)]
    for h in range(hg):
        select_blocks(h)
    for h in range(hg):
        m_own = jnp.max(ss[h], axis=0, keepdims=True)
        m_scr[h] = m_own
        ps.append(jnp.exp2(ss[h] - m_own).astype(_BF16))
    for h in range(hg):
        acc_scr[h] = weighted_values(h, i, ps[h])
        stage(h, 0, even)

    def step(n, cur, nxt, stage_next):
        n_nxt = jnp.minimum(n + 1, nb - 1)
        for h in range(hg):
            if stage_next:
                stage(h, n_nxt, nxt)
            s = cur[0][h]
            b_n = bias_scr[h, pl.ds(n, 1), :]
            m_run = m_scr[h]
            m_new = jnp.maximum(m_run, cur[1][h] + b_n)
            m_scr[h] = m_new
            alpha = jnp.exp2(m_run - m_new)
            p = jnp.exp2(s - (m_new - b_n)).astype(_BF16)
            acc_scr[h] = alpha * acc_scr[h] + weighted_values(h, n, p)

    def run(n0, count, stage_last=True):
        for c in range(count):
            bufs = (even, odd) if c % 2 == 0 else (odd, even)
            step(n0 + c, *bufs, stage_next=stage_last or c + 1 < count)

    def body(j, carry):
        run(4 * j, 4)
        return carry

    lax.fori_loop(0, i // 4, body, 0)

    @pl.when(i % 4 >= 2)
    def _():
        run((i // 4) * 4, 2)

    @pl.when(i % 2 == 1)
    def _():
        run(i - 1, 1, stage_last=False)

    for h in range(hg):
        acc = acc_scr[h]
        out_ref[0, :, h * HEAD_DIM:(h + 1) * HEAD_DIM] = jnp.transpose(
            acc[0:HEAD_DIM] / acc[HEAD_DIM:HEAD_DIM + 1]).astype(out_ref.dtype)


def _moba(qt, k, vt, kmean, riders):
    b, nh, hd, s = qt.shape
    nb = s // MOBA_BLOCK
    hg = MOBA_HEADS_PER_STEP
    assert nh % hg == 0
    groups = nh // hg
    n_steps = b * groups * nb
    rider_specs = []
    for w in riders:
        n_slices = next(n for n in range(n_steps, 0, -1)
                        if w.shape[0] % n == 0 and (w.shape[0] // n) % V7X_BF16_ROWS_PER_VREG == 0)
        rider_specs.append(pl.BlockSpec(
            (w.shape[0] // n_slices, w.shape[1]),
            lambda bi, g, i, n_slices=n_slices: (jnp.minimum((bi * groups + g) * nb + i, n_slices - 1), 0)))
    outs = pl.pallas_call(
        functools.partial(_moba_kernel, nb=nb, hg=hg, n_riders=len(riders)),
        grid=(b, groups, nb),
        in_specs=[
            pl.BlockSpec((1, hg, hd, MOBA_BLOCK), lambda bi, g, i: (bi, g, 0, i)),
            pl.BlockSpec((1, hg, s, hd), lambda bi, g, i: (bi, g, 0, 0)),
            pl.BlockSpec((1, hg, V_ROWS, s), lambda bi, g, i: (bi, g, 0, 0)),
            pl.BlockSpec((1, nb, hg * hd), lambda bi, g, i: (bi, 0, g)),
        ] + rider_specs,
        out_specs=[pl.BlockSpec((1, MOBA_BLOCK, hg * hd), lambda bi, g, i: (bi, i, g))] + rider_specs,
        out_shape=[jax.ShapeDtypeStruct((b, s, nh * hd), _BF16)]
                  + [jax.ShapeDtypeStruct(w.shape, _BF16) for w in riders],
        scratch_shapes=[pltpu.VMEM((hg, nb, MOBA_BLOCK), _F32),
                        pltpu.VMEM((hg, V_ROWS, MOBA_BLOCK), _F32),
                        pltpu.VMEM((hg, 1, MOBA_BLOCK), _F32),
                        pltpu.VMEM((hg, MOBA_BLOCK, MOBA_BLOCK), _F32),
                        pltpu.VMEM((hg, MOBA_BLOCK, MOBA_BLOCK), _F32),
                        pltpu.VMEM((hg, 1, MOBA_BLOCK), _F32),
                        pltpu.VMEM((hg, 1, MOBA_BLOCK), _F32)],
        compiler_params=pltpu.CompilerParams(
            dimension_semantics=("parallel", "parallel", "arbitrary"),
            vmem_limit_bytes=V7X_VMEM_LIMIT_BYTES),
        name="moba",
    )(qt, k, vt, kmean, *riders)
    return outs[0], outs[1:]


def _mix_out_kernel(x_ref, attn_ref, u_ref, halo_ref, pw_ref, ps_ref, wo_ref, out_ref, ubuf, ybuf):
    tm = x_ref.shape[1]
    si = pl.program_id(1)
    ubuf[0:POOL_HALO, :] = jnp.where(si > 0, halo_ref[0], 0.0)
    ubuf[POOL_HALO:, :] = u_ref[0]

    t = si * tm + lax.broadcasted_iota(jnp.int32, (tm, 1), 0)
    mix = _dot(attn_ref[0], wo_ref[0:ATTN_WIDTH, :])
    for g, w in enumerate(POOL_WINDOWS):
        cols = slice(g * POOL_GROUP_DIM, (g + 1) * POOL_GROUP_DIM)
        ext = ubuf[:, cols]
        wsum = ext
        span = 1
        while span < w:
            wsum = wsum + pltpu.roll(wsum, span, 0)
            span *= 2
        own = ext[POOL_HALO:]
        count = jnp.minimum(t + 1, w).astype(_F32)
        pooled = wsum[POOL_HALO:] / count - own
        y = _dot(pooled.astype(_BF16), pw_ref[g]) * ps_ref[:, cols]
        ybuf[:, cols] = y.astype(_BF16)
    mix = mix + _dot(ybuf[...], wo_ref[ATTN_WIDTH:, :])
    out_ref[0] = x_ref[0] + mix


def _mix_out(x1, attn, u, pool_w, pool_scale, w_out):
    b, s, d = x1.shape
    tm = MIX_TOKEN_TILE
    assert s % tm == 0 and tm % POOL_HALO == 0
    assert all(w <= POOL_HALO and w & (w - 1) == 0 for w in POOL_WINDOWS)
    halo_per_tile = tm // POOL_HALO
    const2 = lambda bi, i: (0, 0)
    return pl.pallas_call(
        _mix_out_kernel,
        grid=(b, s // tm),
        in_specs=[
            pl.BlockSpec((1, tm, d), lambda bi, i: (bi, i, 0)),
            pl.BlockSpec((1, tm, ATTN_WIDTH), lambda bi, i: (bi, i, 0)),
            pl.BlockSpec((1, tm, POOL_WIDTH), lambda bi, i: (bi, i, 0)),
            pl.BlockSpec((1, POOL_HALO, POOL_WIDTH),
                         lambda bi, i: (bi, jnp.maximum(i * halo_per_tile - 1, 0), 0)),
            _resident(pool_w.shape, lambda bi, i: (0, 0, 0)),
            pl.BlockSpec((1, POOL_WIDTH), const2),
            _resident(w_out.shape, const2),
        ],
        out_specs=pl.BlockSpec((1, tm, d), lambda bi, i: (bi, i, 0)),
        out_shape=jax.ShapeDtypeStruct((b, s, d), _F32),
        scratch_shapes=[pltpu.VMEM((POOL_HALO + tm, POOL_WIDTH), _F32), pltpu.VMEM((tm, POOL_WIDTH), _BF16)],
        compiler_params=pltpu.CompilerParams(
            dimension_semantics=("parallel", "parallel"),
            vmem_limit_bytes=V7X_VMEM_LIMIT_BYTES),
        name="mix_out",
    )(x1, attn, u, u, pool_w, pool_scale, w_out)


def kernel(x, positions, norm_ffn1, w1_gate, w1_up, w1_down, norm_mix, w_in, pool_w, pool_scale,
           w_out, norm_ffn2, w2_gate, w2_up, w2_down, norm_final):
    b, s, d = x.shape
    depth = w_in.shape[0]
    inv_freq = ROPE_THETA ** (-jnp.arange(0, ROT_DIM, 2, dtype=_F32) / ROT_DIM)
    freq = inv_freq[:, None]
    fgain = norm_final[None, :]
    bf = lambda w: w.astype(_BF16)

    for l in range(depth):
        x1 = _ffn(x.reshape(b * s, d), norm_ffn1[l][None, :], w1_gate[l], w1_up[l], w1_down[l],
                  fgain, final_norm=False, name="ffn_pre").reshape(b, s, d)
        wq, wk, wv, wu = jnp.split(bf(w_in[l]), [ATTN_WIDTH, 2 * ATTN_WIDTH, 3 * ATTN_WIDTH], axis=-1)
        wqvt = jnp.concatenate([wq, wv], axis=1).T
        wku = jnp.concatenate([wk, wu], axis=1)
        qt, k, vt, u, kmean = _in_proj(x1, norm_mix[l][None, :], positions, freq, wqvt, wku)
        riders = [w2_gate[l], w2_up[l], w2_down[l], w_out[l], pool_w[l].reshape(-1, POOL_GROUP_DIM)]
        attn, (wg2, wu2, wd2, wo, pw) = _moba(qt, k, vt, kmean.reshape(b, s // MOBA_BLOCK, ATTN_WIDTH), riders)
        x2 = _mix_out(x1, attn, u, pw.reshape(pool_w.shape[1:]), pool_scale[l][None, :], wo)
        last = l == depth - 1
        x = _ffn_bf16(x2.reshape(b * s, d), norm_ffn2[l][None, :], wg2, wu2, wd2,
                      fgain, final_norm=last, name="ffn_post").reshape(b, s, d)
    return x
```

```python
import functools

import jax
import jax.numpy as jnp
from jax import lax
from jax.experimental import pallas as pl
from jax.experimental.pallas import tpu as pltpu

N_ATTN_HEADS = 8
HEAD_DIM = 128
ATTN_WIDTH = N_ATTN_HEADS * HEAD_DIM
POOL_WINDOWS = (2, 4, 8, 16)
POOL_GROUP_DIM = 256
POOL_WIDTH = POOL_GROUP_DIM * len(POOL_WINDOWS)
MOBA_BLOCK = 256
MOBA_TOPK = 3
ROT_DIM = HEAD_DIM // 4
ROT_HALF = ROT_DIM // 2
ROPE_THETA = 500000.0
EPS = 1e-6
NEG_INF = -1e30
LOG2_E = 1.4426950408889634
QUERY_SCALE = (HEAD_DIM ** -0.5) * LOG2_E
V7X_BF16_ROWS_PER_VREG = 16
V_ROWS = HEAD_DIM + V7X_BF16_ROWS_PER_VREG
POOL_HALO = 16

V7X_VMEM_LIMIT_BYTES = 58 * 1024 * 1024

FFN_TOKEN_TILE = 1024
FFN_FF_TILE = 512
FFN_FIRST_FF_TILE = 256
PROJ_TOKEN_TILE = 512
MIX_TOKEN_TILE = 512
MOBA_HEADS_PER_STEP = 8

_BF16 = jnp.bfloat16
_F32 = jnp.float32


def _rms(x, gain):
    inv = lax.rsqrt(jnp.mean(x * x, axis=-1, keepdims=True) + EPS)
    return x * inv * gain


def _dot(a, b):
    return jnp.dot(a, b, preferred_element_type=_F32)


def _dot_nt(a, b):
    return lax.dot_general(a, b, (((1,), (1,)), ((), ())), preferred_element_type=_F32)


def _resident(block_shape, index_map):
    return pl.BlockSpec(block_shape, index_map, pipeline_mode=pl.Buffered(1))


def _cast_riders(rider_in, rider_out):
    for src_ref, dst_ref in zip(rider_in, rider_out):
        dst_ref[...] = src_ref[...].astype(_BF16)


def _rider_specs(riders, n_steps, step_of):
    specs = []
    for w in riders:
        n_slices = next(n for n in range(n_steps, 0, -1)
                        if w.shape[0] % n == 0 and (w.shape[0] // n) % V7X_BF16_ROWS_PER_VREG == 0)
        specs.append(pl.BlockSpec(
            (w.shape[0] // n_slices, w.shape[1]),
            lambda *ids, n_slices=n_slices: (jnp.minimum(step_of(*ids), n_slices - 1), 0)))
    return specs


def _ffn_step(x_ref, gain_ref, load_weights, fgain_ref, out_ref, h_scr, final_norm):
    f = pl.program_id(1)

    @pl.when(f == 0)
    def _():
        x = x_ref[...]
        h_scr[...] = _rms(x, gain_ref[...]).astype(_BF16)
        out_ref[...] = x

    wg, wu, wd = load_weights()
    h = h_scr[...]
    g = _dot(h, wg)
    u = _dot(h, wu)
    a = (g * jax.nn.sigmoid(g) * (0.5 * u)).astype(_BF16)
    out_ref[...] += _dot(a, wd)

    if final_norm:
        @pl.when(f == pl.num_programs(1) - 1)
        def _():
            out_ref[...] = _rms(out_ref[...], fgain_ref[...])


def _ffn_first_tile_kernel(x_ref, gain_ref, wg32_ref, wu32_ref, wd32_ref, fgain_ref,
                           out_ref, wg_ref, wu_ref, wd_ref, h_scr, *, final_norm):
    def load_weights():
        wg = wg32_ref[...].astype(_BF16)
        wu = wu32_ref[...].astype(_BF16)
        wd = wd32_ref[...].astype(_BF16)
        wg_ref[...] = wg
        wu_ref[...] = wu
        wd_ref[...] = wd
        return wg, wu, wd

    _ffn_step(x_ref, gain_ref, load_weights, fgain_ref, out_ref, h_scr, final_norm)


def _ffn_rest_kernel(x_ref, gain_ref, wg_ref, wu_ref, wd_ref, fgain_ref, first_hbm, *rest, final_norm, n_riders):
    rider_in, rest = rest[:n_riders], rest[n_riders:]
    out_ref, rider_out, (h_scr, copy_sem) = rest[0], rest[1:1 + n_riders], rest[1 + n_riders:]
    _cast_riders(rider_in, rider_out)
    i = pl.program_id(0)

    @pl.when((i == 0) & (pl.program_id(1) == 0))
    def _():
        copy = pltpu.make_async_copy(first_hbm, out_ref, copy_sem)
        copy.start()
        copy.wait()

    @pl.when(i > 0)
    def _():
        load_weights = lambda: (wg_ref[...], wu_ref[...], wd_ref[...])
        _ffn_step(x_ref, gain_ref, load_weights, fgain_ref, out_ref, h_scr, final_norm)


def _ffn(x2d, gain, wg32, wu32, wd32, fgain, riders, *, final_norm, name):
    t, d = x2d.shape
    d_ff = wg32.shape[1]
    tm, tf, tf0 = FFN_TOKEN_TILE, FFN_FF_TILE, FFN_FIRST_FF_TILE
    assert t % tm == 0 and d_ff % tf == 0 and d_ff % tf0 == 0
    params = pltpu.CompilerParams(dimension_semantics=("parallel", "arbitrary"),
                                  vmem_limit_bytes=V7X_VMEM_LIMIT_BYTES)
    row = lambda i, f: (0, 0)
    first, wg, wu, wd = pl.pallas_call(
        functools.partial(_ffn_first_tile_kernel, final_norm=final_norm),
        grid=(1, d_ff // tf0),
        in_specs=[
            _resident((tm, d), lambda i, f: (0, 0)),
            pl.BlockSpec((1, d), row),
            pl.BlockSpec((d, tf0), lambda i, f: (0, f)),
            pl.BlockSpec((d, tf0), lambda i, f: (0, f)),
            pl.BlockSpec((tf0, d), lambda i, f: (f, 0)),
            pl.BlockSpec((1, d), row),
        ],
        out_specs=[
            pl.BlockSpec((tm, d), lambda i, f: (0, 0)),
            pl.BlockSpec((d, tf0), lambda i, f: (0, f)),
            pl.BlockSpec((d, tf0), lambda i, f: (0, f)),
            pl.BlockSpec((tf0, d), lambda i, f: (f, 0)),
        ],
        out_shape=[
            jax.ShapeDtypeStruct((tm, d), _F32),
            jax.ShapeDtypeStruct(wg32.shape, _BF16),
            jax.ShapeDtypeStruct(wu32.shape, _BF16),
            jax.ShapeDtypeStruct(wd32.shape, _BF16),
        ],
        scratch_shapes=[pltpu.VMEM((tm, d), _BF16)],
        compiler_params=params,
        name=name + "_first",
    )(x2d, gain, wg32, wu32, wd32, fgain)
    n_f = d_ff // tf
    rider_specs = _rider_specs(riders, (t // tm) * n_f, lambda i, f: i * n_f + f)
    outs = pl.pallas_call(
        functools.partial(_ffn_rest_kernel, final_norm=final_norm, n_riders=len(riders)),
        grid=(t // tm, n_f),
        in_specs=[
            pl.BlockSpec((tm, d), lambda i, f: (jnp.maximum(i, 1), 0)),
            pl.BlockSpec((1, d), row),
            pl.BlockSpec((d, tf), lambda i, f: (0, jnp.where(i == 0, 0, f))),
            pl.BlockSpec((d, tf), lambda i, f: (0, jnp.where(i == 0, 0, f))),
            pl.BlockSpec((tf, d), lambda i, f: (jnp.where(i == 0, 0, f), 0)),
            pl.BlockSpec((1, d), row),
            pl.BlockSpec(memory_space=pl.ANY),
        ] + rider_specs,
        out_specs=[pl.BlockSpec((tm, d), lambda i, f: (i, 0))] + rider_specs,
        out_shape=[jax.ShapeDtypeStruct((t, d), _F32)] + [jax.ShapeDtypeStruct(w.shape, _BF16) for w in riders],
        scratch_shapes=[pltpu.VMEM((tm, d), _BF16), pltpu.SemaphoreType.DMA(())],
        compiler_params=params,
        name=name + "_rest",
    )(x2d, gain, wg, wu, wd, fgain, first, *riders)
    return outs[0], outs[1:]


def _ffn_bf16_kernel(x_ref, gain_ref, wg_ref, wu_ref, wd_ref, fgain_ref, out_ref, h_scr, *, final_norm):
    load_weights = lambda: (wg_ref[...], wu_ref[...], wd_ref[...])
    _ffn_step(x_ref, gain_ref, load_weights, fgain_ref, out_ref, h_scr, final_norm)


def _ffn_bf16(x2d, gain, wg, wu, wd, fgain, *, final_norm, name):
    t, d = x2d.shape
    d_ff = wg.shape[1]
    tm, tf = FFN_TOKEN_TILE, FFN_FF_TILE
    assert t % tm == 0 and d_ff % tf == 0
    row = lambda i, f: (0, 0)
    return pl.pallas_call(
        functools.partial(_ffn_bf16_kernel, final_norm=final_norm),
        grid=(t // tm, d_ff // tf),
        in_specs=[
            pl.BlockSpec((tm, d), lambda i, f: (i, 0)),
            pl.BlockSpec((1, d), row),
            pl.BlockSpec((d, tf), lambda i, f: (0, f)),
            pl.BlockSpec((d, tf), lambda i, f: (0, f)),
            pl.BlockSpec((tf, d), lambda i, f: (f, 0)),
            pl.BlockSpec((1, d), row),
        ],
        out_specs=pl.BlockSpec((tm, d), lambda i, f: (i, 0)),
        out_shape=jax.ShapeDtypeStruct((t, d), _F32),
        scratch_shapes=[pltpu.VMEM((tm, d), _BF16)],
        compiler_params=pltpu.CompilerParams(dimension_semantics=("parallel", "arbitrary"),
                                             vmem_limit_bytes=V7X_VMEM_LIMIT_BYTES),
        name=name,
    )(x2d, gain, wg, wu, wd, fgain)


def _in_proj_kernel(x_ref, gain_ref, pos_ref, freq_ref, wqvt_ref, wk_ref, wu_ref,
                    qt_ref, k_ref, vt_ref, u_ref, kmean_ref):
    tm = x_ref.shape[1]
    h = _rms(x_ref[0], gain_ref[...]).astype(_BF16)

    k = _dot(h, wk_ref[...])
    u = _dot(h, wu_ref[...])
    qvt = _dot_nt(wqvt_ref[...], h)

    ang_t = freq_ref[...] * pos_ref[0].astype(_F32)
    cos_t, sin_t = jnp.cos(ang_t), jnp.sin(ang_t)
    pad = HEAD_DIM - ROT_DIM
    cos_n = jnp.transpose(jnp.concatenate([cos_t, cos_t, jnp.ones((pad, tm), _F32)], axis=0))
    sin_n = jnp.transpose(jnp.concatenate([-sin_t, sin_t, jnp.zeros((pad, tm), _F32)], axis=0))
    lane = lax.broadcasted_iota(jnp.int32, (1, HEAD_DIM), 1)

    u_ref[0] = u
    for hh in range(N_ATTN_HEADS):
        kh = k[:, hh * HEAD_DIM:(hh + 1) * HEAD_DIM]
        partner = jnp.where(lane < ROT_HALF,
                            pltpu.roll(kh, HEAD_DIM - ROT_HALF, 1),
                            pltpu.roll(kh, ROT_HALF, 1))
        kr = kh * cos_n + partner * sin_n
        k_ref[0, hh] = kr.astype(_BF16)
        for j in range(tm // MOBA_BLOCK):
            kmean_ref[0, j, :, hh * HEAD_DIM:(hh + 1) * HEAD_DIM] = jnp.mean(
                kr[j * MOBA_BLOCK:(j + 1) * MOBA_BLOCK], axis=0, keepdims=True)

    cos_q, sin_q = cos_t * QUERY_SCALE, sin_t * QUERY_SCALE
    for hh in range(N_ATTN_HEADS):
        base = hh * HEAD_DIM
        x1 = qvt[base:base + ROT_HALF]
        x2 = qvt[base + ROT_HALF:base + ROT_DIM]
        qt_ref[0, hh, 0:ROT_HALF, :] = (x1 * cos_q - x2 * sin_q).astype(_BF16)
        qt_ref[0, hh, ROT_HALF:ROT_DIM, :] = (x2 * cos_q + x1 * sin_q).astype(_BF16)
        qt_ref[0, hh, ROT_DIM:, :] = (qvt[base + ROT_DIM:base + HEAD_DIM] * QUERY_SCALE).astype(_BF16)

    ones = jnp.ones((V_ROWS - HEAD_DIM, tm), _BF16)
    for hh in range(N_ATTN_HEADS):
        base = ATTN_WIDTH + hh * HEAD_DIM
        vt_ref[0, hh, 0:HEAD_DIM, :] = qvt[base:base + HEAD_DIM].astype(_BF16)
        vt_ref[0, hh, HEAD_DIM:, :] = ones


def _in_proj(x1, gain, positions, freq, wqvt, w_in):
    b, s, d = x1.shape
    tm = PROJ_TOKEN_TILE
    assert s % tm == 0 and tm % MOBA_BLOCK == 0
    nb = s // MOBA_BLOCK
    bpt = tm // MOBA_BLOCK
    pos_row = positions.reshape(b, 1, s)
    const2 = lambda bi, i: (0, 0)
    return pl.pallas_call(
        _in_proj_kernel,
        grid=(b, s // tm),
        in_specs=[
            pl.BlockSpec((1, tm, d), lambda bi, i: (bi, i, 0)),
            pl.BlockSpec((1, d), const2),
            pl.BlockSpec((1, 1, tm), lambda bi, i: (bi, 0, i)),
            pl.BlockSpec((ROT_HALF, 1), const2),
            _resident((2 * ATTN_WIDTH, d), const2),
            _resident((d, ATTN_WIDTH), lambda bi, i: (0, 1)),
            _resident((d, POOL_WIDTH), lambda bi, i: (0, 3)),
        ],
        out_specs=[
            pl.BlockSpec((1, N_ATTN_HEADS, HEAD_DIM, tm), lambda bi, i: (bi, 0, 0, i)),
            pl.BlockSpec((1, N_ATTN_HEADS, tm, HEAD_DIM), lambda bi, i: (bi, 0, i, 0)),
            pl.BlockSpec((1, N_ATTN_HEADS, V_ROWS, tm), lambda bi, i: (bi, 0, 0, i)),
            pl.BlockSpec((1, tm, POOL_WIDTH), lambda bi, i: (bi, i, 0)),
            pl.BlockSpec((1, bpt, 1, ATTN_WIDTH), lambda bi, i: (bi, i, 0, 0)),
        ],
        out_shape=[
            jax.ShapeDtypeStruct((b, N_ATTN_HEADS, HEAD_DIM, s), _BF16),
            jax.ShapeDtypeStruct((b, N_ATTN_HEADS, s, HEAD_DIM), _BF16),
            jax.ShapeDtypeStruct((b, N_ATTN_HEADS, V_ROWS, s), _BF16),
            jax.ShapeDtypeStruct((b, s, POOL_WIDTH), _F32),
            jax.ShapeDtypeStruct((b, nb, 1, ATTN_WIDTH), _F32),
        ],
        compiler_params=pltpu.CompilerParams(
            dimension_semantics=("parallel", "parallel"),
            vmem_limit_bytes=V7X_VMEM_LIMIT_BYTES),
        name="in_proj",
    )(x1, gain, pos_row, freq, wqvt, w_in, w_in)


def _moba_kernel(qt_ref, k_ref, vt_ref, kmean_ref, *rest, nb, hg, n_riders):
    rider_in, rest = rest[:n_riders], rest[n_riders:]
    out_ref, rider_out, rest = rest[0], rest[1:1 + n_riders], rest[1 + n_riders:]
    bias_scr, acc_scr, m_scr, s_even, s_odd, top_even, top_odd = rest
    _cast_riders(rider_in, rider_out)

    i = pl.program_id(2)
    even, odd = (s_even, top_even), (s_odd, top_odd)
    blk = MOBA_BLOCK
    kb_id = lax.broadcasted_iota(jnp.int32, (nb, blk), 0)
    past = kb_id < i
    kb_f = kb_id.astype(_F32)

    def select_blocks(h):
        km = kmean_ref[0, :, h * HEAD_DIM:(h + 1) * HEAD_DIM]
        km_hi = km.astype(_BF16)
        km_lo = (km - km_hi.astype(_F32)).astype(_BF16)
        gate = _dot(km_hi, qt_ref[0, h]) + _dot(km_lo, qt_ref[0, h])
        gate = jnp.where(past, gate, NEG_INF)
        bias = jnp.full((nb, blk), NEG_INF, _F32)
        for _ in range(MOBA_TOPK):
            top = jnp.max(gate, axis=0, keepdims=True)
            first = jnp.min(jnp.where(gate == top, kb_f, float(nb)), axis=0, keepdims=True)
            hit = kb_f == first
            bias = jnp.where(hit, 0.0, bias)
            gate = jnp.where(hit, -jnp.inf, gate)
        bias_scr[h] = jnp.where(past, bias, NEG_INF)

    def scores(h, n):
        kblk = k_ref[0, h, pl.ds(pl.multiple_of(n * blk, blk), blk), :]
        return _dot(kblk, qt_ref[0, h])

    def stage(h, n, staged):
        s_buf, top_buf = staged
        s = scores(h, n)
        s_buf[h] = s
        top_buf[h] = jnp.max(s, axis=0, keepdims=True)

    def weighted_values(h, n, p):
        vblk = vt_ref[0, h, :, pl.ds(pl.multiple_of(n * blk, blk), blk)]
        return _dot(vblk, p)

    key_pos = lax.broadcasted_iota(jnp.int32, (blk, blk), 0)
    qry_pos = lax.broadcasted_iota(jnp.int32, (blk, blk), 1)
    causal = key_pos <= qry_pos
    ps = []
    ss = [jnp.where(causal, scores(h, i), NEG_INF) for h in range(hg)]
    for h in range(hg):
        select_blocks(h)
    for h in range(hg):
        m_own = jnp.max(ss[h], axis=0, keepdims=True)
        m_scr[h] = m_own
        ps.append(jnp.exp2(ss[h] - m_own).astype(_BF16))
    for h in range(hg):
        acc_scr[h] = weighted_values(h, i, ps[h])
        stage(h, 0, even)

    def step(n, cur, nxt, stage_next):
        n_nxt = jnp.minimum(n + 1, nb - 1)
        for h in range(hg):
            if stage_next:
                stage(h, n_nxt, nxt)
            s = cur[0][h]
            b_n = bias_scr[h, pl.ds(n, 1), :]
            m_run = m_scr[h]
            m_new = jnp.maximum(m_run, cur[1][h] + b_n)
            m_scr[h] = m_new
            alpha = jnp.exp2(m_run - m_new)
            p = jnp.exp2(s - (m_new - b_n)).astype(_BF16)
            acc_scr[h] = alpha * acc_scr[h] + weighted_values(h, n, p)

    def run(n0, count, stage_last=True):
        for c in range(count):
            bufs = (even, odd) if c % 2 == 0 else (odd, even)
            step(n0 + c, *bufs, stage_next=stage_last or c + 1 < count)

    def body(j, carry):
        run(4 * j, 4)
        return carry

    lax.fori_loop(0, i // 4, body, 0)

    @pl.when(i % 4 >= 2)
    def _():
        run((i // 4) * 4, 2)

    @pl.when(i % 2 == 1)
    def _():
        run(i - 1, 1, stage_last=False)

    for h in range(hg):
        acc = acc_scr[h]
        out_ref[0, :, h * HEAD_DIM:(h + 1) * HEAD_DIM] = jnp.transpose(
            acc[0:HEAD_DIM] / acc[HEAD_DIM:HEAD_DIM + 1]).astype(out_ref.dtype)


def _moba(qt, k, vt, kmean, riders):
    b, nh, hd, s = qt.shape
    nb = s // MOBA_BLOCK
    hg = MOBA_HEADS_PER_STEP
    assert nh % hg == 0
    groups = nh // hg
    n_steps = b * groups * nb
    rider_specs = _rider_specs(riders, n_steps, lambda bi, g, i: (bi * groups + g) * nb + i)
    outs = pl.pallas_call(
        functools.partial(_moba_kernel, nb=nb, hg=hg, n_riders=len(riders)),
        grid=(b, groups, nb),
        in_specs=[
            pl.BlockSpec((1, hg, hd, MOBA_BLOCK), lambda bi, g, i: (bi, g, 0, i)),
            pl.BlockSpec((1, hg, s, hd), lambda bi, g, i: (bi, g, 0, 0)),
            pl.BlockSpec((1, hg, V_ROWS, s), lambda bi, g, i: (bi, g, 0, 0)),
            pl.BlockSpec((1, nb, hg * hd), lambda bi, g, i: (bi, 0, g)),
        ] + rider_specs,
        out_specs=[pl.BlockSpec((1, MOBA_BLOCK, hg * hd), lambda bi, g, i: (bi, i, g))] + rider_specs,
        out_shape=[jax.ShapeDtypeStruct((b, s, nh * hd), _BF16)]
                  + [jax.ShapeDtypeStruct(w.shape, _BF16) for w in riders],
        scratch_shapes=[pltpu.VMEM((hg, nb, MOBA_BLOCK), _F32),
                        pltpu.VMEM((hg, V_ROWS, MOBA_BLOCK), _F32),
                        pltpu.VMEM((hg, 1, MOBA_BLOCK), _F32),
                        pltpu.VMEM((hg, MOBA_BLOCK, MOBA_BLOCK), _F32),
                        pltpu.VMEM((hg, MOBA_BLOCK, MOBA_BLOCK), _F32),
                        pltpu.VMEM((hg, 1, MOBA_BLOCK), _F32),
                        pltpu.VMEM((hg, 1, MOBA_BLOCK), _F32)],
        compiler_params=pltpu.CompilerParams(
            dimension_semantics=("parallel", "parallel", "arbitrary"),
            vmem_limit_bytes=V7X_VMEM_LIMIT_BYTES),
        name="moba",
    )(qt, k, vt, kmean, *riders)
    return outs[0], outs[1:]


def _mix_out_kernel(x_ref, attn_ref, u_ref, halo_ref, pw_ref, ps_ref, wo_ref, out_ref, ubuf, ybuf):
    tm = x_ref.shape[1]
    si = pl.program_id(1)
    ubuf[0:POOL_HALO, :] = jnp.where(si > 0, halo_ref[0], 0.0)
    ubuf[POOL_HALO:, :] = u_ref[0]

    t = si * tm + lax.broadcasted_iota(jnp.int32, (tm, 1), 0)
    mix = _dot(attn_ref[0], wo_ref[0:ATTN_WIDTH, :])
    for g, w in enumerate(POOL_WINDOWS):
        cols = slice(g * POOL_GROUP_DIM, (g + 1) * POOL_GROUP_DIM)
        ext = ubuf[:, cols]
        wsum = ext
        span = 1
        while span < w:
            wsum = wsum + pltpu.roll(wsum, span, 0)
            span *= 2
        own = ext[POOL_HALO:]
        count = jnp.minimum(t + 1, w).astype(_F32)
        pooled = wsum[POOL_HALO:] / count - own
        y = _dot(pooled.astype(_BF16), pw_ref[g]) * ps_ref[:, cols]
        ybuf[:, cols] = y.astype(_BF16)
    mix = mix + _dot(ybuf[...], wo_ref[ATTN_WIDTH:, :])
    out_ref[0] = x_ref[0] + mix


def _mix_out(x1, attn, u, pool_w, pool_scale, w_out):
    b, s, d = x1.shape
    tm = MIX_TOKEN_TILE
    assert s % tm == 0 and tm % POOL_HALO == 0
    assert all(w <= POOL_HALO and w & (w - 1) == 0 for w in POOL_WINDOWS)
    halo_per_tile = tm // POOL_HALO
    const2 = lambda bi, i: (0, 0)
    return pl.pallas_call(
        _mix_out_kernel,
        grid=(b, s // tm),
        in_specs=[
            pl.BlockSpec((1, tm, d), lambda bi, i: (bi, i, 0)),
            pl.BlockSpec((1, tm, ATTN_WIDTH), lambda bi, i: (bi, i, 0)),
            pl.BlockSpec((1, tm, POOL_WIDTH), lambda bi, i: (bi, i, 0)),
            pl.BlockSpec((1, POOL_HALO, POOL_WIDTH),
                         lambda bi, i: (bi, jnp.maximum(i * halo_per_tile - 1, 0), 0)),
            _resident(pool_w.shape, lambda bi, i: (0, 0, 0)),
            pl.BlockSpec((1, POOL_WIDTH), const2),
            _resident(w_out.shape, const2),
        ],
        out_specs=pl.BlockSpec((1, tm, d), lambda bi, i: (bi, i, 0)),
        out_shape=jax.ShapeDtypeStruct((b, s, d), _F32),
        scratch_shapes=[pltpu.VMEM((POOL_HALO + tm, POOL_WIDTH), _F32), pltpu.VMEM((tm, POOL_WIDTH), _BF16)],
        compiler_params=pltpu.CompilerParams(
            dimension_semantics=("parallel", "parallel"),
            vmem_limit_bytes=V7X_VMEM_LIMIT_BYTES),
        name="mix_out",
    )(x1, attn, u, u, pool_w, pool_scale, w_out)


def kernel(x, positions, norm_ffn1, w1_gate, w1_up, w1_down, norm_mix, w_in, pool_w, pool_scale,
           w_out, norm_ffn2, w2_gate, w2_up, w2_down, norm_final):
    b, s, d = x.shape
    depth = w_in.shape[0]
    inv_freq = ROPE_THETA ** (-jnp.arange(0, ROT_DIM, 2, dtype=_F32) / ROT_DIM)
    freq = inv_freq[:, None]
    fgain = norm_final[None, :]

    for l in range(depth):
        x1, (w_in_b,) = _ffn(x.reshape(b * s, d), norm_ffn1[l][None, :], w1_gate[l], w1_up[l], w1_down[l],
                             fgain, [w_in[l]], final_norm=False, name="ffn_pre")
        x1 = x1.reshape(b, s, d)
        wq, _, wv, _ = jnp.split(w_in_b, [ATTN_WIDTH, 2 * ATTN_WIDTH, 3 * ATTN_WIDTH], axis=-1)
        wqvt = jnp.concatenate([wq, wv], axis=1).T
        qt, k, vt, u, kmean = _in_proj(x1, norm_mix[l][None, :], positions, freq, wqvt, w_in_b)
        riders = [w2_gate[l], w2_up[l], w2_down[l], w_out[l], pool_w[l].reshape(-1, POOL_GROUP_DIM)]
        attn, (wg2, wu2, wd2, wo, pw) = _moba(qt, k, vt, kmean.reshape(b, s // MOBA_BLOCK, ATTN_WIDTH), riders)
        x2 = _mix_out(x1, attn, u, pw.reshape(pool_w.shape[1:]), pool_scale[l][None, :], wo)
        last = l == depth - 1
        x = _ffn_bf16(x2.reshape(b * s, d), norm_ffn2[l][None, :], wg2, wu2, wd2,
                      fgain, final_norm=last, name="ffn_post").reshape(b, s, d)
    return x
```

```python
import functools

import jax
import jax.numpy as jnp
from jax import lax
from jax.experimental import pallas as pl
from jax.experimental.pallas import tpu as pltpu

N_ATTN_HEADS = 8
HEAD_DIM = 128
ATTN_WIDTH = N_ATTN_HEADS * HEAD_DIM
POOL_WINDOWS = (2, 4, 8, 16)
POOL_GROUP_DIM = 256
POOL_WIDTH = POOL_GROUP_DIM * len(POOL_WINDOWS)
MOBA_BLOCK = 256
MOBA_TOPK = 3
ROT_DIM = HEAD_DIM // 4
ROT_HALF = ROT_DIM // 2
ROPE_THETA = 500000.0
EPS = 1e-6
NEG_INF = -1e30
LOG2_E = 1.4426950408889634
QUERY_SCALE = (HEAD_DIM ** -0.5) * LOG2_E
V7X_BF16_ROWS_PER_VREG = 16
V_ROWS = HEAD_DIM + V7X_BF16_ROWS_PER_VREG
POOL_HALO = 16

V7X_VMEM_LIMIT_BYTES = 58 * 1024 * 1024

FFN_TOKEN_TILE = 1024
FFN_FF_TILE = 512
FFN_FIRST_FF_TILE = 256
PROJ_TOKEN_TILE = 512
MIX_TOKEN_TILE = 512
MIX_ROW_BLOCK = 256
MIX_COL_BLOCK = 256
MOBA_HEADS_PER_STEP = 8

_BF16 = jnp.bfloat16
_F32 = jnp.float32


def _rms(x, gain):
    inv = lax.rsqrt(jnp.mean(x * x, axis=-1, keepdims=True) + EPS)
    return x * inv * gain


def _dot(a, b):
    return jnp.dot(a, b, preferred_element_type=_F32)


def _dot_nt(a, b):
    return lax.dot_general(a, b, (((1,), (1,)), ((), ())), preferred_element_type=_F32)


def _resident(block_shape, index_map):
    return pl.BlockSpec(block_shape, index_map, pipeline_mode=pl.Buffered(1))


def _cast_riders(rider_in, rider_out):
    for src_ref, dst_ref in zip(rider_in, rider_out):
        dst_ref[...] = src_ref[...].astype(_BF16)


def _rider_specs(riders, n_steps, step_of):
    specs = []
    for w in riders:
        n_slices = next(n for n in range(n_steps, 0, -1)
                        if w.shape[0] % n == 0 and (w.shape[0] // n) % V7X_BF16_ROWS_PER_VREG == 0)
        specs.append(pl.BlockSpec(
            (w.shape[0] // n_slices, w.shape[1]),
            lambda *ids, n_slices=n_slices: (jnp.minimum(step_of(*ids), n_slices - 1), 0)))
    return specs


def _ffn_step(x_ref, gain_ref, load_weights, fgain_ref, out_ref, h_scr, final_norm):
    f = pl.program_id(1)

    @pl.when(f == 0)
    def _():
        x = x_ref[...]
        h_scr[...] = _rms(x, gain_ref[...]).astype(_BF16)
        out_ref[...] = x

    wg, wu, wd = load_weights()
    h = h_scr[...]
    g = _dot(h, wg)
    u = _dot(h, wu)
    a = (g * jax.nn.sigmoid(g) * (0.5 * u)).astype(_BF16)
    out_ref[...] += _dot(a, wd)

    if final_norm:
        @pl.when(f == pl.num_programs(1) - 1)
        def _():
            out_ref[...] = _rms(out_ref[...], fgain_ref[...])


def _ffn_first_tile_kernel(x_ref, gain_ref, wg32_ref, wu32_ref, wd32_ref, fgain_ref,
                           out_ref, wg_ref, wu_ref, wd_ref, h_scr, *, final_norm):
    def load_weights():
        wg = wg32_ref[...].astype(_BF16)
        wu = wu32_ref[...].astype(_BF16)
        wd = wd32_ref[...].astype(_BF16)
        wg_ref[...] = wg
        wu_ref[...] = wu
        wd_ref[...] = wd
        return wg, wu, wd

    _ffn_step(x_ref, gain_ref, load_weights, fgain_ref, out_ref, h_scr, final_norm)


def _ffn_rest_kernel(x_ref, gain_ref, wg_ref, wu_ref, wd_ref, fgain_ref, first_hbm, *rest, final_norm, n_riders):
    rider_in, rest = rest[:n_riders], rest[n_riders:]
    out_ref, rider_out, (h_scr, copy_sem) = rest[0], rest[1:1 + n_riders], rest[1 + n_riders:]
    _cast_riders(rider_in, rider_out)
    i = pl.program_id(0)

    @pl.when((i == 0) & (pl.program_id(1) == 0))
    def _():
        copy = pltpu.make_async_copy(first_hbm, out_ref, copy_sem)
        copy.start()
        copy.wait()

    @pl.when(i > 0)
    def _():
        load_weights = lambda: (wg_ref[...], wu_ref[...], wd_ref[...])
        _ffn_step(x_ref, gain_ref, load_weights, fgain_ref, out_ref, h_scr, final_norm)


def _ffn(x2d, gain, wg32, wu32, wd32, fgain, riders, *, final_norm, name):
    t, d = x2d.shape
    d_ff = wg32.shape[1]
    tm, tf, tf0 = FFN_TOKEN_TILE, FFN_FF_TILE, FFN_FIRST_FF_TILE
    assert t % tm == 0 and d_ff % tf == 0 and d_ff % tf0 == 0
    params = pltpu.CompilerParams(dimension_semantics=("parallel", "arbitrary"),
                                  vmem_limit_bytes=V7X_VMEM_LIMIT_BYTES)
    row = lambda i, f: (0, 0)
    first, wg, wu, wd = pl.pallas_call(
        functools.partial(_ffn_first_tile_kernel, final_norm=final_norm),
        grid=(1, d_ff // tf0),
        in_specs=[
            _resident((tm, d), lambda i, f: (0, 0)),
            pl.BlockSpec((1, d), row),
            pl.BlockSpec((d, tf0), lambda i, f: (0, f)),
            pl.BlockSpec((d, tf0), lambda i, f: (0, f)),
            pl.BlockSpec((tf0, d), lambda i, f: (f, 0)),
            pl.BlockSpec((1, d), row),
        ],
        out_specs=[
            pl.BlockSpec((tm, d), lambda i, f: (0, 0)),
            pl.BlockSpec((d, tf0), lambda i, f: (0, f)),
            pl.BlockSpec((d, tf0), lambda i, f: (0, f)),
            pl.BlockSpec((tf0, d), lambda i, f: (f, 0)),
        ],
        out_shape=[
            jax.ShapeDtypeStruct((tm, d), _F32),
            jax.ShapeDtypeStruct(wg32.shape, _BF16),
            jax.ShapeDtypeStruct(wu32.shape, _BF16),
            jax.ShapeDtypeStruct(wd32.shape, _BF16),
        ],
        scratch_shapes=[pltpu.VMEM((tm, d), _BF16)],
        compiler_params=params,
        name=name + "_first",
    )(x2d, gain, wg32, wu32, wd32, fgain)
    n_f = d_ff // tf
    rider_specs = _rider_specs(riders, (t // tm) * n_f, lambda i, f: i * n_f + f)
    outs = pl.pallas_call(
        functools.partial(_ffn_rest_kernel, final_norm=final_norm, n_riders=len(riders)),
        grid=(t // tm, n_f),
        in_specs=[
            pl.BlockSpec((tm, d), lambda i, f: (jnp.maximum(i, 1), 0)),
            pl.BlockSpec((1, d), row),
            pl.BlockSpec((d, tf), lambda i, f: (0, jnp.where(i == 0, 0, f))),
            pl.BlockSpec((d, tf), lambda i, f: (0, jnp.where(i == 0, 0, f))),
            pl.BlockSpec((tf, d), lambda i, f: (jnp.where(i == 0, 0, f), 0)),
            pl.BlockSpec((1, d), row),
            pl.BlockSpec(memory_space=pl.ANY),
        ] + rider_specs,
        out_specs=[pl.BlockSpec((tm, d), lambda i, f: (i, 0))] + rider_specs,
        out_shape=[jax.ShapeDtypeStruct((t, d), _F32)] + [jax.ShapeDtypeStruct(w.shape, _BF16) for w in riders],
        scratch_shapes=[pltpu.VMEM((tm, d), _BF16), pltpu.SemaphoreType.DMA(())],
        compiler_params=params,
        name=name + "_rest",
    )(x2d, gain, wg, wu, wd, fgain, first, *riders)
    return outs[0], outs[1:]


def _ffn_bf16_kernel(x_ref, gain_ref, wg_ref, wu_ref, wd_ref, fgain_ref, out_ref, h_scr, *, final_norm):
    load_weights = lambda: (wg_ref[...], wu_ref[...], wd_ref[...])
    _ffn_step(x_ref, gain_ref, load_weights, fgain_ref, out_ref, h_scr, final_norm)


def _ffn_bf16(x2d, gain, wg, wu, wd, fgain, *, final_norm, name):
    t, d = x2d.shape
    d_ff = wg.shape[1]
    tm, tf = FFN_TOKEN_TILE, FFN_FF_TILE
    assert t % tm == 0 and d_ff % tf == 0
    row = lambda i, f: (0, 0)
    return pl.pallas_call(
        functools.partial(_ffn_bf16_kernel, final_norm=final_norm),
        grid=(t // tm, d_ff // tf),
        in_specs=[
            pl.BlockSpec((tm, d), lambda i, f: (i, 0)),
            pl.BlockSpec((1, d), row),
            pl.BlockSpec((d, tf), lambda i, f: (0, f)),
            pl.BlockSpec((d, tf), lambda i, f: (0, f)),
            pl.BlockSpec((tf, d), lambda i, f: (f, 0)),
            pl.BlockSpec((1, d), row),
        ],
        out_specs=pl.BlockSpec((tm, d), lambda i, f: (i, 0)),
        out_shape=jax.ShapeDtypeStruct((t, d), _F32),
        scratch_shapes=[pltpu.VMEM((tm, d), _BF16)],
        compiler_params=pltpu.CompilerParams(dimension_semantics=("parallel", "arbitrary"),
                                             vmem_limit_bytes=V7X_VMEM_LIMIT_BYTES),
        name=name,
    )(x2d, gain, wg, wu, wd, fgain)


def _in_proj_kernel(x_ref, gain_ref, pos_ref, freq_ref, wqvt_ref, wk_ref, wu_ref,
                    qt_ref, k_ref, vt_ref, u_ref, kmean_ref):
    tm = x_ref.shape[1]
    h = _rms(x_ref[0], gain_ref[...]).astype(_BF16)

    k = _dot(h, wk_ref[...])
    u = _dot(h, wu_ref[...])
    qvt = _dot_nt(wqvt_ref[...], h)

    ang_t = freq_ref[...] * pos_ref[0].astype(_F32)
    cos_t, sin_t = jnp.cos(ang_t), jnp.sin(ang_t)
    pad = HEAD_DIM - ROT_DIM
    cos_n = jnp.transpose(jnp.concatenate([cos_t, cos_t, jnp.ones((pad, tm), _F32)], axis=0))
    sin_n = jnp.transpose(jnp.concatenate([-sin_t, sin_t, jnp.zeros((pad, tm), _F32)], axis=0))
    lane = lax.broadcasted_iota(jnp.int32, (1, HEAD_DIM), 1)

    u_ref[0] = u
    for hh in range(N_ATTN_HEADS):
        kh = k[:, hh * HEAD_DIM:(hh + 1) * HEAD_DIM]
        partner = jnp.where(lane < ROT_HALF,
                            pltpu.roll(kh, HEAD_DIM - ROT_HALF, 1),
                            pltpu.roll(kh, ROT_HALF, 1))
        kr = kh * cos_n + partner * sin_n
        k_ref[0, hh] = kr.astype(_BF16)
        for j in range(tm // MOBA_BLOCK):
            kmean_ref[0, j, :, hh * HEAD_DIM:(hh + 1) * HEAD_DIM] = jnp.mean(
                kr[j * MOBA_BLOCK:(j + 1) * MOBA_BLOCK], axis=0, keepdims=True)

    cos_q, sin_q = cos_t * QUERY_SCALE, sin_t * QUERY_SCALE
    for hh in range(N_ATTN_HEADS):
        base = hh * HEAD_DIM
        x1 = qvt[base:base + ROT_HALF]
        x2 = qvt[base + ROT_HALF:base + ROT_DIM]
        qt_ref[0, hh, 0:ROT_HALF, :] = (x1 * cos_q - x2 * sin_q).astype(_BF16)
        qt_ref[0, hh, ROT_HALF:ROT_DIM, :] = (x2 * cos_q + x1 * sin_q).astype(_BF16)
        qt_ref[0, hh, ROT_DIM:, :] = (qvt[base + ROT_DIM:base + HEAD_DIM] * QUERY_SCALE).astype(_BF16)

    ones = jnp.ones((V_ROWS - HEAD_DIM, tm), _BF16)
    for hh in range(N_ATTN_HEADS):
        base = ATTN_WIDTH + hh * HEAD_DIM
        vt_ref[0, hh, 0:HEAD_DIM, :] = qvt[base:base + HEAD_DIM].astype(_BF16)
        vt_ref[0, hh, HEAD_DIM:, :] = ones


def _in_proj(x1, gain, positions, freq, wqvt, w_in):
    b, s, d = x1.shape
    tm = PROJ_TOKEN_TILE
    assert s % tm == 0 and tm % MOBA_BLOCK == 0
    nb = s // MOBA_BLOCK
    bpt = tm // MOBA_BLOCK
    pos_row = positions.reshape(b, 1, s)
    const2 = lambda bi, i: (0, 0)
    return pl.pallas_call(
        _in_proj_kernel,
        grid=(b, s // tm),
        in_specs=[
            pl.BlockSpec((1, tm, d), lambda bi, i: (bi, i, 0)),
            pl.BlockSpec((1, d), const2),
            pl.BlockSpec((1, 1, tm), lambda bi, i: (bi, 0, i)),
            pl.BlockSpec((ROT_HALF, 1), const2),
            _resident((2 * ATTN_WIDTH, d), const2),
            _resident((d, ATTN_WIDTH), lambda bi, i: (0, 1)),
            _resident((d, POOL_WIDTH), lambda bi, i: (0, 3)),
        ],
        out_specs=[
            pl.BlockSpec((1, N_ATTN_HEADS, HEAD_DIM, tm), lambda bi, i: (bi, 0, 0, i)),
            pl.BlockSpec((1, N_ATTN_HEADS, tm, HEAD_DIM), lambda bi, i: (bi, 0, i, 0)),
            pl.BlockSpec((1, N_ATTN_HEADS, V_ROWS, tm), lambda bi, i: (bi, 0, 0, i)),
            pl.BlockSpec((1, tm, POOL_WIDTH), lambda bi, i: (bi, i, 0)),
            pl.BlockSpec((1, bpt, 1, ATTN_WIDTH), lambda bi, i: (bi, i, 0, 0)),
        ],
        out_shape=[
            jax.ShapeDtypeStruct((b, N_ATTN_HEADS, HEAD_DIM, s), _BF16),
            jax.ShapeDtypeStruct((b, N_ATTN_HEADS, s, HEAD_DIM), _BF16),
            jax.ShapeDtypeStruct((b, N_ATTN_HEADS, V_ROWS, s), _BF16),
            jax.ShapeDtypeStruct((b, s, POOL_WIDTH), _F32),
            jax.ShapeDtypeStruct((b, nb, 1, ATTN_WIDTH), _F32),
        ],
        compiler_params=pltpu.CompilerParams(
            dimension_semantics=("parallel", "parallel"),
            vmem_limit_bytes=V7X_VMEM_LIMIT_BYTES),
        name="in_proj",
    )(x1, gain, pos_row, freq, wqvt, w_in, w_in)


def _moba_kernel(qt_ref, k_ref, vt_ref, kmean_ref, *rest, nb, hg, n_riders):
    rider_in, rest = rest[:n_riders], rest[n_riders:]
    out_ref, rider_out, rest = rest[0], rest[1:1 + n_riders], rest[1 + n_riders:]
    bias_scr, acc_scr, m_scr, s_even, s_odd, top_even, top_odd = rest
    _cast_riders(rider_in, rider_out)

    i = pl.program_id(2)
    even, odd = (s_even, top_even), (s_odd, top_odd)
    blk = MOBA_BLOCK
    kb_id = lax.broadcasted_iota(jnp.int32, (nb, blk), 0)
    past = kb_id < i
    kb_f = kb_id.astype(_F32)

    def select_blocks(h):
        km = kmean_ref[0, :, h * HEAD_DIM:(h + 1) * HEAD_DIM]
        km_hi = km.astype(_BF16)
        km_lo = (km - km_hi.astype(_F32)).astype(_BF16)
        gate = _dot(km_hi, qt_ref[0, h]) + _dot(km_lo, qt_ref[0, h])
        gate = jnp.where(past, gate, NEG_INF)
        bias = jnp.full((nb, blk), NEG_INF, _F32)
        for _ in range(MOBA_TOPK):
            top = jnp.max(gate, axis=0, keepdims=True)
            first = jnp.min(jnp.where(gate == top, kb_f, float(nb)), axis=0, keepdims=True)
            hit = kb_f == first
            bias = jnp.where(hit, 0.0, bias)
            gate = jnp.where(hit, -jnp.inf, gate)
        bias_scr[h] = jnp.where(past, bias, NEG_INF)

    def scores(h, n):
        kblk = k_ref[0, h, pl.ds(pl.multiple_of(n * blk, blk), blk), :]
        return _dot(kblk, qt_ref[0, h])

    def stage(h, n, staged):
        s_buf, top_buf = staged
        s = scores(h, n)
        s_buf[h] = s
        top_buf[h] = jnp.max(s, axis=0, keepdims=True)

    def weighted_values(h, n, p):
        vblk = vt_ref[0, h, :, pl.ds(pl.multiple_of(n * blk, blk), blk)]
        return _dot(vblk, p)

    key_pos = lax.broadcasted_iota(jnp.int32, (blk, blk), 0)
    qry_pos = lax.broadcasted_iota(jnp.int32, (blk, blk), 1)
    causal = key_pos <= qry_pos
    ps = []
    ss = [jnp.where(causal, scores(h, i), NEG_INF) for h in range(hg)]
    for h in range(hg):
        select_blocks(h)
    for h in range(hg):
        m_own = jnp.max(ss[h], axis=0, keepdims=True)
        m_scr[h] = m_own
        ps.append(jnp.exp2(ss[h] - m_own).astype(_BF16))
    for h in range(hg):
        acc_scr[h] = weighted_values(h, i, ps[h])
        stage(h, 0, even)

    def step(n, cur, nxt, stage_next):
        n_nxt = jnp.minimum(n + 1, nb - 1)
        for h in range(hg):
            if stage_next:
                stage(h, n_nxt, nxt)
            s = cur[0][h]
            b_n = bias_scr[h, pl.ds(n, 1), :]
            m_run = m_scr[h]
            m_new = jnp.maximum(m_run, cur[1][h] + b_n)
            m_scr[h] = m_new
            alpha = jnp.exp2(m_run - m_new)
            p = jnp.exp2(s - (m_new - b_n)).astype(_BF16)
            acc_scr[h] = alpha * acc_scr[h] + weighted_values(h, n, p)

    def run(n0, count, stage_last=True):
        for c in range(count):
            bufs = (even, odd) if c % 2 == 0 else (odd, even)
            step(n0 + c, *bufs, stage_next=stage_last or c + 1 < count)

    def body(j, carry):
        run(4 * j, 4)
        return carry

    lax.fori_loop(0, i // 4, body, 0)

    @pl.when(i % 4 >= 2)
    def _():
        run((i // 4) * 4, 2)

    @pl.when(i % 2 == 1)
    def _():
        run(i - 1, 1, stage_last=False)

    for h in range(hg):
        acc = acc_scr[h]
        out_ref[0, :, h * HEAD_DIM:(h + 1) * HEAD_DIM] = jnp.transpose(
            acc[0:HEAD_DIM] / acc[HEAD_DIM:HEAD_DIM + 1]).astype(out_ref.dtype)


def _moba(qt, k, vt, kmean, riders):
    b, nh, hd, s = qt.shape
    nb = s // MOBA_BLOCK
    hg = MOBA_HEADS_PER_STEP
    assert nh % hg == 0
    groups = nh // hg
    n_steps = b * groups * nb
    rider_specs = _rider_specs(riders, n_steps, lambda bi, g, i: (bi * groups + g) * nb + i)
    outs = pl.pallas_call(
        functools.partial(_moba_kernel, nb=nb, hg=hg, n_riders=len(riders)),
        grid=(b, groups, nb),
        in_specs=[
            pl.BlockSpec((1, hg, hd, MOBA_BLOCK), lambda bi, g, i: (bi, g, 0, i)),
            pl.BlockSpec((1, hg, s, hd), lambda bi, g, i: (bi, g, 0, 0)),
            pl.BlockSpec((1, hg, V_ROWS, s), lambda bi, g, i: (bi, g, 0, 0)),
            pl.BlockSpec((1, nb, hg * hd), lambda bi, g, i: (bi, 0, g)),
        ] + rider_specs,
        out_specs=[pl.BlockSpec((1, MOBA_BLOCK, hg * hd), lambda bi, g, i: (bi, i, g))] + rider_specs,
        out_shape=[jax.ShapeDtypeStruct((b, s, nh * hd), _BF16)]
                  + [jax.ShapeDtypeStruct(w.shape, _BF16) for w in riders],
        scratch_shapes=[pltpu.VMEM((hg, nb, MOBA_BLOCK), _F32),
                        pltpu.VMEM((hg, V_ROWS, MOBA_BLOCK), _F32),
                        pltpu.VMEM((hg, 1, MOBA_BLOCK), _F32),
                        pltpu.VMEM((hg, MOBA_BLOCK, MOBA_BLOCK), _F32),
                        pltpu.VMEM((hg, MOBA_BLOCK, MOBA_BLOCK), _F32),
                        pltpu.VMEM((hg, 1, MOBA_BLOCK), _F32),
                        pltpu.VMEM((hg, 1, MOBA_BLOCK), _F32)],
        compiler_params=pltpu.CompilerParams(
            dimension_semantics=("parallel", "parallel", "arbitrary"),
            vmem_limit_bytes=V7X_VMEM_LIMIT_BYTES),
        name="moba",
    )(qt, k, vt, kmean, *riders)
    return outs[0], outs[1:]


def _mix_out_kernel(x_ref, attn_ref, u_ref, halo_ref, pw_ref, ps_ref, wo_ref, out_ref, ubuf, ybuf):
    tm = x_ref.shape[1]
    si = pl.program_id(1)
    ubuf[0:POOL_HALO, :] = jnp.where(si > 0, halo_ref[0], 0.0)
    ubuf[POOL_HALO:, :] = u_ref[0]

    ybuf[:, 0:ATTN_WIDTH] = attn_ref[0]
    n_blocks = tm // MIX_ROW_BLOCK
    n_groups = len(POOL_WINDOWS)

    def pool_group(r, g):
        w = POOL_WINDOWS[g]
        row0 = r * MIX_ROW_BLOCK
        cols = slice(g * POOL_GROUP_DIM, (g + 1) * POOL_GROUP_DIM)
        t = si * tm + row0 + lax.broadcasted_iota(jnp.int32, (MIX_ROW_BLOCK, 1), 0)
        ext = ubuf[row0:row0 + POOL_HALO + MIX_ROW_BLOCK, cols]
        wsum = ext
        span = 1
        while span < w:
            wsum = wsum + pltpu.roll(wsum, span, 0)
            span *= 2
        own = ext[POOL_HALO:]
        count = jnp.minimum(t + 1, w).astype(_F32)
        pooled = wsum[POOL_HALO:] / count - own
        y = _dot(pooled.astype(_BF16), pw_ref[g]) * ps_ref[:, cols]
        ybuf[row0:row0 + MIX_ROW_BLOCK, ATTN_WIDTH + g * POOL_GROUP_DIM:ATTN_WIDTH + (g + 1) * POOL_GROUP_DIM] = (
            y.astype(_BF16))

    for g in range(n_groups):
        pool_group(0, g)
    n_col_blocks = out_ref.shape[2] // MIX_COL_BLOCK
    for r in range(n_blocks):
        rows = slice(r * MIX_ROW_BLOCK, (r + 1) * MIX_ROW_BLOCK)
        lhs = ybuf[rows, :]
        for c in range(n_col_blocks):
            cols = slice(c * MIX_COL_BLOCK, (c + 1) * MIX_COL_BLOCK)
            out_ref[0, rows, cols] = x_ref[0, rows, cols] + _dot(lhs, wo_ref[:, cols])
            if r + 1 < n_blocks and c < n_groups:
                pool_group(r + 1, c)


def _mix_out(x1, attn, u, pool_w, pool_scale, w_out):
    b, s, d = x1.shape
    tm = MIX_TOKEN_TILE
    assert s % tm == 0 and tm % POOL_HALO == 0
    assert all(w <= POOL_HALO and w & (w - 1) == 0 for w in POOL_WINDOWS)
    halo_per_tile = tm // POOL_HALO
    const2 = lambda bi, i: (0, 0)
    return pl.pallas_call(
        _mix_out_kernel,
        grid=(b, s // tm),
        in_specs=[
            pl.BlockSpec((1, tm, d), lambda bi, i: (bi, i, 0)),
            pl.BlockSpec((1, tm, ATTN_WIDTH), lambda bi, i: (bi, i, 0)),
            pl.BlockSpec((1, tm, POOL_WIDTH), lambda bi, i: (bi, i, 0)),
            pl.BlockSpec((1, POOL_HALO, POOL_WIDTH),
                         lambda bi, i: (bi, jnp.maximum(i * halo_per_tile - 1, 0), 0)),
            _resident(pool_w.shape, lambda bi, i: (0, 0, 0)),
            pl.BlockSpec((1, POOL_WIDTH), const2),
            _resident(w_out.shape, const2),
        ],
        out_specs=pl.BlockSpec((1, tm, d), lambda bi, i: (bi, i, 0)),
        out_shape=jax.ShapeDtypeStruct((b, s, d), _F32),
        scratch_shapes=[pltpu.VMEM((POOL_HALO + tm, POOL_WIDTH), _F32), pltpu.VMEM((tm, ATTN_WIDTH + POOL_WIDTH), _BF16)],
        compiler_params=pltpu.CompilerParams(
            dimension_semantics=("parallel", "parallel"),
            vmem_limit_bytes=V7X_VMEM_LIMIT_BYTES),
        name="mix_out",
    )(x1, attn, u, u, pool_w, pool_scale, w_out)


def kernel(x, positions, norm_ffn1, w1_gate, w1_up, w1_down, norm_mix, w_in, pool_w, pool_scale,
           w_out, norm_ffn2, w2_gate, w2_up, w2_down, norm_final):
    b, s, d = x.shape
    depth = w_in.shape[0]
    inv_freq = ROPE_THETA ** (-jnp.arange(0, ROT_DIM, 2, dtype=_F32) / ROT_DIM)
    freq = inv_freq[:, None]
    fgain = norm_final[None, :]

    for l in range(depth):
        x1, (w_in_b,) = _ffn(x.reshape(b * s, d), norm_ffn1[l][None, :], w1_gate[l], w1_up[l], w1_down[l],
                             fgain, [w_in[l]], final_norm=False, name="ffn_pre")
        x1 = x1.reshape(b, s, d)
        wq, _, wv, _ = jnp.split(w_in_b, [ATTN_WIDTH, 2 * ATTN_WIDTH, 3 * ATTN_WIDTH], axis=-1)
        wqvt = jnp.concatenate([wq, wv], axis=1).T
        qt, k, vt, u, kmean = _in_proj(x1, norm_mix[l][None, :], positions, freq, wqvt, w_in_b)
        riders = [w2_gate[l], w2_up[l], w2_down[l], w_out[l], pool_w[l].reshape(-1, POOL_GROUP_DIM)]
        attn, (wg2, wu2, wd2, wo, pw) = _moba(qt, k, vt, kmean.reshape(b, s // MOBA_BLOCK, ATTN_WIDTH), riders)
        x2 = _mix_out(x1, attn, u, pw.reshape(pool_w.shape[1:]), pool_scale[l][None, :], wo)
        last = l == depth - 1
        x = _ffn_bf16(x2.reshape(b * s, d), norm_ffn2[l][None, :], wg2, wu2, wd2,
                      fgain, final_norm=last, name="ffn_post").reshape(b, s, d)
    return x
```

```python
import functools

import jax
import jax.numpy as jnp
from jax import lax
from jax.experimental import pallas as pl
from jax.experimental.pallas import tpu as pltpu

N_ATTN_HEADS = 8
HEAD_DIM = 128
ATTN_WIDTH = N_ATTN_HEADS * HEAD_DIM
POOL_WINDOWS = (2, 4, 8, 16)
POOL_GROUP_DIM = 256
POOL_WIDTH = POOL_GROUP_DIM * len(POOL_WINDOWS)
MOBA_BLOCK = 256
MOBA_TOPK = 3
ROT_DIM = HEAD_DIM // 4
ROT_HALF = ROT_DIM // 2
ROPE_THETA = 500000.0
EPS = 1e-6
NEG_INF = -1e30
LOG2_E = 1.4426950408889634
QUERY_SCALE = (HEAD_DIM ** -0.5) * LOG2_E
V7X_BF16_ROWS_PER_VREG = 16
V_ROWS = HEAD_DIM + V7X_BF16_ROWS_PER_VREG
POOL_HALO = 16

V7X_VMEM_LIMIT_BYTES = 58 * 1024 * 1024

FFN_TOKEN_TILE = 1024
FFN_FF_TILE = 512
FFN_FIRST_FF_TILE = 256
PROJ_TOKEN_TILE = 512
MIX_TOKEN_TILE = 512
MIX_ROW_BLOCK = 256
MIX_COL_BLOCK = 256
MOBA_HEADS_PER_STEP = 8

_BF16 = jnp.bfloat16
_F32 = jnp.float32


def _rms(x, gain):
    inv = lax.rsqrt(jnp.mean(x * x, axis=-1, keepdims=True) + EPS)
    return x * inv * gain


def _dot(a, b):
    return jnp.dot(a, b, preferred_element_type=_F32)


def _dot_nt(a, b):
    return lax.dot_general(a, b, (((1,), (1,)), ((), ())), preferred_element_type=_F32)


def _resident(block_shape, index_map):
    return pl.BlockSpec(block_shape, index_map, pipeline_mode=pl.Buffered(1))


def _cast_riders(rider_in, rider_out):
    for src_ref, dst_ref in zip(rider_in, rider_out):
        dst_ref[...] = src_ref[...].astype(_BF16)


def _rider_specs(riders, n_steps, step_of):
    specs = []
    for w in riders:
        n_slices = next(n for n in range(n_steps, 0, -1)
                        if w.shape[0] % n == 0 and (w.shape[0] // n) % V7X_BF16_ROWS_PER_VREG == 0)
        specs.append(pl.BlockSpec(
            (w.shape[0] // n_slices, w.shape[1]),
            lambda *ids, n_slices=n_slices: (jnp.minimum(step_of(*ids), n_slices - 1), 0)))
    return specs


def _ffn_step(x_ref, gain_ref, load_weights, fgain_ref, out_ref, h_scr, final_norm):
    f = pl.program_id(1)

    @pl.when(f == 0)
    def _():
        x = x_ref[...]
        h_scr[...] = _rms(x, gain_ref[...]).astype(_BF16)
        out_ref[...] = x

    wg, wu, wd = load_weights()
    h = h_scr[...]
    g = _dot(h, wg)
    u = _dot(h, wu)
    g_half = 0.5 * g
    a = ((g_half + g_half * jnp.tanh(g_half)) * (0.5 * u)).astype(_BF16)
    out_ref[...] += _dot(a, wd)

    if final_norm:
        @pl.when(f == pl.num_programs(1) - 1)
        def _():
            out_ref[...] = _rms(out_ref[...], fgain_ref[...])


def _ffn_first_tile_kernel(x_ref, gain_ref, wg32_ref, wu32_ref, wd32_ref, fgain_ref,
                           out_ref, wg_ref, wu_ref, wd_ref, h_scr, *, final_norm):
    def load_weights():
        wg = wg32_ref[...].astype(_BF16)
        wu = wu32_ref[...].astype(_BF16)
        wd = wd32_ref[...].astype(_BF16)
        wg_ref[...] = wg
        wu_ref[...] = wu
        wd_ref[...] = wd
        return wg, wu, wd

    _ffn_step(x_ref, gain_ref, load_weights, fgain_ref, out_ref, h_scr, final_norm)


def _ffn_rest_kernel(x_ref, gain_ref, wg_ref, wu_ref, wd_ref, fgain_ref, first_hbm, *rest, final_norm, n_riders):
    rider_in, rest = rest[:n_riders], rest[n_riders:]
    out_ref, rider_out, (h_scr, copy_sem) = rest[0], rest[1:1 + n_riders], rest[1 + n_riders:]
    _cast_riders(rider_in, rider_out)
    i = pl.program_id(0)

    @pl.when((i == 0) & (pl.program_id(1) == 0))
    def _():
        copy = pltpu.make_async_copy(first_hbm, out_ref, copy_sem)
        copy.start()
        copy.wait()

    @pl.when(i > 0)
    def _():
        load_weights = lambda: (wg_ref[...], wu_ref[...], wd_ref[...])
        _ffn_step(x_ref, gain_ref, load_weights, fgain_ref, out_ref, h_scr, final_norm)


def _ffn(x2d, gain, wg32, wu32, wd32, fgain, riders, *, final_norm, name):
    t, d = x2d.shape
    d_ff = wg32.shape[1]
    tm, tf, tf0 = FFN_TOKEN_TILE, FFN_FF_TILE, FFN_FIRST_FF_TILE
    assert t % tm == 0 and d_ff % tf == 0 and d_ff % tf0 == 0
    params = pltpu.CompilerParams(dimension_semantics=("parallel", "arbitrary"),
                                  vmem_limit_bytes=V7X_VMEM_LIMIT_BYTES)
    row = lambda i, f: (0, 0)
    first, wg, wu, wd = pl.pallas_call(
        functools.partial(_ffn_first_tile_kernel, final_norm=final_norm),
        grid=(1, d_ff // tf0),
        in_specs=[
            _resident((tm, d), lambda i, f: (0, 0)),
            pl.BlockSpec((1, d), row),
            pl.BlockSpec((d, tf0), lambda i, f: (0, f)),
            pl.BlockSpec((d, tf0), lambda i, f: (0, f)),
            pl.BlockSpec((tf0, d), lambda i, f: (f, 0)),
            pl.BlockSpec((1, d), row),
        ],
        out_specs=[
            pl.BlockSpec((tm, d), lambda i, f: (0, 0)),
            pl.BlockSpec((d, tf0), lambda i, f: (0, f)),
            pl.BlockSpec((d, tf0), lambda i, f: (0, f)),
            pl.BlockSpec((tf0, d), lambda i, f: (f, 0)),
        ],
        out_shape=[
            jax.ShapeDtypeStruct((tm, d), _F32),
            jax.ShapeDtypeStruct(wg32.shape, _BF16),
            jax.ShapeDtypeStruct(wu32.shape, _BF16),
            jax.ShapeDtypeStruct(wd32.shape, _BF16),
        ],
        scratch_shapes=[pltpu.VMEM((tm, d), _BF16)],
        compiler_params=params,
        name=name + "_first",
    )(x2d, gain, wg32, wu32, wd32, fgain)
    n_f = d_ff // tf
    rider_specs = _rider_specs(riders, (t // tm) * n_f, lambda i, f: i * n_f + f)
    outs = pl.pallas_call(
        functools.partial(_ffn_rest_kernel, final_norm=final_norm, n_riders=len(riders)),
        grid=(t // tm, n_f),
        in_specs=[
            pl.BlockSpec((tm, d), lambda i, f: (jnp.maximum(i, 1), 0)),
            pl.BlockSpec((1, d), row),
            pl.BlockSpec((d, tf), lambda i, f: (0, jnp.where(i == 0, 0, f))),
            pl.BlockSpec((d, tf), lambda i, f: (0, jnp.where(i == 0, 0, f))),
            pl.BlockSpec((tf, d), lambda i, f: (jnp.where(i == 0, 0, f), 0)),
            pl.BlockSpec((1, d), row),
            pl.BlockSpec(memory_space=pl.ANY),
        ] + rider_specs,
        out_specs=[pl.BlockSpec((tm, d), lambda i, f: (i, 0))] + rider_specs,
        out_shape=[jax.ShapeDtypeStruct((t, d), _F32)] + [jax.ShapeDtypeStruct(w.shape, _BF16) for w in riders],
        scratch_shapes=[pltpu.VMEM((tm, d), _BF16), pltpu.SemaphoreType.DMA(())],
        compiler_params=params,
        name=name + "_rest",
    )(x2d, gain, wg, wu, wd, fgain, first, *riders)
    return outs[0], outs[1:]


def _ffn_bf16_kernel(x_ref, gain_ref, wg_ref, wu_ref, wd_ref, fgain_ref, out_ref, h_scr, *, final_norm):
    load_weights = lambda: (wg_ref[...], wu_ref[...], wd_ref[...])
    _ffn_step(x_ref, gain_ref, load_weights, fgain_ref, out_ref, h_scr, final_norm)


def _ffn_bf16(x2d, gain, wg, wu, wd, fgain, *, final_norm, name):
    t, d = x2d.shape
    d_ff = wg.shape[1]
    tm, tf = FFN_TOKEN_TILE, FFN_FF_TILE
    assert t % tm == 0 and d_ff % tf == 0
    row = lambda i, f: (0, 0)
    return pl.pallas_call(
        functools.partial(_ffn_bf16_kernel, final_norm=final_norm),
        grid=(t // tm, d_ff // tf),
        in_specs=[
            pl.BlockSpec((tm, d), lambda i, f: (i, 0)),
            pl.BlockSpec((1, d), row),
            pl.BlockSpec((d, tf), lambda i, f: (0, f)),
            pl.BlockSpec((d, tf), lambda i, f: (0, f)),
            pl.BlockSpec((tf, d), lambda i, f: (f, 0)),
            pl.BlockSpec((1, d), row),
        ],
        out_specs=pl.BlockSpec((tm, d), lambda i, f: (i, 0)),
        out_shape=jax.ShapeDtypeStruct((t, d), _F32),
        scratch_shapes=[pltpu.VMEM((tm, d), _BF16)],
        compiler_params=pltpu.CompilerParams(dimension_semantics=("parallel", "arbitrary"),
                                             vmem_limit_bytes=V7X_VMEM_LIMIT_BYTES),
        name=name,
    )(x2d, gain, wg, wu, wd, fgain)


def _in_proj_kernel(x_ref, gain_ref, pos_ref, freq_ref, wqvt_ref, wk_ref, wu_ref,
                    qt_ref, k_ref, vt_ref, u_ref, kmean_ref):
    tm = x_ref.shape[1]
    h = _rms(x_ref[0], gain_ref[...]).astype(_BF16)

    k = _dot(h, wk_ref[...])
    u = _dot(h, wu_ref[...])
    qvt = _dot_nt(wqvt_ref[...], h)

    ang_t = freq_ref[...] * pos_ref[0].astype(_F32)
    cos_t, sin_t = jnp.cos(ang_t), jnp.sin(ang_t)
    pad = HEAD_DIM - ROT_DIM
    cos_n = jnp.transpose(jnp.concatenate([cos_t, cos_t, jnp.ones((pad, tm), _F32)], axis=0))
    sin_n = jnp.transpose(jnp.concatenate([-sin_t, sin_t, jnp.zeros((pad, tm), _F32)], axis=0))
    lane = lax.broadcasted_iota(jnp.int32, (1, HEAD_DIM), 1)

    u_ref[0] = u
    for hh in range(N_ATTN_HEADS):
        kh = k[:, hh * HEAD_DIM:(hh + 1) * HEAD_DIM]
        partner = jnp.where(lane < ROT_HALF,
                            pltpu.roll(kh, HEAD_DIM - ROT_HALF, 1),
                            pltpu.roll(kh, ROT_HALF, 1))
        kr = kh * cos_n + partner * sin_n
        k_ref[0, hh] = kr.astype(_BF16)
        for j in range(tm // MOBA_BLOCK):
            kmean_ref[0, j, :, hh * HEAD_DIM:(hh + 1) * HEAD_DIM] = jnp.mean(
                kr[j * MOBA_BLOCK:(j + 1) * MOBA_BLOCK], axis=0, keepdims=True)

    cos_q, sin_q = cos_t * QUERY_SCALE, sin_t * QUERY_SCALE
    for hh in range(N_ATTN_HEADS):
        base = hh * HEAD_DIM
        x1 = qvt[base:base + ROT_HALF]
        x2 = qvt[base + ROT_HALF:base + ROT_DIM]
        qt_ref[0, hh, 0:ROT_HALF, :] = (x1 * cos_q - x2 * sin_q).astype(_BF16)
        qt_ref[0, hh, ROT_HALF:ROT_DIM, :] = (x2 * cos_q + x1 * sin_q).astype(_BF16)
        qt_ref[0, hh, ROT_DIM:, :] = (qvt[base + ROT_DIM:base + HEAD_DIM] * QUERY_SCALE).astype(_BF16)

    ones = jnp.ones((V_ROWS - HEAD_DIM, tm), _BF16)
    for hh in range(N_ATTN_HEADS):
        base = ATTN_WIDTH + hh * HEAD_DIM
        vt_ref[0, hh, 0:HEAD_DIM, :] = qvt[base:base + HEAD_DIM].astype(_BF16)
        vt_ref[0, hh, HEAD_DIM:, :] = ones


def _in_proj(x1, gain, positions, freq, wqvt, w_in):
    b, s, d = x1.shape
    tm = PROJ_TOKEN_TILE
    assert s % tm == 0 and tm % MOBA_BLOCK == 0
    nb = s // MOBA_BLOCK
    bpt = tm // MOBA_BLOCK
    pos_row = positions.reshape(b, 1, s)
    const2 = lambda bi, i: (0, 0)
    return pl.pallas_call(
        _in_proj_kernel,
        grid=(b, s // tm),
        in_specs=[
            pl.BlockSpec((1, tm, d), lambda bi, i: (bi, i, 0)),
            pl.BlockSpec((1, d), const2),
            pl.BlockSpec((1, 1, tm), lambda bi, i: (bi, 0, i)),
            pl.BlockSpec((ROT_HALF, 1), const2),
            _resident((2 * ATTN_WIDTH, d), const2),
            _resident((d, ATTN_WIDTH), lambda bi, i: (0, 1)),
            _resident((d, POOL_WIDTH), lambda bi, i: (0, 3)),
        ],
        out_specs=[
            pl.BlockSpec((1, N_ATTN_HEADS, HEAD_DIM, tm), lambda bi, i: (bi, 0, 0, i)),
            pl.BlockSpec((1, N_ATTN_HEADS, tm, HEAD_DIM), lambda bi, i: (bi, 0, i, 0)),
            pl.BlockSpec((1, N_ATTN_HEADS, V_ROWS, tm), lambda bi, i: (bi, 0, 0, i)),
            pl.BlockSpec((1, tm, POOL_WIDTH), lambda bi, i: (bi, i, 0)),
            pl.BlockSpec((1, bpt, 1, ATTN_WIDTH), lambda bi, i: (bi, i, 0, 0)),
        ],
        out_shape=[
            jax.ShapeDtypeStruct((b, N_ATTN_HEADS, HEAD_DIM, s), _BF16),
            jax.ShapeDtypeStruct((b, N_ATTN_HEADS, s, HEAD_DIM), _BF16),
            jax.ShapeDtypeStruct((b, N_ATTN_HEADS, V_ROWS, s), _BF16),
            jax.ShapeDtypeStruct((b, s, POOL_WIDTH), _F32),
            jax.ShapeDtypeStruct((b, nb, 1, ATTN_WIDTH), _F32),
        ],
        compiler_params=pltpu.CompilerParams(
            dimension_semantics=("parallel", "parallel"),
            vmem_limit_bytes=V7X_VMEM_LIMIT_BYTES),
        name="in_proj",
    )(x1, gain, pos_row, freq, wqvt, w_in, w_in)


def _moba_kernel(qt_ref, k_ref, vt_ref, kmean_ref, *rest, nb, hg, n_riders):
    rider_in, rest = rest[:n_riders], rest[n_riders:]
    out_ref, rider_out, rest = rest[0], rest[1:1 + n_riders], rest[1 + n_riders:]
    bias_scr, acc_scr, m_scr, s_even, s_odd, top_even, top_odd = rest
    _cast_riders(rider_in, rider_out)

    i = pl.program_id(2)
    even, odd = (s_even, top_even), (s_odd, top_odd)
    blk = MOBA_BLOCK
    kb_id = lax.broadcasted_iota(jnp.int32, (nb, blk), 0)
    past = kb_id < i
    kb_f = kb_id.astype(_F32)

    def select_blocks(h):
        km = kmean_ref[0, :, h * HEAD_DIM:(h + 1) * HEAD_DIM]
        km_hi = km.astype(_BF16)
        km_lo = (km - km_hi.astype(_F32)).astype(_BF16)
        gate = _dot(km_hi, qt_ref[0, h]) + _dot(km_lo, qt_ref[0, h])
        gate = jnp.where(past, gate, NEG_INF)
        bias = jnp.full((nb, blk), NEG_INF, _F32)
        for _ in range(MOBA_TOPK):
            top = jnp.max(gate, axis=0, keepdims=True)
            first = jnp.min(jnp.where(gate == top, kb_f, float(nb)), axis=0, keepdims=True)
            hit = kb_f == first
            bias = jnp.where(hit, 0.0, bias)
            gate = jnp.where(hit, -jnp.inf, gate)
        bias_scr[h] = jnp.where(past, bias, NEG_INF)

    def scores(h, n):
        kblk = k_ref[0, h, pl.ds(pl.multiple_of(n * blk, blk), blk), :]
        return _dot(kblk, qt_ref[0, h])

    def stage(h, n, staged):
        s_buf, top_buf = staged
        s = scores(h, n)
        s_buf[h] = s
        top_buf[h] = jnp.max(s, axis=0, keepdims=True)

    def weighted_values(h, n, p):
        vblk = vt_ref[0, h, :, pl.ds(pl.multiple_of(n * blk, blk), blk)]
        return _dot(vblk, p)

    key_pos = lax.broadcasted_iota(jnp.int32, (blk, blk), 0)
    qry_pos = lax.broadcasted_iota(jnp.int32, (blk, blk), 1)
    causal = key_pos <= qry_pos
    ps = []
    ss = [jnp.where(causal, scores(h, i), NEG_INF) for h in range(hg)]
    for h in range(hg):
        select_blocks(h)
    for h in range(hg):
        m_own = jnp.max(ss[h], axis=0, keepdims=True)
        m_scr[h] = m_own
        ps.append(jnp.exp2(ss[h] - m_own).astype(_BF16))
    for h in range(hg):
        acc_scr[h] = weighted_values(h, i, ps[h])
        stage(h, 0, even)

    def step(n, cur, nxt, stage_next):
        n_nxt = jnp.minimum(n + 1, nb - 1)
        for h in range(hg):
            if stage_next:
                stage(h, n_nxt, nxt)
            s = cur[0][h]
            b_n = bias_scr[h, pl.ds(n, 1), :]
            m_run = m_scr[h]
            m_new = jnp.maximum(m_run, cur[1][h] + b_n)
            m_scr[h] = m_new
            alpha = jnp.exp2(m_run - m_new)
            p = jnp.exp2(s - (m_new - b_n)).astype(_BF16)
            acc_scr[h] = alpha * acc_scr[h] + weighted_values(h, n, p)

    def run(n0, count, stage_last=True):
        for c in range(count):
            bufs = (even, odd) if c % 2 == 0 else (odd, even)
            step(n0 + c, *bufs, stage_next=stage_last or c + 1 < count)

    def body(j, carry):
        run(4 * j, 4)
        return carry

    lax.fori_loop(0, i // 4, body, 0)

    @pl.when(i % 4 >= 2)
    def _():
        run((i // 4) * 4, 2)

    @pl.when(i % 2 == 1)
    def _():
        run(i - 1, 1, stage_last=False)

    for h in range(hg):
        acc = acc_scr[h]
        out_ref[0, :, h * HEAD_DIM:(h + 1) * HEAD_DIM] = jnp.transpose(
            acc[0:HEAD_DIM] / acc[HEAD_DIM:HEAD_DIM + 1]).astype(out_ref.dtype)


def _moba(qt, k, vt, kmean, riders):
    b, nh, hd, s = qt.shape
    nb = s // MOBA_BLOCK
    hg = MOBA_HEADS_PER_STEP
    assert nh % hg == 0
    groups = nh // hg
    n_steps = b * groups * nb
    rider_specs = _rider_specs(riders, n_steps, lambda bi, g, i: (bi * groups + g) * nb + i)
    outs = pl.pallas_call(
        functools.partial(_moba_kernel, nb=nb, hg=hg, n_riders=len(riders)),
        grid=(b, groups, nb),
        in_specs=[
            pl.BlockSpec((1, hg, hd, MOBA_BLOCK), lambda bi, g, i: (bi, g, 0, i)),
            pl.BlockSpec((1, hg, s, hd), lambda bi, g, i: (bi, g, 0, 0)),
            pl.BlockSpec((1, hg, V_ROWS, s), lambda bi, g, i: (bi, g, 0, 0)),
            pl.BlockSpec((1, nb, hg * hd), lambda bi, g, i: (bi, 0, g)),
        ] + rider_specs,
        out_specs=[pl.BlockSpec((1, MOBA_BLOCK, hg * hd), lambda bi, g, i: (bi, i, g))] + rider_specs,
        out_shape=[jax.ShapeDtypeStruct((b, s, nh * hd), _BF16)]
                  + [jax.ShapeDtypeStruct(w.shape, _BF16) for w in riders],
        scratch_shapes=[pltpu.VMEM((hg, nb, MOBA_BLOCK), _F32),
                        pltpu.VMEM((hg, V_ROWS, MOBA_BLOCK), _F32),
                        pltpu.VMEM((hg, 1, MOBA_BLOCK), _F32),
                        pltpu.VMEM((hg, MOBA_BLOCK, MOBA_BLOCK), _F32),
                        pltpu.VMEM((hg, MOBA_BLOCK, MOBA_BLOCK), _F32),
                        pltpu.VMEM((hg, 1, MOBA_BLOCK), _F32),
                        pltpu.VMEM((hg, 1, MOBA_BLOCK), _F32)],
        compiler_params=pltpu.CompilerParams(
            dimension_semantics=("parallel", "parallel", "arbitrary"),
            vmem_limit_bytes=V7X_VMEM_LIMIT_BYTES),
        name="moba",
    )(qt, k, vt, kmean, *riders)
    return outs[0], outs[1:]


def _mix_out_kernel(x_ref, attn_ref, u_ref, halo_ref, pw_ref, ps_ref, wo_ref, out_ref, ubuf, ybuf):
    tm = x_ref.shape[1]
    si = pl.program_id(1)
    ubuf[0:POOL_HALO, :] = jnp.where(si > 0, halo_ref[0], 0.0)
    ubuf[POOL_HALO:, :] = u_ref[0]

    ybuf[:, 0:ATTN_WIDTH] = attn_ref[0]
    n_blocks = tm // MIX_ROW_BLOCK
    n_groups = len(POOL_WINDOWS)

    def pool_group(r, g):
        w = POOL_WINDOWS[g]
        row0 = r * MIX_ROW_BLOCK
        cols = slice(g * POOL_GROUP_DIM, (g + 1) * POOL_GROUP_DIM)
        t = si * tm + row0 + lax.broadcasted_iota(jnp.int32, (MIX_ROW_BLOCK, 1), 0)
        ext = ubuf[row0:row0 + POOL_HALO + MIX_ROW_BLOCK, cols]
        wsum = ext
        span = 1
        while span < w:
            wsum = wsum + pltpu.roll(wsum, span, 0)
            span *= 2
        own = ext[POOL_HALO:]
        count = jnp.minimum(t + 1, w).astype(_F32)
        pooled = wsum[POOL_HALO:] / count - own
        y = _dot(pooled.astype(_BF16), pw_ref[g]) * ps_ref[:, cols]
        ybuf[row0:row0 + MIX_ROW_BLOCK, ATTN_WIDTH + g * POOL_GROUP_DIM:ATTN_WIDTH + (g + 1) * POOL_GROUP_DIM] = (
            y.astype(_BF16))

    for g in range(n_groups):
        pool_group(0, g)
    n_col_blocks = out_ref.shape[2] // MIX_COL_BLOCK
    for r in range(n_blocks):
        rows = slice(r * MIX_ROW_BLOCK, (r + 1) * MIX_ROW_BLOCK)
        lhs = ybuf[rows, :]
        for c in range(n_col_blocks):
            cols = slice(c * MIX_COL_BLOCK, (c + 1) * MIX_COL_BLOCK)
            out_ref[0, rows, cols] = x_ref[0, rows, cols] + _dot(lhs, wo_ref[:, cols])
            if r + 1 < n_blocks and c < n_groups:
                pool_group(r + 1, c)


def _mix_out(x1, attn, u, pool_w, pool_scale, w_out):
    b, s, d = x1.shape
    tm = MIX_TOKEN_TILE
    assert s % tm == 0 and tm % POOL_HALO == 0
    assert all(w <= POOL_HALO and w & (w - 1) == 0 for w in POOL_WINDOWS)
    halo_per_tile = tm // POOL_HALO
    const2 = lambda bi, i: (0, 0)
    return pl.pallas_call(
        _mix_out_kernel,
        grid=(b, s // tm),
        in_specs=[
            pl.BlockSpec((1, tm, d), lambda bi, i: (bi, i, 0)),
            pl.BlockSpec((1, tm, ATTN_WIDTH), lambda bi, i: (bi, i, 0)),
            pl.BlockSpec((1, tm, POOL_WIDTH), lambda bi, i: (bi, i, 0)),
            pl.BlockSpec((1, POOL_HALO, POOL_WIDTH),
                         lambda bi, i: (bi, jnp.maximum(i * halo_per_tile - 1, 0), 0)),
            _resident(pool_w.shape, lambda bi, i: (0, 0, 0)),
            pl.BlockSpec((1, POOL_WIDTH), const2),
            _resident(w_out.shape, const2),
        ],
        out_specs=pl.BlockSpec((1, tm, d), lambda bi, i: (bi, i, 0)),
        out_shape=jax.ShapeDtypeStruct((b, s, d), _F32),
        scratch_shapes=[pltpu.VMEM((POOL_HALO + tm, POOL_WIDTH), _F32), pltpu.VMEM((tm, ATTN_WIDTH + POOL_WIDTH), _BF16)],
        compiler_params=pltpu.CompilerParams(
            dimension_semantics=("parallel", "parallel"),
            vmem_limit_bytes=V7X_VMEM_LIMIT_BYTES),
        name="mix_out",
    )(x1, attn, u, u, pool_w, pool_scale, w_out)


def kernel(x, positions, norm_ffn1, w1_gate, w1_up, w1_down, norm_mix, w_in, pool_w, pool_scale,
           w_out, norm_ffn2, w2_gate, w2_up, w2_down, norm_final):
    b, s, d = x.shape
    depth = w_in.shape[0]
    inv_freq = ROPE_THETA ** (-jnp.arange(0, ROT_DIM, 2, dtype=_F32) / ROT_DIM)
    freq = inv_freq[:, None]
    fgain = norm_final[None, :]

    for l in range(depth):
        x1, (w_in_b,) = _ffn(x.reshape(b * s, d), norm_ffn1[l][None, :], w1_gate[l], w1_up[l], w1_down[l],
                             fgain, [w_in[l]], final_norm=False, name="ffn_pre")
        x1 = x1.reshape(b, s, d)
        wq, _, wv, _ = jnp.split(w_in_b, [ATTN_WIDTH, 2 * ATTN_WIDTH, 3 * ATTN_WIDTH], axis=-1)
        wqvt = jnp.concatenate([wq, wv], axis=1).T
        qt, k, vt, u, kmean = _in_proj(x1, norm_mix[l][None, :], positions, freq, wqvt, w_in_b)
        riders = [w2_gate[l], w2_up[l], w2_down[l], w_out[l], pool_w[l].reshape(-1, POOL_GROUP_DIM)]
        attn, (wg2, wu2, wd2, wo, pw) = _moba(qt, k, vt, kmean.reshape(b, s // MOBA_BLOCK, ATTN_WIDTH), riders)
        x2 = _mix_out(x1, attn, u, pw.reshape(pool_w.shape[1:]), pool_scale[l][None, :], wo)
        last = l == depth - 1
        x = _ffn_bf16(x2.reshape(b * s, d), norm_ffn2[l][None, :], wg2, wu2, wd2,
                      fgain, final_norm=last, name="ffn_post").reshape(b, s, d)
    return x
```

```python
import functools

import jax
import jax.numpy as jnp
from jax import lax
from jax.experimental import pallas as pl
from jax.experimental.pallas import tpu as pltpu

N_ATTN_HEADS = 8
HEAD_DIM = 128
ATTN_WIDTH = N_ATTN_HEADS * HEAD_DIM
POOL_WINDOWS = (2, 4, 8, 16)
POOL_GROUP_DIM = 256
POOL_WIDTH = POOL_GROUP_DIM * len(POOL_WINDOWS)
MOBA_BLOCK = 256
MOBA_TOPK = 3
ROT_DIM = HEAD_DIM // 4
ROT_HALF = ROT_DIM // 2
ROPE_THETA = 500000.0
EPS = 1e-6
NEG_INF = -1e30
LOG2_E = 1.4426950408889634
QUERY_SCALE = (HEAD_DIM ** -0.5) * LOG2_E
V7X_BF16_ROWS_PER_VREG = 16
V_ROWS = HEAD_DIM + V7X_BF16_ROWS_PER_VREG
POOL_HALO = 16

V7X_VMEM_LIMIT_BYTES = 58 * 1024 * 1024

FFN_TOKEN_TILE = 1024
FFN_FF_TILE = 512
FFN_FIRST_FF_TILE = 256
PROJ_TOKEN_TILE = 512
MIX_TOKEN_TILE = 512
MIX_ROW_BLOCK = 256
MIX_COL_BLOCK = 256
MOBA_HEADS_PER_STEP = 8

_BF16 = jnp.bfloat16
_F32 = jnp.float32


def _rms(x, gain):
    inv = lax.rsqrt(jnp.mean(x * x, axis=-1, keepdims=True) + EPS)
    return x * inv * gain


def _dot(a, b):
    return jnp.dot(a, b, preferred_element_type=_F32)


def _dot_nt(a, b):
    return lax.dot_general(a, b, (((1,), (1,)), ((), ())), preferred_element_type=_F32)


def _resident(block_shape, index_map):
    return pl.BlockSpec(block_shape, index_map, pipeline_mode=pl.Buffered(1))


def _cast_riders(rider_in, rider_out):
    for src_ref, dst_ref in zip(rider_in, rider_out):
        dst_ref[...] = src_ref[...].astype(_BF16)


def _rider_specs(riders, n_steps, step_of):
    specs = []
    for w in riders:
        n_slices = next(n for n in range(n_steps, 0, -1)
                        if w.shape[0] % n == 0 and (w.shape[0] // n) % V7X_BF16_ROWS_PER_VREG == 0)
        specs.append(pl.BlockSpec(
            (w.shape[0] // n_slices, w.shape[1]),
            lambda *ids, n_slices=n_slices: (jnp.minimum(step_of(*ids), n_slices - 1), 0)))
    return specs


def _ffn_step(x_ref, gain_ref, load_weights, fgain_ref, out_ref, h_scr, final_norm):
    f = pl.program_id(1)

    @pl.when(f == 0)
    def _():
        x = x_ref[...]
        h_scr[...] = _rms(x, gain_ref[...]).astype(_BF16)
        out_ref[...] = x

    wg, wu, wd = load_weights()
    h = h_scr[...]
    g = _dot(h, wg)
    u = _dot(h, wu)
    g_half = 0.5 * g
    a = ((g_half + g_half * jnp.tanh(g_half)) * (0.5 * u)).astype(_BF16)
    out_ref[...] += _dot(a, wd)

    if final_norm:
        @pl.when(f == pl.num_programs(1) - 1)
        def _():
            out_ref[...] = _rms(out_ref[...], fgain_ref[...])


def _ffn_first_tile_kernel(x_ref, gain_ref, wg32_ref, wu32_ref, wd32_ref, fgain_ref,
                           out_ref, wg_ref, wu_ref, wd_ref, h_scr, *, final_norm):
    def load_weights():
        wg = wg32_ref[...].astype(_BF16)
        wu = wu32_ref[...].astype(_BF16)
        wd = wd32_ref[...].astype(_BF16)
        wg_ref[...] = wg
        wu_ref[...] = wu
        wd_ref[...] = wd
        return wg, wu, wd

    _ffn_step(x_ref, gain_ref, load_weights, fgain_ref, out_ref, h_scr, final_norm)


def _ffn_rest_kernel(x_ref, gain_ref, wg_ref, wu_ref, wd_ref, fgain_ref, first_hbm, *rest, final_norm, n_riders):
    rider_in, rest = rest[:n_riders], rest[n_riders:]
    out_ref, rider_out, (h_scr, copy_sem) = rest[0], rest[1:1 + n_riders], rest[1 + n_riders:]
    _cast_riders(rider_in, rider_out)
    i = pl.program_id(0)

    @pl.when((i == 0) & (pl.program_id(1) == 0))
    def _():
        copy = pltpu.make_async_copy(first_hbm, out_ref, copy_sem)
        copy.start()
        copy.wait()

    @pl.when(i > 0)
    def _():
        load_weights = lambda: (wg_ref[...], wu_ref[...], wd_ref[...])
        _ffn_step(x_ref, gain_ref, load_weights, fgain_ref, out_ref, h_scr, final_norm)


def _ffn(x2d, gain, wg32, wu32, wd32, fgain, riders, *, final_norm, name):
    t, d = x2d.shape
    d_ff = wg32.shape[1]
    tm, tf, tf0 = FFN_TOKEN_TILE, FFN_FF_TILE, FFN_FIRST_FF_TILE
    assert t % tm == 0 and d_ff % tf == 0 and d_ff % tf0 == 0
    params = pltpu.CompilerParams(dimension_semantics=("parallel", "arbitrary"),
                                  vmem_limit_bytes=V7X_VMEM_LIMIT_BYTES)
    row = lambda i, f: (0, 0)
    first, wg, wu, wd = pl.pallas_call(
        functools.partial(_ffn_first_tile_kernel, final_norm=final_norm),
        grid=(1, d_ff // tf0),
        in_specs=[
            _resident((tm, d), lambda i, f: (0, 0)),
            pl.BlockSpec((1, d), row),
            pl.BlockSpec((d, tf0), lambda i, f: (0, f)),
            pl.BlockSpec((d, tf0), lambda i, f: (0, f)),
            pl.BlockSpec((tf0, d), lambda i, f: (f, 0)),
            pl.BlockSpec((1, d), row),
        ],
        out_specs=[
            pl.BlockSpec((tm, d), lambda i, f: (0, 0)),
            pl.BlockSpec((d, tf0), lambda i, f: (0, f)),
            pl.BlockSpec((d, tf0), lambda i, f: (0, f)),
            pl.BlockSpec((tf0, d), lambda i, f: (f, 0)),
        ],
        out_shape=[
            jax.ShapeDtypeStruct((tm, d), _F32),
            jax.ShapeDtypeStruct(wg32.shape, _BF16),
            jax.ShapeDtypeStruct(wu32.shape, _BF16),
            jax.ShapeDtypeStruct(wd32.shape, _BF16),
        ],
        scratch_shapes=[pltpu.VMEM((tm, d), _BF16)],
        compiler_params=params,
        name=name + "_first",
    )(x2d, gain, wg32, wu32, wd32, fgain)
    n_f = d_ff // tf
    rider_specs = _rider_specs(riders, (t // tm) * n_f, lambda i, f: i * n_f + f)
    outs = pl.pallas_call(
        functools.partial(_ffn_rest_kernel, final_norm=final_norm, n_riders=len(riders)),
        grid=(t // tm, n_f),
        in_specs=[
            pl.BlockSpec((tm, d), lambda i, f: (jnp.maximum(i, 1), 0)),
            pl.BlockSpec((1, d), row),
            pl.BlockSpec((d, tf), lambda i, f: (0, jnp.where(i == 0, 0, f))),
            pl.BlockSpec((d, tf), lambda i, f: (0, jnp.where(i == 0, 0, f))),
            pl.BlockSpec((tf, d), lambda i, f: (jnp.where(i == 0, 0, f), 0)),
            pl.BlockSpec((1, d), row),
            pl.BlockSpec(memory_space=pl.ANY),
        ] + rider_specs,
        out_specs=[pl.BlockSpec((tm, d), lambda i, f: (i, 0))] + rider_specs,
        out_shape=[jax.ShapeDtypeStruct((t, d), _F32)] + [jax.ShapeDtypeStruct(w.shape, _BF16) for w in riders],
        scratch_shapes=[pltpu.VMEM((tm, d), _BF16), pltpu.SemaphoreType.DMA(())],
        compiler_params=params,
        name=name + "_rest",
    )(x2d, gain, wg, wu, wd, fgain, first, *riders)
    return outs[0], outs[1:]


def _ffn_bf16_kernel(x_ref, gain_ref, wg_ref, wu_ref, wd_ref, fgain_ref, out_ref, h_scr, *, final_norm):
    load_weights = lambda: (wg_ref[...], wu_ref[...], wd_ref[...])
    _ffn_step(x_ref, gain_ref, load_weights, fgain_ref, out_ref, h_scr, final_norm)


def _ffn_bf16(x2d, gain, wg, wu, wd, fgain, *, final_norm, name):
    t, d = x2d.shape
    d_ff = wg.shape[1]
    tm, tf = FFN_TOKEN_TILE, FFN_FF_TILE
    assert t % tm == 0 and d_ff % tf == 0
    row = lambda i, f: (0, 0)
    return pl.pallas_call(
        functools.partial(_ffn_bf16_kernel, final_norm=final_norm),
        grid=(t // tm, d_ff // tf),
        in_specs=[
            pl.BlockSpec((tm, d), lambda i, f: (i, 0)),
            pl.BlockSpec((1, d), row),
            pl.BlockSpec((d, tf), lambda i, f: (0, f)),
            pl.BlockSpec((d, tf), lambda i, f: (0, f)),
            pl.BlockSpec((tf, d), lambda i, f: (f, 0)),
            pl.BlockSpec((1, d), row),
        ],
        out_specs=pl.BlockSpec((tm, d), lambda i, f: (i, 0)),
        out_shape=jax.ShapeDtypeStruct((t, d), _F32),
        scratch_shapes=[pltpu.VMEM((tm, d), _BF16)],
        compiler_params=pltpu.CompilerParams(dimension_semantics=("parallel", "arbitrary"),
                                             vmem_limit_bytes=V7X_VMEM_LIMIT_BYTES),
        name=name,
    )(x2d, gain, wg, wu, wd, fgain)


def _in_proj_kernel(x_ref, gain_ref, pos_ref, freq_ref, wqvt_ref, wk_ref, wu_ref,
                    qt_ref, k_ref, vt_ref, u_ref, kmean_ref):
    tm = x_ref.shape[1]
    h = _rms(x_ref[0], gain_ref[...]).astype(_BF16)

    k = _dot(h, wk_ref[...])
    u = _dot(h, wu_ref[...])
    qvt = _dot_nt(wqvt_ref[...], h)

    ang_t = freq_ref[...] * pos_ref[0].astype(_F32)
    cos_t, sin_t = jnp.cos(ang_t), jnp.sin(ang_t)
    pad = HEAD_DIM - ROT_DIM
    cos_n = jnp.transpose(jnp.concatenate([cos_t, cos_t, jnp.ones((pad, tm), _F32)], axis=0))
    sin_n = jnp.transpose(jnp.concatenate([-sin_t, sin_t, jnp.zeros((pad, tm), _F32)], axis=0))
    lane = lax.broadcasted_iota(jnp.int32, (1, HEAD_DIM), 1)

    u_ref[0] = u
    for hh in range(N_ATTN_HEADS):
        kh = k[:, hh * HEAD_DIM:(hh + 1) * HEAD_DIM]
        partner = jnp.where(lane < ROT_HALF,
                            pltpu.roll(kh, HEAD_DIM - ROT_HALF, 1),
                            pltpu.roll(kh, ROT_HALF, 1))
        kr = kh * cos_n + partner * sin_n
        k_ref[0, hh] = kr.astype(_BF16)
        for j in range(tm // MOBA_BLOCK):
            kmean_ref[0, j, :, hh * HEAD_DIM:(hh + 1) * HEAD_DIM] = jnp.mean(
                kr[j * MOBA_BLOCK:(j + 1) * MOBA_BLOCK], axis=0, keepdims=True)

    cos_q, sin_q = cos_t * QUERY_SCALE, sin_t * QUERY_SCALE
    for hh in range(N_ATTN_HEADS):
        base = hh * HEAD_DIM
        x1 = qvt[base:base + ROT_HALF]
        x2 = qvt[base + ROT_HALF:base + ROT_DIM]
        qt_ref[0, hh, 0:ROT_HALF, :] = (x1 * cos_q - x2 * sin_q).astype(_BF16)
        qt_ref[0, hh, ROT_HALF:ROT_DIM, :] = (x2 * cos_q + x1 * sin_q).astype(_BF16)
        qt_ref[0, hh, ROT_DIM:, :] = (qvt[base + ROT_DIM:base + HEAD_DIM] * QUERY_SCALE).astype(_BF16)

    ones = jnp.ones((V_ROWS - HEAD_DIM, tm), _BF16)
    for hh in range(N_ATTN_HEADS):
        base = ATTN_WIDTH + hh * HEAD_DIM
        vt_ref[0, hh, 0:HEAD_DIM, :] = qvt[base:base + HEAD_DIM].astype(_BF16)
        vt_ref[0, hh, HEAD_DIM:, :] = ones


def _in_proj(x1, gain, positions, freq, wqvt, w_in):
    b, s, d = x1.shape
    tm = PROJ_TOKEN_TILE
    assert s % tm == 0 and tm % MOBA_BLOCK == 0
    nb = s // MOBA_BLOCK
    bpt = tm // MOBA_BLOCK
    pos_row = positions.reshape(b, 1, s)
    const2 = lambda bi, i: (0, 0)
    return pl.pallas_call(
        _in_proj_kernel,
        grid=(b, s // tm),
        in_specs=[
            pl.BlockSpec((1, tm, d), lambda bi, i: (bi, i, 0)),
            pl.BlockSpec((1, d), const2),
            pl.BlockSpec((1, 1, tm), lambda bi, i: (bi, 0, i)),
            pl.BlockSpec((ROT_HALF, 1), const2),
            _resident((2 * ATTN_WIDTH, d), const2),
            _resident((d, ATTN_WIDTH), lambda bi, i: (0, 1)),
            _resident((d, POOL_WIDTH), lambda bi, i: (0, 3)),
        ],
        out_specs=[
            pl.BlockSpec((1, N_ATTN_HEADS, HEAD_DIM, tm), lambda bi, i: (bi, 0, 0, i)),
            pl.BlockSpec((1, N_ATTN_HEADS, tm, HEAD_DIM), lambda bi, i: (bi, 0, i, 0)),
            pl.BlockSpec((1, N_ATTN_HEADS, V_ROWS, tm), lambda bi, i: (bi, 0, 0, i)),
            pl.BlockSpec((1, tm, POOL_WIDTH), lambda bi, i: (bi, i, 0)),
            pl.BlockSpec((1, bpt, 1, ATTN_WIDTH), lambda bi, i: (bi, i, 0, 0)),
        ],
        out_shape=[
            jax.ShapeDtypeStruct((b, N_ATTN_HEADS, HEAD_DIM, s), _BF16),
            jax.ShapeDtypeStruct((b, N_ATTN_HEADS, s, HEAD_DIM), _BF16),
            jax.ShapeDtypeStruct((b, N_ATTN_HEADS, V_ROWS, s), _BF16),
            jax.ShapeDtypeStruct((b, s, POOL_WIDTH), _F32),
            jax.ShapeDtypeStruct((b, nb, 1, ATTN_WIDTH), _F32),
        ],
        compiler_params=pltpu.CompilerParams(
            dimension_semantics=("parallel", "parallel"),
            vmem_limit_bytes=V7X_VMEM_LIMIT_BYTES),
        name="in_proj",
    )(x1, gain, pos_row, freq, wqvt, w_in, w_in)


def _moba_kernel(qt_ref, k_ref, vt_ref, kmean_ref, *rest, nb, hg, n_riders):
    rider_in, rest = rest[:n_riders], rest[n_riders:]
    out_ref, rider_out, rest = rest[0], rest[1:1 + n_riders], rest[1 + n_riders:]
    bias_scr, acc_scr, m_scr, s_even, s_odd, top_even, top_odd = rest

    i = pl.program_id(2)
    even, odd = (s_even, top_even), (s_odd, top_odd)
    blk = MOBA_BLOCK
    kb_id = lax.broadcasted_iota(jnp.int32, (nb, blk), 0)
    past = kb_id < i
    kb_f = kb_id.astype(_F32)

    def gate_scores(h):
        km = kmean_ref[0, :, h * HEAD_DIM:(h + 1) * HEAD_DIM]
        km_hi = km.astype(_BF16)
        km_lo = (km - km_hi.astype(_F32)).astype(_BF16)
        return _dot(km_hi, qt_ref[0, h]) + _dot(km_lo, qt_ref[0, h])

    def select_blocks(h, gate):
        gate = jnp.where(past, gate, NEG_INF)
        bias = jnp.full((nb, blk), NEG_INF, _F32)
        for _ in range(MOBA_TOPK):
            top = jnp.max(gate, axis=0, keepdims=True)
            first = jnp.min(jnp.where(gate == top, kb_f, float(nb)), axis=0, keepdims=True)
            hit = kb_f == first
            bias = jnp.where(hit, 0.0, bias)
            gate = jnp.where(hit, -jnp.inf, gate)
        bias_scr[h] = jnp.where(past, bias, NEG_INF)

    def scores(h, n):
        kblk = k_ref[0, h, pl.ds(pl.multiple_of(n * blk, blk), blk), :]
        return _dot(kblk, qt_ref[0, h])

    def stage(h, n, staged):
        s_buf, top_buf = staged
        s = scores(h, n)
        s_buf[h] = s
        top_buf[h] = jnp.max(s, axis=0, keepdims=True)

    def weighted_values(h, n, p):
        vblk = vt_ref[0, h, :, pl.ds(pl.multiple_of(n * blk, blk), blk)]
        return _dot(vblk, p)

    key_pos = lax.broadcasted_iota(jnp.int32, (blk, blk), 0)
    qry_pos = lax.broadcasted_iota(jnp.int32, (blk, blk), 1)
    causal = key_pos <= qry_pos
    gates = [gate_scores(h) for h in range(hg)]
    ss = [jnp.where(causal, scores(h, i), NEG_INF) for h in range(hg)]
    for h in range(hg):
        select_blocks(h, gates[h])
    for h in range(hg):
        m_own = jnp.max(ss[h], axis=0, keepdims=True)
        m_scr[h] = m_own
        acc_scr[h] = weighted_values(h, i, jnp.exp2(ss[h] - m_own).astype(_BF16))
        stage(h, 0, even)

    def step(n, cur, nxt, stage_next):
        n_nxt = jnp.minimum(n + 1, nb - 1)
        for h in range(hg):
            if stage_next:
                stage(h, n_nxt, nxt)
            s = cur[0][h]
            b_n = bias_scr[h, pl.ds(n, 1), :]
            m_run = m_scr[h]
            m_new = jnp.maximum(m_run, cur[1][h] + b_n)
            m_scr[h] = m_new
            alpha = jnp.exp2(m_run - m_new)
            p = jnp.exp2(s - (m_new - b_n)).astype(_BF16)
            acc_scr[h] = alpha * acc_scr[h] + weighted_values(h, n, p)

    def run(n0, count, stage_last=True):
        for c in range(count):
            bufs = (even, odd) if c % 2 == 0 else (odd, even)
            step(n0 + c, *bufs, stage_next=stage_last or c + 1 < count)

    def body(j, carry):
        run(8 * j, 8)
        return carry

    lax.fori_loop(0, i // 8, body, 0)

    @pl.when(i % 8 >= 4)
    def _():
        run((i // 8) * 8, 4)

    @pl.when(i % 4 >= 2)
    def _():
        run((i // 4) * 4, 2)

    @pl.when(i % 2 == 1)
    def _():
        run(i - 1, 1, stage_last=False)

    _cast_riders(rider_in, rider_out)
    for h in range(hg):
        acc = acc_scr[h]
        out_ref[0, :, h * HEAD_DIM:(h + 1) * HEAD_DIM] = jnp.transpose(
            acc[0:HEAD_DIM] / acc[HEAD_DIM:HEAD_DIM + 1]).astype(out_ref.dtype)


def _moba(qt, k, vt, kmean, riders):
    b, nh, hd, s = qt.shape
    nb = s // MOBA_BLOCK
    hg = MOBA_HEADS_PER_STEP
    assert nh % hg == 0
    groups = nh // hg
    n_steps = b * groups * nb
    rider_specs = _rider_specs(riders, n_steps, lambda bi, g, i: (bi * groups + g) * nb + i)
    outs = pl.pallas_call(
        functools.partial(_moba_kernel, nb=nb, hg=hg, n_riders=len(riders)),
        grid=(b, groups, nb),
        in_specs=[
            pl.BlockSpec((1, hg, hd, MOBA_BLOCK), lambda bi, g, i: (bi, g, 0, i)),
            pl.BlockSpec((1, hg, s, hd), lambda bi, g, i: (bi, g, 0, 0)),
            pl.BlockSpec((1, hg, V_ROWS, s), lambda bi, g, i: (bi, g, 0, 0)),
            pl.BlockSpec((1, nb, hg * hd), lambda bi, g, i: (bi, 0, g)),
        ] + rider_specs,
        out_specs=[pl.BlockSpec((1, MOBA_BLOCK, hg * hd), lambda bi, g, i: (bi, i, g))] + rider_specs,
        out_shape=[jax.ShapeDtypeStruct((b, s, nh * hd), _BF16)]
                  + [jax.ShapeDtypeStruct(w.shape, _BF16) for w in riders],
        scratch_shapes=[pltpu.VMEM((hg, nb, MOBA_BLOCK), _F32),
                        pltpu.VMEM((hg, V_ROWS, MOBA_BLOCK), _F32),
                        pltpu.VMEM((hg, 1, MOBA_BLOCK), _F32),
                        pltpu.VMEM((hg, MOBA_BLOCK, MOBA_BLOCK), _F32),
                        pltpu.VMEM((hg, MOBA_BLOCK, MOBA_BLOCK), _F32),
                        pltpu.VMEM((hg, 1, MOBA_BLOCK), _F32),
                        pltpu.VMEM((hg, 1, MOBA_BLOCK), _F32)],
        compiler_params=pltpu.CompilerParams(
            dimension_semantics=("parallel", "parallel", "arbitrary"),
            vmem_limit_bytes=V7X_VMEM_LIMIT_BYTES),
        name="moba",
    )(qt, k, vt, kmean, *riders)
    return outs[0], outs[1:]


def _mix_out_kernel(x_ref, attn_ref, u_ref, halo_ref, pw_ref, ps_ref, wo_ref, out_ref, ubuf, lhs_buf):
    tm = x_ref.shape[1]
    si = pl.program_id(1)
    ubuf[0:POOL_HALO, :] = jnp.where(si > 0, halo_ref[0], 0.0)
    ubuf[POOL_HALO:, :] = u_ref[0]

    lhs_buf[:, 0:ATTN_WIDTH] = attn_ref[0]
    n_blocks = tm // MIX_ROW_BLOCK
    n_groups = len(POOL_WINDOWS)

    def pool_group(r, g):
        w = POOL_WINDOWS[g]
        row0 = r * MIX_ROW_BLOCK
        cols = slice(g * POOL_GROUP_DIM, (g + 1) * POOL_GROUP_DIM)
        t = si * tm + row0 + lax.broadcasted_iota(jnp.int32, (MIX_ROW_BLOCK, 1), 0)
        ext = ubuf[row0:row0 + POOL_HALO + MIX_ROW_BLOCK, cols]
        wsum = ext
        span = 1
        while span < w:
            wsum = wsum + pltpu.roll(wsum, span, 0)
            span *= 2
        own = ext[POOL_HALO:]
        count = jnp.minimum(t + 1, w).astype(_F32)
        pooled = wsum[POOL_HALO:] / count - own
        y = _dot(pooled.astype(_BF16), pw_ref[g]) * ps_ref[:, cols]
        lhs_buf[row0:row0 + MIX_ROW_BLOCK, ATTN_WIDTH + g * POOL_GROUP_DIM:ATTN_WIDTH + (g + 1) * POOL_GROUP_DIM] = (
            y.astype(_BF16))

    for g in range(n_groups):
        pool_group(0, g)
    n_col_blocks = out_ref.shape[2] // MIX_COL_BLOCK
    for r in range(n_blocks):
        rows = slice(r * MIX_ROW_BLOCK, (r + 1) * MIX_ROW_BLOCK)
        lhs = lhs_buf[rows, :]
        for c in range(n_col_blocks):
            cols = slice(c * MIX_COL_BLOCK, (c + 1) * MIX_COL_BLOCK)
            out_ref[0, rows, cols] = x_ref[0, rows, cols] + _dot(lhs, wo_ref[:, cols])
            if r + 1 < n_blocks and c < n_groups:
                pool_group(r + 1, c)


def _mix_out(x1, attn, u, pool_w, pool_scale, w_out):
    b, s, d = x1.shape
    tm = MIX_TOKEN_TILE
    assert s % tm == 0 and tm % POOL_HALO == 0
    assert all(w <= POOL_HALO and w & (w - 1) == 0 for w in POOL_WINDOWS)
    halo_per_tile = tm // POOL_HALO
    const2 = lambda bi, i: (0, 0)
    return pl.pallas_call(
        _mix_out_kernel,
        grid=(b, s // tm),
        in_specs=[
            pl.BlockSpec((1, tm, d), lambda bi, i: (bi, i, 0)),
            pl.BlockSpec((1, tm, ATTN_WIDTH), lambda bi, i: (bi, i, 0)),
            pl.BlockSpec((1, tm, POOL_WIDTH), lambda bi, i: (bi, i, 0)),
            pl.BlockSpec((1, POOL_HALO, POOL_WIDTH),
                         lambda bi, i: (bi, jnp.maximum(i * halo_per_tile - 1, 0), 0)),
            _resident(pool_w.shape, lambda bi, i: (0, 0, 0)),
            pl.BlockSpec((1, POOL_WIDTH), const2),
            _resident(w_out.shape, const2),
        ],
        out_specs=pl.BlockSpec((1, tm, d), lambda bi, i: (bi, i, 0)),
        out_shape=jax.ShapeDtypeStruct((b, s, d), _F32),
        scratch_shapes=[pltpu.VMEM((POOL_HALO + tm, POOL_WIDTH), _F32), pltpu.VMEM((tm, ATTN_WIDTH + POOL_WIDTH), _BF16)],
        compiler_params=pltpu.CompilerParams(
            dimension_semantics=("parallel", "parallel"),
            vmem_limit_bytes=V7X_VMEM_LIMIT_BYTES),
        name="mix_out",
    )(x1, attn, u, u, pool_w, pool_scale, w_out)


def kernel(x, positions, norm_ffn1, w1_gate, w1_up, w1_down, norm_mix, w_in, pool_w, pool_scale,
           w_out, norm_ffn2, w2_gate, w2_up, w2_down, norm_final):
    b, s, d = x.shape
    depth = w_in.shape[0]
    inv_freq = ROPE_THETA ** (-jnp.arange(0, ROT_DIM, 2, dtype=_F32) / ROT_DIM)
    freq = inv_freq[:, None]
    fgain = norm_final[None, :]

    for l in range(depth):
        x1, (w_in_b,) = _ffn(x.reshape(b * s, d), norm_ffn1[l][None, :], w1_gate[l], w1_up[l], w1_down[l],
                             fgain, [w_in[l]], final_norm=False, name="ffn_pre")
        x1 = x1.reshape(b, s, d)
        wq, _, wv, _ = jnp.split(w_in_b, [ATTN_WIDTH, 2 * ATTN_WIDTH, 3 * ATTN_WIDTH], axis=-1)
        wqvt = jnp.concatenate([wq, wv], axis=1).T
        qt, k, vt, u, kmean = _in_proj(x1, norm_mix[l][None, :], positions, freq, wqvt, w_in_b)
        riders = [w2_gate[l], w2_up[l], w2_down[l], w_out[l], pool_w[l].reshape(-1, POOL_GROUP_DIM)]
        attn, (wg2, wu2, wd2, wo, pw) = _moba(qt, k, vt, kmean.reshape(b, s // MOBA_BLOCK, ATTN_WIDTH), riders)
        x2 = _mix_out(x1, attn, u, pw.reshape(pool_w.shape[1:]), pool_scale[l][None, :], wo)
        last = l == depth - 1
        x = _ffn_bf16(x2.reshape(b * s, d), norm_ffn2[l][None, :], wg2, wu2, wd2,
                      fgain, final_norm=last, name="ffn_post").reshape(b, s, d)
    return x
```

```python
import functools

import jax
import jax.numpy as jnp
from jax import lax
from jax.experimental import pallas as pl
from jax.experimental.pallas import tpu as pltpu

N_ATTN_HEADS = 8
HEAD_DIM = 128
ATTN_WIDTH = N_ATTN_HEADS * HEAD_DIM
POOL_WINDOWS = (2, 4, 8, 16)
POOL_GROUP_DIM = 256
POOL_WIDTH = POOL_GROUP_DIM * len(POOL_WINDOWS)
MOBA_BLOCK = 256
MOBA_TOPK = 3
ROT_DIM = HEAD_DIM // 4
ROT_HALF = ROT_DIM // 2
ROPE_THETA = 500000.0
EPS = 1e-6
NEG_INF = -1e30
LOG2_E = 1.4426950408889634
QUERY_SCALE = (HEAD_DIM ** -0.5) * LOG2_E
V7X_BF16_ROWS_PER_VREG = 16
V_ROWS = HEAD_DIM + V7X_BF16_ROWS_PER_VREG
POOL_HALO = 16

V7X_VMEM_LIMIT_BYTES = 58 * 1024 * 1024

FFN_TOKEN_TILE = 1024
FFN_FF_TILE = 512
FFN_FIRST_FF_TILE = 256
PROJ_TOKEN_TILE = 512
MIX_TOKEN_TILE = 512
MIX_ROW_BLOCK = 256
MIX_COL_BLOCK = 256
MOBA_HEADS_PER_STEP = 8

_BF16 = jnp.bfloat16
_F32 = jnp.float32


def _rms(x, gain):
    inv = lax.rsqrt(jnp.mean(x * x, axis=-1, keepdims=True) + EPS)
    return x * inv * gain


def _dot(a, b):
    return jnp.dot(a, b, preferred_element_type=_F32)


def _dot_nt(a, b):
    return lax.dot_general(a, b, (((1,), (1,)), ((), ())), preferred_element_type=_F32)


def _resident(block_shape, index_map):
    return pl.BlockSpec(block_shape, index_map, pipeline_mode=pl.Buffered(1))


def _cast_riders(rider_in, rider_out):
    for src_ref, dst_ref in zip(rider_in, rider_out):
        dst_ref[...] = src_ref[...].astype(_BF16)


def _rider_specs(riders, n_steps, step_of):
    specs = []
    for w in riders:
        n_slices = next(n for n in range(n_steps, 0, -1)
                        if w.shape[0] % n == 0 and (w.shape[0] // n) % V7X_BF16_ROWS_PER_VREG == 0)
        specs.append(pl.BlockSpec(
            (w.shape[0] // n_slices, w.shape[1]),
            lambda *ids, n_slices=n_slices: (jnp.minimum(step_of(*ids), n_slices - 1), 0)))
    return specs


def _ffn_step(x_ref, gain_ref, load_weights, fgain_ref, out_ref, h_scr, final_norm):
    f = pl.program_id(1)

    @pl.when(f == 0)
    def _():
        x = x_ref[...]
        h_scr[...] = _rms(x, gain_ref[...]).astype(_BF16)
        out_ref[...] = x

    wg, wu, wd = load_weights()
    h = h_scr[...]
    g = _dot(h, wg)
    u = _dot(h, wu)
    g_half = 0.5 * g
    a = ((g_half + g_half * jnp.tanh(g_half)) * (0.5 * u)).astype(_BF16)
    out_ref[...] += _dot(a, wd)

    if final_norm:
        @pl.when(f == pl.num_programs(1) - 1)
        def _():
            out_ref[...] = _rms(out_ref[...], fgain_ref[...])


def _ffn_first_tile_kernel(x_ref, gain_ref, wg32_ref, wu32_ref, wd32_ref, fgain_ref,
                           out_ref, wg_ref, wu_ref, wd_ref, h_scr, *, final_norm):
    def load_weights():
        wg = wg32_ref[...].astype(_BF16)
        wu = wu32_ref[...].astype(_BF16)
        wd = wd32_ref[...].astype(_BF16)
        wg_ref[...] = wg
        wu_ref[...] = wu
        wd_ref[...] = wd
        return wg, wu, wd

    _ffn_step(x_ref, gain_ref, load_weights, fgain_ref, out_ref, h_scr, final_norm)


def _ffn_rest_kernel(x_ref, gain_ref, wg_ref, wu_ref, wd_ref, fgain_ref, first_hbm, *rest, final_norm, n_riders):
    rider_in, rest = rest[:n_riders], rest[n_riders:]
    out_ref, rider_out, (h_scr, copy_sem) = rest[0], rest[1:1 + n_riders], rest[1 + n_riders:]
    _cast_riders(rider_in, rider_out)
    i = pl.program_id(0)

    @pl.when((i == 0) & (pl.program_id(1) == 0))
    def _():
        copy = pltpu.make_async_copy(first_hbm, out_ref, copy_sem)
        copy.start()
        copy.wait()

    @pl.when(i > 0)
    def _():
        load_weights = lambda: (wg_ref[...], wu_ref[...], wd_ref[...])
        _ffn_step(x_ref, gain_ref, load_weights, fgain_ref, out_ref, h_scr, final_norm)


def _ffn(x2d, gain, wg32, wu32, wd32, fgain, riders, *, final_norm, name):
    t, d = x2d.shape
    d_ff = wg32.shape[1]
    tm, tf, tf0 = FFN_TOKEN_TILE, FFN_FF_TILE, FFN_FIRST_FF_TILE
    assert t % tm == 0 and d_ff % tf == 0 and d_ff % tf0 == 0
    params = pltpu.CompilerParams(dimension_semantics=("parallel", "arbitrary"),
                                  vmem_limit_bytes=V7X_VMEM_LIMIT_BYTES)
    row = lambda i, f: (0, 0)
    first, wg, wu, wd = pl.pallas_call(
        functools.partial(_ffn_first_tile_kernel, final_norm=final_norm),
        grid=(1, d_ff // tf0),
        in_specs=[
            _resident((tm, d), lambda i, f: (0, 0)),
            pl.BlockSpec((1, d), row),
            pl.BlockSpec((d, tf0), lambda i, f: (0, f)),
            pl.BlockSpec((d, tf0), lambda i, f: (0, f)),
            pl.BlockSpec((tf0, d), lambda i, f: (f, 0)),
            pl.BlockSpec((1, d), row),
        ],
        out_specs=[
            pl.BlockSpec((tm, d), lambda i, f: (0, 0)),
            pl.BlockSpec((d, tf0), lambda i, f: (0, f)),
            pl.BlockSpec((d, tf0), lambda i, f: (0, f)),
            pl.BlockSpec((tf0, d), lambda i, f: (f, 0)),
        ],
        out_shape=[
            jax.ShapeDtypeStruct((tm, d), _F32),
            jax.ShapeDtypeStruct(wg32.shape, _BF16),
            jax.ShapeDtypeStruct(wu32.shape, _BF16),
            jax.ShapeDtypeStruct(wd32.shape, _BF16),
        ],
        scratch_shapes=[pltpu.VMEM((tm, d), _BF16)],
        compiler_params=params,
        name=name + "_first",
    )(x2d, gain, wg32, wu32, wd32, fgain)
    n_f = d_ff // tf
    rider_specs = _rider_specs(riders, (t // tm) * n_f, lambda i, f: i * n_f + f)
    outs = pl.pallas_call(
        functools.partial(_ffn_rest_kernel, final_norm=final_norm, n_riders=len(riders)),
        grid=(t // tm, n_f),
        in_specs=[
            pl.BlockSpec((tm, d), lambda i, f: (jnp.maximum(i, 1), 0)),
            pl.BlockSpec((1, d), row),
            pl.BlockSpec((d, tf), lambda i, f: (0, jnp.where(i == 0, 0, f))),
            pl.BlockSpec((d, tf), lambda i, f: (0, jnp.where(i == 0, 0, f))),
            pl.BlockSpec((tf, d), lambda i, f: (jnp.where(i == 0, 0, f), 0)),
            pl.BlockSpec((1, d), row),
            pl.BlockSpec(memory_space=pl.ANY),
        ] + rider_specs,
        out_specs=[pl.BlockSpec((tm, d), lambda i, f: (i, 0))] + rider_specs,
        out_shape=[jax.ShapeDtypeStruct((t, d), _F32)] + [jax.ShapeDtypeStruct(w.shape, _BF16) for w in riders],
        scratch_shapes=[pltpu.VMEM((tm, d), _BF16), pltpu.SemaphoreType.DMA(())],
        compiler_params=params,
        name=name + "_rest",
    )(x2d, gain, wg, wu, wd, fgain, first, *riders)
    return outs[0], outs[1:]


def _ffn_bf16_kernel(x_ref, gain_ref, wg_ref, wu_ref, wd_ref, fgain_ref, out_ref, h_scr, *, final_norm):
    load_weights = lambda: (wg_ref[...], wu_ref[...], wd_ref[...])
    _ffn_step(x_ref, gain_ref, load_weights, fgain_ref, out_ref, h_scr, final_norm)


def _ffn_bf16(x2d, gain, wg, wu, wd, fgain, *, final_norm, name):
    t, d = x2d.shape
    d_ff = wg.shape[1]
    tm, tf = FFN_TOKEN_TILE, FFN_FF_TILE
    assert t % tm == 0 and d_ff % tf == 0
    row = lambda i, f: (0, 0)
    return pl.pallas_call(
        functools.partial(_ffn_bf16_kernel, final_norm=final_norm),
        grid=(t // tm, d_ff // tf),
        in_specs=[
            pl.BlockSpec((tm, d), lambda i, f: (i, 0)),
            pl.BlockSpec((1, d), row),
            pl.BlockSpec((d, tf), lambda i, f: (0, f)),
            pl.BlockSpec((d, tf), lambda i, f: (0, f)),
            pl.BlockSpec((tf, d), lambda i, f: (f, 0)),
            pl.BlockSpec((1, d), row),
        ],
        out_specs=pl.BlockSpec((tm, d), lambda i, f: (i, 0)),
        out_shape=jax.ShapeDtypeStruct((t, d), _F32),
        scratch_shapes=[pltpu.VMEM((tm, d), _BF16)],
        compiler_params=pltpu.CompilerParams(dimension_semantics=("parallel", "arbitrary"),
                                             vmem_limit_bytes=V7X_VMEM_LIMIT_BYTES),
        name=name,
    )(x2d, gain, wg, wu, wd, fgain)


def _in_proj_kernel(x_ref, gain_ref, pos_ref, freq_ref, wqt_ref, wvt_ref, wk_ref, wu_ref,
                    qt_ref, k_ref, vt_ref, u_ref, kmean_ref):
    tm = x_ref.shape[1]
    h = _rms(x_ref[0], gain_ref[...]).astype(_BF16)

    k = _dot(h, wk_ref[...])
    u = _dot(h, wu_ref[...])
    qt = _dot_nt(wqt_ref[...], h)
    vt = _dot_nt(wvt_ref[...], h)

    ang_t = freq_ref[...] * pos_ref[0].astype(_F32)
    cos_t, sin_t = jnp.cos(ang_t), jnp.sin(ang_t)
    pad = HEAD_DIM - ROT_DIM
    cos_n = jnp.transpose(jnp.concatenate([cos_t, cos_t, jnp.ones((pad, tm), _F32)], axis=0))
    sin_n = jnp.transpose(jnp.concatenate([-sin_t, sin_t, jnp.zeros((pad, tm), _F32)], axis=0))
    lane = lax.broadcasted_iota(jnp.int32, (1, HEAD_DIM), 1)

    u_ref[0] = u
    for hh in range(N_ATTN_HEADS):
        kh = k[:, hh * HEAD_DIM:(hh + 1) * HEAD_DIM]
        partner = jnp.where(lane < ROT_HALF,
                            pltpu.roll(kh, HEAD_DIM - ROT_HALF, 1),
                            pltpu.roll(kh, ROT_HALF, 1))
        kr = kh * cos_n + partner * sin_n
        k_ref[0, hh] = kr.astype(_BF16)
        for j in range(tm // MOBA_BLOCK):
            kmean_ref[0, j, :, hh * HEAD_DIM:(hh + 1) * HEAD_DIM] = jnp.mean(
                kr[j * MOBA_BLOCK:(j + 1) * MOBA_BLOCK], axis=0, keepdims=True)

    cos_q, sin_q = cos_t * QUERY_SCALE, sin_t * QUERY_SCALE
    for hh in range(N_ATTN_HEADS):
        base = hh * HEAD_DIM
        x1 = qt[base:base + ROT_HALF]
        x2 = qt[base + ROT_HALF:base + ROT_DIM]
        qt_ref[0, hh, 0:ROT_HALF, :] = (x1 * cos_q - x2 * sin_q).astype(_BF16)
        qt_ref[0, hh, ROT_HALF:ROT_DIM, :] = (x2 * cos_q + x1 * sin_q).astype(_BF16)
        qt_ref[0, hh, ROT_DIM:, :] = (qt[base + ROT_DIM:base + HEAD_DIM] * QUERY_SCALE).astype(_BF16)

    ones = jnp.ones((V_ROWS - HEAD_DIM, tm), _BF16)
    for hh in range(N_ATTN_HEADS):
        vt_ref[0, hh, 0:HEAD_DIM, :] = vt[hh * HEAD_DIM:(hh + 1) * HEAD_DIM].astype(_BF16)
        vt_ref[0, hh, HEAD_DIM:, :] = ones


def _in_proj(x1, gain, positions, freq, w_in, w_in_t):
    b, s, d = x1.shape
    tm = PROJ_TOKEN_TILE
    assert s % tm == 0 and tm % MOBA_BLOCK == 0
    nb = s // MOBA_BLOCK
    bpt = tm // MOBA_BLOCK
    pos_row = positions.reshape(b, 1, s)
    const2 = lambda bi, i: (0, 0)
    return pl.pallas_call(
        _in_proj_kernel,
        grid=(b, s // tm),
        in_specs=[
            pl.BlockSpec((1, tm, d), lambda bi, i: (bi, i, 0)),
            pl.BlockSpec((1, d), const2),
            pl.BlockSpec((1, 1, tm), lambda bi, i: (bi, 0, i)),
            pl.BlockSpec((ROT_HALF, 1), const2),
            _resident((ATTN_WIDTH, d), lambda bi, i: (0, 0)),
            _resident((ATTN_WIDTH, d), lambda bi, i: (2, 0)),
            _resident((d, ATTN_WIDTH), lambda bi, i: (0, 1)),
            _resident((d, POOL_WIDTH), lambda bi, i: (0, 3)),
        ],
        out_specs=[
            pl.BlockSpec((1, N_ATTN_HEADS, HEAD_DIM, tm), lambda bi, i: (bi, 0, 0, i)),
            pl.BlockSpec((1, N_ATTN_HEADS, tm, HEAD_DIM), lambda bi, i: (bi, 0, i, 0)),
            pl.BlockSpec((1, N_ATTN_HEADS, V_ROWS, tm), lambda bi, i: (bi, 0, 0, i)),
            pl.BlockSpec((1, tm, POOL_WIDTH), lambda bi, i: (bi, i, 0)),
            pl.BlockSpec((1, bpt, 1, ATTN_WIDTH), lambda bi, i: (bi, i, 0, 0)),
        ],
        out_shape=[
            jax.ShapeDtypeStruct((b, N_ATTN_HEADS, HEAD_DIM, s), _BF16),
            jax.ShapeDtypeStruct((b, N_ATTN_HEADS, s, HEAD_DIM), _BF16),
            jax.ShapeDtypeStruct((b, N_ATTN_HEADS, V_ROWS, s), _BF16),
            jax.ShapeDtypeStruct((b, s, POOL_WIDTH), _F32),
            jax.ShapeDtypeStruct((b, nb, 1, ATTN_WIDTH), _F32),
        ],
        compiler_params=pltpu.CompilerParams(
            dimension_semantics=("parallel", "parallel"),
            vmem_limit_bytes=V7X_VMEM_LIMIT_BYTES),
        name="in_proj",
    )(x1, gain, pos_row, freq, w_in_t, w_in_t, w_in, w_in)


def _moba_kernel(qt_ref, k_ref, vt_ref, kmean_ref, *rest, nb, hg, n_riders):
    rider_in, rest = rest[:n_riders], rest[n_riders:]
    out_ref, rider_out, rest = rest[0], rest[1:1 + n_riders], rest[1 + n_riders:]
    bias_scr, acc_scr, m_scr, s_even, s_odd, top_even, top_odd = rest

    i = pl.program_id(2)
    even, odd = (s_even, top_even), (s_odd, top_odd)
    blk = MOBA_BLOCK
    kb_id = lax.broadcasted_iota(jnp.int32, (nb, blk), 0)
    past = kb_id < i
    kb_f = kb_id.astype(_F32)

    def gate_scores(h):
        km = kmean_ref[0, :, h * HEAD_DIM:(h + 1) * HEAD_DIM]
        km_hi = km.astype(_BF16)
        km_lo = (km - km_hi.astype(_F32)).astype(_BF16)
        return _dot(km_hi, qt_ref[0, h]) + _dot(km_lo, qt_ref[0, h])

    def select_blocks(h, gate):
        gate = jnp.where(past, gate, NEG_INF)
        bias = jnp.full((nb, blk), NEG_INF, _F32)
        for _ in range(MOBA_TOPK):
            top = jnp.max(gate, axis=0, keepdims=True)
            first = jnp.min(jnp.where(gate == top, kb_f, float(nb)), axis=0, keepdims=True)
            hit = kb_f == first
            bias = jnp.where(hit, 0.0, bias)
            gate = jnp.where(hit, -jnp.inf, gate)
        bias_scr[h] = jnp.where(past, bias, NEG_INF)

    def scores(h, n):
        kblk = k_ref[0, h, pl.ds(pl.multiple_of(n * blk, blk), blk), :]
        return _dot(kblk, qt_ref[0, h])

    def stage(h, n, staged):
        s_buf, top_buf = staged
        s = scores(h, n)
        s_buf[h] = s
        top_buf[h] = jnp.max(s, axis=0, keepdims=True)

    def weighted_values(h, n, p):
        vblk = vt_ref[0, h, :, pl.ds(pl.multiple_of(n * blk, blk), blk)]
        return _dot(vblk, p)

    key_pos = lax.broadcasted_iota(jnp.int32, (blk, blk), 0)
    qry_pos = lax.broadcasted_iota(jnp.int32, (blk, blk), 1)
    causal = key_pos <= qry_pos
    gates = [gate_scores(h) for h in range(hg)]
    ss = [jnp.where(causal, scores(h, i), NEG_INF) for h in range(hg)]
    for h in range(hg):
        select_blocks(h, gates[h])
    for h in range(hg):
        m_own = jnp.max(ss[h], axis=0, keepdims=True)
        m_scr[h] = m_own
        acc_scr[h] = weighted_values(h, i, jnp.exp2(ss[h] - m_own).astype(_BF16))
        stage(h, 0, even)

    def step(n, cur, nxt, stage_next):
        n_nxt = jnp.minimum(n + 1, nb - 1)
        for h in range(hg):
            if stage_next:
                stage(h, n_nxt, nxt)
            s = cur[0][h]
            b_n = bias_scr[h, pl.ds(n, 1), :]
            m_run = m_scr[h]
            m_new = jnp.maximum(m_run, cur[1][h] + b_n)
            m_scr[h] = m_new
            alpha = jnp.exp2(m_run - m_new)
            p = jnp.exp2(s - (m_new - b_n)).astype(_BF16)
            acc_scr[h] = alpha * acc_scr[h] + weighted_values(h, n, p)

    def run(n0, count, stage_last=True):
        for c in range(count):
            bufs = (even, odd) if c % 2 == 0 else (odd, even)
            step(n0 + c, *bufs, stage_next=stage_last or c + 1 < count)

    def body(j, carry):
        run(8 * j, 8)
        return carry

    lax.fori_loop(0, i // 8, body, 0)

    @pl.when(i % 8 >= 4)
    def _():
        run((i // 8) * 8, 4)

    @pl.when(i % 4 >= 2)
    def _():
        run((i // 4) * 4, 2)

    @pl.when(i % 2 == 1)
    def _():
        run(i - 1, 1, stage_last=False)

    _cast_riders(rider_in, rider_out)
    for h in range(hg):
        acc = acc_scr[h]
        out_ref[0, :, h * HEAD_DIM:(h + 1) * HEAD_DIM] = jnp.transpose(
            acc[0:HEAD_DIM] / acc[HEAD_DIM:HEAD_DIM + 1]).astype(out_ref.dtype)


def _moba(qt, k, vt, kmean, riders):
    b, nh, hd, s = qt.shape
    nb = s // MOBA_BLOCK
    hg = MOBA_HEADS_PER_STEP
    assert nh % hg == 0
    groups = nh // hg
    n_steps = b * groups * nb
    rider_specs = _rider_specs(riders, n_steps, lambda bi, g, i: (bi * groups + g) * nb + i)
    outs = pl.pallas_call(
        functools.partial(_moba_kernel, nb=nb, hg=hg, n_riders=len(riders)),
        grid=(b, groups, nb),
        in_specs=[
            pl.BlockSpec((1, hg, hd, MOBA_BLOCK), lambda bi, g, i: (bi, g, 0, i)),
            pl.BlockSpec((1, hg, s, hd), lambda bi, g, i: (bi, g, 0, 0)),
            pl.BlockSpec((1, hg, V_ROWS, s), lambda bi, g, i: (bi, g, 0, 0)),
            pl.BlockSpec((1, nb, hg * hd), lambda bi, g, i: (bi, 0, g)),
        ] + rider_specs,
        out_specs=[pl.BlockSpec((1, MOBA_BLOCK, hg * hd), lambda bi, g, i: (bi, i, g))] + rider_specs,
        out_shape=[jax.ShapeDtypeStruct((b, s, nh * hd), _BF16)]
                  + [jax.ShapeDtypeStruct(w.shape, _BF16) for w in riders],
        scratch_shapes=[pltpu.VMEM((hg, nb, MOBA_BLOCK), _F32),
                        pltpu.VMEM((hg, V_ROWS, MOBA_BLOCK), _F32),
                        pltpu.VMEM((hg, 1, MOBA_BLOCK), _F32),
                        pltpu.VMEM((hg, MOBA_BLOCK, MOBA_BLOCK), _F32),
                        pltpu.VMEM((hg, MOBA_BLOCK, MOBA_BLOCK), _F32),
                        pltpu.VMEM((hg, 1, MOBA_BLOCK), _F32),
                        pltpu.VMEM((hg, 1, MOBA_BLOCK), _F32)],
        compiler_params=pltpu.CompilerParams(
            dimension_semantics=("parallel", "parallel", "arbitrary"),
            vmem_limit_bytes=V7X_VMEM_LIMIT_BYTES),
        name="moba",
    )(qt, k, vt, kmean, *riders)
    return outs[0], outs[1:]


def _mix_out_kernel(x_ref, attn_ref, u_ref, halo_ref, pw_ref, ps_ref, wo_ref, out_ref, ubuf, lhs_buf):
    tm = x_ref.shape[1]
    si = pl.program_id(1)
    ubuf[0:POOL_HALO, :] = jnp.where(si > 0, halo_ref[0], 0.0)
    ubuf[POOL_HALO:, :] = u_ref[0]

    lhs_buf[:, 0:ATTN_WIDTH] = attn_ref[0]
    n_blocks = tm // MIX_ROW_BLOCK
    n_groups = len(POOL_WINDOWS)

    def pool_group(r, g):
        w = POOL_WINDOWS[g]
        row0 = r * MIX_ROW_BLOCK
        cols = slice(g * POOL_GROUP_DIM, (g + 1) * POOL_GROUP_DIM)
        t = si * tm + row0 + lax.broadcasted_iota(jnp.int32, (MIX_ROW_BLOCK, 1), 0)
        ext = ubuf[row0:row0 + POOL_HALO + MIX_ROW_BLOCK, cols]
        wsum = ext
        span = 1
        while span < w:
            wsum = wsum + pltpu.roll(wsum, span, 0)
            span *= 2
        own = ext[POOL_HALO:]
        count = jnp.minimum(t + 1, w).astype(_F32)
        pooled = wsum[POOL_HALO:] / count - own
        y = _dot(pooled.astype(_BF16), pw_ref[g]) * ps_ref[:, cols]
        lhs_buf[row0:row0 + MIX_ROW_BLOCK, ATTN_WIDTH + g * POOL_GROUP_DIM:ATTN_WIDTH + (g + 1) * POOL_GROUP_DIM] = (
            y.astype(_BF16))

    for g in range(n_groups):
        pool_group(0, g)
    n_col_blocks = out_ref.shape[2] // MIX_COL_BLOCK
    for r in range(n_blocks):
        rows = slice(r * MIX_ROW_BLOCK, (r + 1) * MIX_ROW_BLOCK)
        lhs = lhs_buf[rows, :]
        for c in range(n_col_blocks):
            cols = slice(c * MIX_COL_BLOCK, (c + 1) * MIX_COL_BLOCK)
            out_ref[0, rows, cols] = x_ref[0, rows, cols] + _dot(lhs, wo_ref[:, cols])
            if r + 1 < n_blocks and c < n_groups:
                pool_group(r + 1, c)


def _mix_out(x1, attn, u, pool_w, pool_scale, w_out):
    b, s, d = x1.shape
    tm = MIX_TOKEN_TILE
    assert s % tm == 0 and tm % POOL_HALO == 0
    assert all(w <= POOL_HALO and w & (w - 1) == 0 for w in POOL_WINDOWS)
    halo_per_tile = tm // POOL_HALO
    const2 = lambda bi, i: (0, 0)
    return pl.pallas_call(
        _mix_out_kernel,
        grid=(b, s // tm),
        in_specs=[
            pl.BlockSpec((1, tm, d), lambda bi, i: (bi, i, 0)),
            pl.BlockSpec((1, tm, ATTN_WIDTH), lambda bi, i: (bi, i, 0)),
            pl.BlockSpec((1, tm, POOL_WIDTH), lambda bi, i: (bi, i, 0)),
            pl.BlockSpec((1, POOL_HALO, POOL_WIDTH),
                         lambda bi, i: (bi, jnp.maximum(i * halo_per_tile - 1, 0), 0)),
            _resident(pool_w.shape, lambda bi, i: (0, 0, 0)),
            pl.BlockSpec((1, POOL_WIDTH), const2),
            _resident(w_out.shape, const2),
        ],
        out_specs=pl.BlockSpec((1, tm, d), lambda bi, i: (bi, i, 0)),
        out_shape=jax.ShapeDtypeStruct((b, s, d), _F32),
        scratch_shapes=[pltpu.VMEM((POOL_HALO + tm, POOL_WIDTH), _F32), pltpu.VMEM((tm, ATTN_WIDTH + POOL_WIDTH), _BF16)],
        compiler_params=pltpu.CompilerParams(
            dimension_semantics=("parallel", "parallel"),
            vmem_limit_bytes=V7X_VMEM_LIMIT_BYTES),
        name="mix_out",
    )(x1, attn, u, u, pool_w, pool_scale, w_out)


def kernel(x, positions, norm_ffn1, w1_gate, w1_up, w1_down, norm_mix, w_in, pool_w, pool_scale,
           w_out, norm_ffn2, w2_gate, w2_up, w2_down, norm_final):
    b, s, d = x.shape
    depth = w_in.shape[0]
    inv_freq = ROPE_THETA ** (-jnp.arange(0, ROT_DIM, 2, dtype=_F32) / ROT_DIM)
    freq = inv_freq[:, None]
    fgain = norm_final[None, :]

    for l in range(depth):
        x1, (w_in_b,) = _ffn(x.reshape(b * s, d), norm_ffn1[l][None, :], w1_gate[l], w1_up[l], w1_down[l],
                             fgain, [w_in[l]], final_norm=False, name="ffn_pre")
        x1 = x1.reshape(b, s, d)
        qt, k, vt, u, kmean = _in_proj(x1, norm_mix[l][None, :], positions, freq, w_in_b, w_in_b.T)
        riders = [w2_gate[l], w2_up[l], w2_down[l], w_out[l], pool_w[l].reshape(-1, POOL_GROUP_DIM)]
        attn, (wg2, wu2, wd2, wo, pw) = _moba(qt, k, vt, kmean.reshape(b, s // MOBA_BLOCK, ATTN_WIDTH), riders)
        x2 = _mix_out(x1, attn, u, pw.reshape(pool_w.shape[1:]), pool_scale[l][None, :], wo)
        last = l == depth - 1
        x = _ffn_bf16(x2.reshape(b * s, d), norm_ffn2[l][None, :], wg2, wu2, wd2,
                      fgain, final_norm=last, name="ffn_post").reshape(b, s, d)
    return x
```

```python
import functools

import jax
import jax.numpy as jnp
from jax import lax
from jax.experimental import pallas as pl
from jax.experimental.pallas import tpu as pltpu

N_ATTN_HEADS = 8
HEAD_DIM = 128
ATTN_WIDTH = N_ATTN_HEADS * HEAD_DIM
POOL_WINDOWS = (2, 4, 8, 16)
POOL_GROUP_DIM = 256
POOL_WIDTH = POOL_GROUP_DIM * len(POOL_WINDOWS)
MOBA_BLOCK = 256
MOBA_TOPK = 3
ROT_DIM = HEAD_DIM // 4
ROT_HALF = ROT_DIM // 2
ROPE_THETA = 500000.0
EPS = 1e-6
NEG_INF = -1e30
LOG2_E = 1.4426950408889634
QUERY_SCALE = (HEAD_DIM ** -0.5) * LOG2_E
V7X_BF16_ROWS_PER_VREG = 16
V_ROWS = HEAD_DIM + V7X_BF16_ROWS_PER_VREG
POOL_HALO = 16

V7X_VMEM_LIMIT_BYTES = 58 * 1024 * 1024

FFN_TOKEN_TILE = 1024
FFN_FF_TILE = 512
FFN_FIRST_FF_TILE = 256
PROJ_TOKEN_TILE = 512
MIX_TOKEN_TILE = 512
MIX_ROW_BLOCK = 256
MIX_COL_BLOCK = 256
MOBA_HEADS_PER_STEP = 8

_BF16 = jnp.bfloat16
_F32 = jnp.float32


def _rms(x, gain):
    inv = lax.rsqrt(jnp.mean(x * x, axis=-1, keepdims=True) + EPS)
    return x * inv * gain


def _dot(a, b):
    return jnp.dot(a, b, preferred_element_type=_F32)


def _dot_nt(a, b):
    return lax.dot_general(a, b, (((1,), (1,)), ((), ())), preferred_element_type=_F32)


def _resident(block_shape, index_map):
    return pl.BlockSpec(block_shape, index_map, pipeline_mode=pl.Buffered(1))


def _cast_riders(rider_in, rider_out):
    for src_ref, dst_ref in zip(rider_in, rider_out):
        dst_ref[...] = src_ref[...].astype(_BF16)


def _rider_specs(riders, n_steps, step_of):
    specs = []
    for w in riders:
        n_slices = next(n for n in range(n_steps, 0, -1)
                        if w.shape[0] % n == 0 and (w.shape[0] // n) % V7X_BF16_ROWS_PER_VREG == 0)
        specs.append(pl.BlockSpec(
            (w.shape[0] // n_slices, w.shape[1]),
            lambda *ids, n_slices=n_slices: (jnp.minimum(step_of(*ids), n_slices - 1), 0)))
    return specs


def _ffn_step(x_ref, gain_ref, load_weights, fgain_ref, out_ref, h_scr, final_norm):
    f = pl.program_id(1)

    @pl.when(f == 0)
    def _():
        x = x_ref[...]
        h_scr[...] = _rms(x, gain_ref[...]).astype(_BF16)
        out_ref[...] = x

    wg, wu, wd = load_weights()
    h = h_scr[...]
    g = _dot(h, wg)
    u = _dot(h, wu)
    g_half = 0.5 * g
    a = ((g_half + g_half * jnp.tanh(g_half)) * (0.5 * u)).astype(_BF16)
    out_ref[...] += _dot(a, wd)

    if final_norm:
        @pl.when(f == pl.num_programs(1) - 1)
        def _():
            out_ref[...] = _rms(out_ref[...], fgain_ref[...])


def _ffn_first_tile_kernel(x_ref, gain_ref, wg32_ref, wu32_ref, wd32_ref, fgain_ref,
                           out_ref, wg_ref, wu_ref, wd_ref, h_scr, *, final_norm):
    def load_weights():
        wg = wg32_ref[...].astype(_BF16)
        wu = wu32_ref[...].astype(_BF16)
        wd = wd32_ref[...].astype(_BF16)
        wg_ref[...] = wg
        wu_ref[...] = wu
        wd_ref[...] = wd
        return wg, wu, wd

    _ffn_step(x_ref, gain_ref, load_weights, fgain_ref, out_ref, h_scr, final_norm)


def _ffn_rest_kernel(x_ref, gain_ref, wg_ref, wu_ref, wd_ref, fgain_ref, first_hbm, *rest, final_norm, n_riders):
    rider_in, rest = rest[:n_riders], rest[n_riders:]
    out_ref, rider_out, (h_scr, copy_sem) = rest[0], rest[1:1 + n_riders], rest[1 + n_riders:]
    _cast_riders(rider_in, rider_out)
    i = pl.program_id(0)

    @pl.when((i == 0) & (pl.program_id(1) == 0))
    def _():
        copy = pltpu.make_async_copy(first_hbm, out_ref, copy_sem)
        copy.start()
        copy.wait()

    @pl.when(i > 0)
    def _():
        load_weights = lambda: (wg_ref[...], wu_ref[...], wd_ref[...])
        _ffn_step(x_ref, gain_ref, load_weights, fgain_ref, out_ref, h_scr, final_norm)


def _ffn(x2d, gain, wg32, wu32, wd32, fgain, riders, *, final_norm, name):
    t, d = x2d.shape
    d_ff = wg32.shape[1]
    tm, tf, tf0 = FFN_TOKEN_TILE, FFN_FF_TILE, FFN_FIRST_FF_TILE
    assert t % tm == 0 and d_ff % tf == 0 and d_ff % tf0 == 0
    params = pltpu.CompilerParams(dimension_semantics=("parallel", "arbitrary"),
                                  vmem_limit_bytes=V7X_VMEM_LIMIT_BYTES)
    row = lambda i, f: (0, 0)
    first, wg, wu, wd = pl.pallas_call(
        functools.partial(_ffn_first_tile_kernel, final_norm=final_norm),
        grid=(1, d_ff // tf0),
        in_specs=[
            _resident((tm, d), lambda i, f: (0, 0)),
            pl.BlockSpec((1, d), row),
            pl.BlockSpec((d, tf0), lambda i, f: (0, f)),
            pl.BlockSpec((d, tf0), lambda i, f: (0, f)),
            pl.BlockSpec((tf0, d), lambda i, f: (f, 0)),
            pl.BlockSpec((1, d), row),
        ],
        out_specs=[
            pl.BlockSpec((tm, d), lambda i, f: (0, 0)),
            pl.BlockSpec((d, tf0), lambda i, f: (0, f)),
            pl.BlockSpec((d, tf0), lambda i, f: (0, f)),
            pl.BlockSpec((tf0, d), lambda i, f: (f, 0)),
        ],
        out_shape=[
            jax.ShapeDtypeStruct((tm, d), _F32),
            jax.ShapeDtypeStruct(wg32.shape, _BF16),
            jax.ShapeDtypeStruct(wu32.shape, _BF16),
            jax.ShapeDtypeStruct(wd32.shape, _BF16),
        ],
        scratch_shapes=[pltpu.VMEM((tm, d), _BF16)],
        compiler_params=params,
        name=name + "_first",
    )(x2d, gain, wg32, wu32, wd32, fgain)
    n_f = d_ff // tf
    rider_specs = _rider_specs(riders, (t // tm) * n_f, lambda i, f: i * n_f + f)
    outs = pl.pallas_call(
        functools.partial(_ffn_rest_kernel, final_norm=final_norm, n_riders=len(riders)),
        grid=(t // tm, n_f),
        in_specs=[
            pl.BlockSpec((tm, d), lambda i, f: (jnp.maximum(i, 1), 0)),
            pl.BlockSpec((1, d), row),
            pl.BlockSpec((d, tf), lambda i, f: (0, jnp.where(i == 0, 0, f))),
            pl.BlockSpec((d, tf), lambda i, f: (0, jnp.where(i == 0, 0, f))),
            pl.BlockSpec((tf, d), lambda i, f: (jnp.where(i == 0, 0, f), 0)),
            pl.BlockSpec((1, d), row),
            pl.BlockSpec(memory_space=pl.ANY),
        ] + rider_specs,
        out_specs=[pl.BlockSpec((tm, d), lambda i, f: (i, 0))] + rider_specs,
        out_shape=[jax.ShapeDtypeStruct((t, d), _F32)] + [jax.ShapeDtypeStruct(w.shape, _BF16) for w in riders],
        scratch_shapes=[pltpu.VMEM((tm, d), _BF16), pltpu.SemaphoreType.DMA(())],
        compiler_params=params,
        name=name + "_rest",
    )(x2d, gain, wg, wu, wd, fgain, first, *riders)
    return outs[0], outs[1:]


def _ffn_bf16_kernel(x_ref, gain_ref, wg_ref, wu_ref, wd_ref, fgain_ref, out_ref, h_scr, *, final_norm):
    load_weights = lambda: (wg_ref[...], wu_ref[...], wd_ref[...])
    _ffn_step(x_ref, gain_ref, load_weights, fgain_ref, out_ref, h_scr, final_norm)


def _ffn_bf16(x2d, gain, wg, wu, wd, fgain, *, final_norm, name):
    t, d = x2d.shape
    d_ff = wg.shape[1]
    tm, tf = FFN_TOKEN_TILE, FFN_FF_TILE
    assert t % tm == 0 and d_ff % tf == 0
    row = lambda i, f: (0, 0)
    return pl.pallas_call(
        functools.partial(_ffn_bf16_kernel, final_norm=final_norm),
        grid=(t // tm, d_ff // tf),
        in_specs=[
            pl.BlockSpec((tm, d), lambda i, f: (i, 0)),
            pl.BlockSpec((1, d), row),
            pl.BlockSpec((d, tf), lambda i, f: (0, f)),
            pl.BlockSpec((d, tf), lambda i, f: (0, f)),
            pl.BlockSpec((tf, d), lambda i, f: (f, 0)),
            pl.BlockSpec((1, d), row),
        ],
        out_specs=pl.BlockSpec((tm, d), lambda i, f: (i, 0)),
        out_shape=jax.ShapeDtypeStruct((t, d), _F32),
        scratch_shapes=[pltpu.VMEM((tm, d), _BF16)],
        compiler_params=pltpu.CompilerParams(dimension_semantics=("parallel", "arbitrary"),
                                             vmem_limit_bytes=V7X_VMEM_LIMIT_BYTES),
        name=name,
    )(x2d, gain, wg, wu, wd, fgain)


def _in_proj_kernel(x_ref, gain_ref, pos_ref, freq_ref, wqt_ref, wvt_ref, wk_ref, wu_ref,
                    qt_ref, k_ref, vt_ref, u_ref, kmean_ref):
    tm = x_ref.shape[1]
    h = _rms(x_ref[0], gain_ref[...]).astype(_BF16)

    k = _dot(h, wk_ref[...])
    u = _dot(h, wu_ref[...])
    qt = _dot_nt(wqt_ref[...], h)
    vt = _dot_nt(wvt_ref[...], h)

    ang_t = freq_ref[...] * pos_ref[0].astype(_F32)
    cos_t, sin_t = jnp.cos(ang_t), jnp.sin(ang_t)
    pad = HEAD_DIM - ROT_DIM
    cos_n = jnp.transpose(jnp.concatenate([cos_t, cos_t, jnp.ones((pad, tm), _F32)], axis=0))
    sin_n = jnp.transpose(jnp.concatenate([-sin_t, sin_t, jnp.zeros((pad, tm), _F32)], axis=0))
    lane = lax.broadcasted_iota(jnp.int32, (1, HEAD_DIM), 1)

    u_ref[0] = u
    for hh in range(N_ATTN_HEADS):
        kh = k[:, hh * HEAD_DIM:(hh + 1) * HEAD_DIM]
        partner = jnp.where(lane < ROT_HALF,
                            pltpu.roll(kh, HEAD_DIM - ROT_HALF, 1),
                            pltpu.roll(kh, ROT_HALF, 1))
        kr = kh * cos_n + partner * sin_n
        k_ref[0, hh] = kr.astype(_BF16)
        for j in range(tm // MOBA_BLOCK):
            kmean_ref[0, j, :, hh * HEAD_DIM:(hh + 1) * HEAD_DIM] = jnp.mean(
                kr[j * MOBA_BLOCK:(j + 1) * MOBA_BLOCK], axis=0, keepdims=True)

    cos_q, sin_q = cos_t * QUERY_SCALE, sin_t * QUERY_SCALE
    for hh in range(N_ATTN_HEADS):
        base = hh * HEAD_DIM
        x1 = qt[base:base + ROT_HALF]
        x2 = qt[base + ROT_HALF:base + ROT_DIM]
        qt_ref[0, hh, 0:ROT_HALF, :] = (x1 * cos_q - x2 * sin_q).astype(_BF16)
        qt_ref[0, hh, ROT_HALF:ROT_DIM, :] = (x2 * cos_q + x1 * sin_q).astype(_BF16)
        qt_ref[0, hh, ROT_DIM:, :] = (qt[base + ROT_DIM:base + HEAD_DIM] * QUERY_SCALE).astype(_BF16)

    ones = jnp.ones((V_ROWS - HEAD_DIM, tm), _BF16)
    for hh in range(N_ATTN_HEADS):
        vt_ref[0, hh, 0:HEAD_DIM, :] = vt[hh * HEAD_DIM:(hh + 1) * HEAD_DIM].astype(_BF16)
        vt_ref[0, hh, HEAD_DIM:, :] = ones


def _in_proj(x1, gain, positions, freq, w_in, w_in_t):
    b, s, d = x1.shape
    tm = PROJ_TOKEN_TILE
    assert s % tm == 0 and tm % MOBA_BLOCK == 0
    nb = s // MOBA_BLOCK
    bpt = tm // MOBA_BLOCK
    pos_row = positions.reshape(b, 1, s)
    const2 = lambda bi, i: (0, 0)
    return pl.pallas_call(
        _in_proj_kernel,
        grid=(b, s // tm),
        in_specs=[
            pl.BlockSpec((1, tm, d), lambda bi, i: (bi, i, 0)),
            pl.BlockSpec((1, d), const2),
            pl.BlockSpec((1, 1, tm), lambda bi, i: (bi, 0, i)),
            pl.BlockSpec((ROT_HALF, 1), const2),
            _resident((ATTN_WIDTH, d), lambda bi, i: (0, 0)),
            _resident((ATTN_WIDTH, d), lambda bi, i: (2, 0)),
            _resident((d, ATTN_WIDTH), lambda bi, i: (0, 1)),
            _resident((d, POOL_WIDTH), lambda bi, i: (0, 3)),
        ],
        out_specs=[
            pl.BlockSpec((1, N_ATTN_HEADS, HEAD_DIM, tm), lambda bi, i: (bi, 0, 0, i)),
            pl.BlockSpec((1, N_ATTN_HEADS, tm, HEAD_DIM), lambda bi, i: (bi, 0, i, 0)),
            pl.BlockSpec((1, N_ATTN_HEADS, V_ROWS, tm), lambda bi, i: (bi, 0, 0, i)),
            pl.BlockSpec((1, tm, POOL_WIDTH), lambda bi, i: (bi, i, 0)),
            pl.BlockSpec((1, bpt, 1, ATTN_WIDTH), lambda bi, i: (bi, i, 0, 0)),
        ],
        out_shape=[
            jax.ShapeDtypeStruct((b, N_ATTN_HEADS, HEAD_DIM, s), _BF16),
            jax.ShapeDtypeStruct((b, N_ATTN_HEADS, s, HEAD_DIM), _BF16),
            jax.ShapeDtypeStruct((b, N_ATTN_HEADS, V_ROWS, s), _BF16),
            jax.ShapeDtypeStruct((b, s, POOL_WIDTH), _F32),
            jax.ShapeDtypeStruct((b, nb, 1, ATTN_WIDTH), _F32),
        ],
        compiler_params=pltpu.CompilerParams(
            dimension_semantics=("parallel", "parallel"),
            vmem_limit_bytes=V7X_VMEM_LIMIT_BYTES),
        name="in_proj",
    )(x1, gain, pos_row, freq, w_in_t, w_in_t, w_in, w_in)


def _moba_kernel(qt_ref, k_ref, vt_ref, kmean_ref, *rest, nb, hg, n_riders):
    rider_in, rest = rest[:n_riders], rest[n_riders:]
    out_ref, rider_out, rest = rest[0], rest[1:1 + n_riders], rest[1 + n_riders:]
    bias_scr, acc_scr, m_scr, s_even, s_odd, top_even, top_odd = rest

    i = pl.program_id(2)
    even, odd = (s_even, top_even), (s_odd, top_odd)
    blk = MOBA_BLOCK
    kb_id = lax.broadcasted_iota(jnp.int32, (nb, blk), 0)
    past = kb_id < i
    kb_f = kb_id.astype(_F32)

    def gate_scores(h):
        km = kmean_ref[0, :, h * HEAD_DIM:(h + 1) * HEAD_DIM]
        km_hi = km.astype(_BF16)
        km_lo = (km - km_hi.astype(_F32)).astype(_BF16)
        return _dot(km_hi, qt_ref[0, h]) + _dot(km_lo, qt_ref[0, h])

    def select_blocks(h, gate):
        gate = jnp.where(past, gate, NEG_INF)
        bias = jnp.full((nb, blk), NEG_INF, _F32)
        for _ in range(MOBA_TOPK):
            top = jnp.max(gate, axis=0, keepdims=True)
            first = jnp.min(jnp.where(gate == top, kb_f, float(nb)), axis=0, keepdims=True)
            hit = kb_f == first
            bias = jnp.where(hit, 0.0, bias)
            gate = jnp.where(hit, -jnp.inf, gate)
        bias_scr[h] = jnp.where(past, bias, NEG_INF)

    def scores(h, n):
        kblk = k_ref[0, h, pl.ds(pl.multiple_of(n * blk, blk), blk), :]
        return _dot(kblk, qt_ref[0, h])

    def stage(h, n, staged):
        s_buf, top_buf = staged
        s = scores(h, n)
        s_buf[h] = s
        top_buf[h] = jnp.max(s, axis=0, keepdims=True)

    def weighted_values(h, n, p):
        vblk = vt_ref[0, h, :, pl.ds(pl.multiple_of(n * blk, blk), blk)]
        return _dot(vblk, p)

    key_pos = lax.broadcasted_iota(jnp.int32, (blk, blk), 0)
    qry_pos = lax.broadcasted_iota(jnp.int32, (blk, blk), 1)
    causal = key_pos <= qry_pos
    gates = [gate_scores(h) for h in range(hg)]
    ss = [jnp.where(causal, scores(h, i), NEG_INF) for h in range(hg)]
    for h in range(hg):
        select_blocks(h, gates[h])
    for h in range(hg):
        m_own = jnp.max(ss[h], axis=0, keepdims=True)
        m_scr[h] = m_own
        acc_scr[h] = weighted_values(h, i, jnp.exp2(ss[h] - m_own).astype(_BF16))
        stage(h, 0, even)

    def step(n, cur, nxt, stage_next):
        n_nxt = jnp.minimum(n + 1, nb - 1)
        for h in range(hg):
            if stage_next:
                stage(h, n_nxt, nxt)
            s = cur[0][h]
            b_n = bias_scr[h, pl.ds(n, 1), :]
            m_run = m_scr[h]
            m_new = jnp.maximum(m_run, cur[1][h] + b_n)
            m_scr[h] = m_new
            alpha = jnp.exp2(m_run - m_new)
            p = jnp.exp2(s - (m_new - b_n)).astype(_BF16)
            acc_scr[h] = alpha * acc_scr[h] + weighted_values(h, n, p)

    def run(n0, count, stage_last=True):
        for c in range(count):
            bufs = (even, odd) if c % 2 == 0 else (odd, even)
            step(n0 + c, *bufs, stage_next=stage_last or c + 1 < count)

    def body(j, carry):
        run(8 * j, 8)
        return carry

    lax.fori_loop(0, i // 8, body, 0)

    @pl.when(i % 8 >= 4)
    def _():
        run((i // 8) * 8, 4)

    @pl.when(i % 4 >= 2)
    def _():
        run((i // 4) * 4, 2)

    @pl.when(i % 2 == 1)
    def _():
        run(i - 1, 1, stage_last=False)

    _cast_riders(rider_in, rider_out)
    for h in range(hg):
        acc = acc_scr[h]
        out_ref[0, :, h * HEAD_DIM:(h + 1) * HEAD_DIM] = jnp.transpose(
            acc[0:HEAD_DIM] / acc[HEAD_DIM:HEAD_DIM + 1]).astype(out_ref.dtype)


def _moba(qt, k, vt, kmean, riders):
    b, nh, hd, s = qt.shape
    nb = s // MOBA_BLOCK
    hg = MOBA_HEADS_PER_STEP
    assert nh % hg == 0
    groups = nh // hg
    n_steps = b * groups * nb
    rider_specs = _rider_specs(riders, n_steps, lambda bi, g, i: (bi * groups + g) * nb + i)
    outs = pl.pallas_call(
        functools.partial(_moba_kernel, nb=nb, hg=hg, n_riders=len(riders)),
        grid=(b, groups, nb),
        in_specs=[
            pl.BlockSpec((1, hg, hd, MOBA_BLOCK), lambda bi, g, i: (bi, g, 0, i)),
            pl.BlockSpec((1, hg, s, hd), lambda bi, g, i: (bi, g, 0, 0)),
            pl.BlockSpec((1, hg, V_ROWS, s), lambda bi, g, i: (bi, g, 0, 0)),
            pl.BlockSpec((1, nb, hg * hd), lambda bi, g, i: (bi, 0, g)),
        ] + rider_specs,
        out_specs=[pl.BlockSpec((1, MOBA_BLOCK, hg * hd), lambda bi, g, i: (bi, i, g))] + rider_specs,
        out_shape=[jax.ShapeDtypeStruct((b, s, nh * hd), _BF16)]
                  + [jax.ShapeDtypeStruct(w.shape, _BF16) for w in riders],
        scratch_shapes=[pltpu.VMEM((hg, nb, MOBA_BLOCK), _F32),
                        pltpu.VMEM((hg, V_ROWS, MOBA_BLOCK), _F32),
                        pltpu.VMEM((hg, 1, MOBA_BLOCK), _F32),
                        pltpu.VMEM((hg, MOBA_BLOCK, MOBA_BLOCK), _F32),
                        pltpu.VMEM((hg, MOBA_BLOCK, MOBA_BLOCK), _F32),
                        pltpu.VMEM((hg, 1, MOBA_BLOCK), _F32),
                        pltpu.VMEM((hg, 1, MOBA_BLOCK), _F32)],
        compiler_params=pltpu.CompilerParams(
            dimension_semantics=("parallel", "parallel", "arbitrary"),
            vmem_limit_bytes=V7X_VMEM_LIMIT_BYTES),
        name="moba",
    )(qt, k, vt, kmean, *riders)
    return outs[0], outs[1:]


def _mix_out_kernel(x_ref, attn_ref, u_ref, halo_ref, pw_ref, ps_ref, wo_ref, out_ref, ubuf, lhs_buf):
    tm = x_ref.shape[1]
    si = pl.program_id(1)
    ubuf[0:POOL_HALO, :] = jnp.where(si > 0, halo_ref[0], 0.0)
    ubuf[POOL_HALO:, :] = u_ref[0]

    lhs_buf[:, 0:ATTN_WIDTH] = attn_ref[0]
    n_blocks = tm // MIX_ROW_BLOCK
    n_groups = len(POOL_WINDOWS)

    def pool_group(r, g):
        w = POOL_WINDOWS[g]
        row0 = r * MIX_ROW_BLOCK
        cols = slice(g * POOL_GROUP_DIM, (g + 1) * POOL_GROUP_DIM)
        t = si * tm + row0 + lax.broadcasted_iota(jnp.int32, (MIX_ROW_BLOCK, 1), 0)
        ext = ubuf[row0:row0 + POOL_HALO + MIX_ROW_BLOCK, cols]
        wsum = ext
        span = 1
        while span < w:
            wsum = wsum + pltpu.roll(wsum, span, 0)
            span *= 2
        own = ext[POOL_HALO:]
        count = jnp.minimum(t + 1, w).astype(_F32)
        pooled = wsum[POOL_HALO:] / count - own
        y = _dot(pooled.astype(_BF16), pw_ref[g]) * ps_ref[:, cols]
        lhs_buf[row0:row0 + MIX_ROW_BLOCK, ATTN_WIDTH + g * POOL_GROUP_DIM:ATTN_WIDTH + (g + 1) * POOL_GROUP_DIM] = (
            y.astype(_BF16))

    for g in range(n_groups):
        pool_group(0, g)
    n_col_blocks = out_ref.shape[2] // MIX_COL_BLOCK
    for r in range(n_blocks):
        rows = slice(r * MIX_ROW_BLOCK, (r + 1) * MIX_ROW_BLOCK)
        lhs = lhs_buf[rows, :]
        for c in range(n_col_blocks):
            cols = slice(c * MIX_COL_BLOCK, (c + 1) * MIX_COL_BLOCK)
            out_ref[0, rows, cols] = x_ref[0, rows, cols] + _dot(lhs, wo_ref[:, cols])
            if r + 1 < n_blocks and c < n_groups:
                pool_group(r + 1, c)


def _mix_out(x1, attn, u, pool_w, pool_scale, w_out):
    b, s, d = x1.shape
    tm = MIX_TOKEN_TILE
    assert s % tm == 0 and tm % POOL_HALO == 0
    assert all(w <= POOL_HALO and w & (w - 1) == 0 for w in POOL_WINDOWS)
    halo_per_tile = tm // POOL_HALO
    const2 = lambda bi, i: (0, 0)
    return pl.pallas_call(
        _mix_out_kernel,
        grid=(b, s // tm),
        in_specs=[
            pl.BlockSpec((1, tm, d), lambda bi, i: (bi, i, 0)),
            pl.BlockSpec((1, tm, ATTN_WIDTH), lambda bi, i: (bi, i, 0)),
            pl.BlockSpec((1, tm, POOL_WIDTH), lambda bi, i: (bi, i, 0)),
            pl.BlockSpec((1, POOL_HALO, POOL_WIDTH),
                         lambda bi, i: (bi, jnp.maximum(i * halo_per_tile - 1, 0), 0)),
            _resident(pool_w.shape, lambda bi, i: (0, 0, 0)),
            pl.BlockSpec((1, POOL_WIDTH), const2),
            _resident(w_out.shape, const2),
        ],
        out_specs=pl.BlockSpec((1, tm, d), lambda bi, i: (bi, i, 0)),
        out_shape=jax.ShapeDtypeStruct((b, s, d), _F32),
        scratch_shapes=[pltpu.VMEM((POOL_HALO + tm, POOL_WIDTH), _F32), pltpu.VMEM((tm, ATTN_WIDTH + POOL_WIDTH), _BF16)],
        compiler_params=pltpu.CompilerParams(
            dimension_semantics=("parallel", "parallel"),
            vmem_limit_bytes=V7X_VMEM_LIMIT_BYTES),
        name="mix_out",
    )(x1, attn, u, u, pool_w, pool_scale, w_out)


def kernel(x, positions, norm_ffn1, w1_gate, w1_up, w1_down, norm_mix, w_in, pool_w, pool_scale,
           w_out, norm_ffn2, w2_gate, w2_up, w2_down, norm_final):
    b, s, d = x.shape
    depth = w_in.shape[0]
    inv_freq = ROPE_THETA ** (-jnp.arange(0, ROT_DIM, 2, dtype=_F32) / ROT_DIM)
    freq = inv_freq[:, None]
    fgain = norm_final[None, :]

    for l in range(depth):
        w_in_b = w_in[l].astype(_BF16)
        w_in_t = w_in_b.T
        x1, _ = _ffn(x.reshape(b * s, d), norm_ffn1[l][None, :], w1_gate[l], w1_up[l], w1_down[l],
                     fgain, [], final_norm=False, name="ffn_pre")
        x1 = x1.reshape(b, s, d)
        qt, k, vt, u, kmean = _in_proj(x1, norm_mix[l][None, :], positions, freq, w_in_b, w_in_t)
        riders = [w2_gate[l], w2_up[l], w2_down[l], w_out[l], pool_w[l].reshape(-1, POOL_GROUP_DIM)]
        attn, (wg2, wu2, wd2, wo, pw) = _moba(qt, k, vt, kmean.reshape(b, s // MOBA_BLOCK, ATTN_WIDTH), riders)
        x2 = _mix_out(x1, attn, u, pw.reshape(pool_w.shape[1:]), pool_scale[l][None, :], wo)
        last = l == depth - 1
        x = _ffn_bf16(x2.reshape(b * s, d), norm_ffn2[l][None, :], wg2, wu2, wd2,
                      fgain, final_norm=last, name="ffn_post").reshape(b, s, d)
    return x
```

```python
import functools

import jax
import jax.numpy as jnp
from jax import lax
from jax.experimental import pallas as pl
from jax.experimental.pallas import tpu as pltpu

N_ATTN_HEADS = 8
HEAD_DIM = 128
ATTN_WIDTH = N_ATTN_HEADS * HEAD_DIM
POOL_WINDOWS = (2, 4, 8, 16)
POOL_GROUP_DIM = 256
POOL_WIDTH = POOL_GROUP_DIM * len(POOL_WINDOWS)
MOBA_BLOCK = 256
MOBA_TOPK = 3
ROT_DIM = HEAD_DIM // 4
ROT_HALF = ROT_DIM // 2
ROPE_THETA = 500000.0
EPS = 1e-6
NEG_INF = -1e30
LOG2_E = 1.4426950408889634
QUERY_SCALE = (HEAD_DIM ** -0.5) * LOG2_E
V7X_BF16_ROWS_PER_VREG = 16
V_ROWS = HEAD_DIM + V7X_BF16_ROWS_PER_VREG
POOL_HALO = 16

V7X_VMEM_LIMIT_BYTES = 58 * 1024 * 1024

FFN_TOKEN_TILE = 1024
FFN_FF_TILE = 512
FFN_FIRST_FF_TILE = 256
PROJ_TOKEN_TILE = 512
MIX_TOKEN_TILE = 512
MIX_ROW_BLOCK = 256
MIX_COL_BLOCK = 256
MOBA_HEADS_PER_STEP = 8

_BF16 = jnp.bfloat16
_F32 = jnp.float32


def _rms(x, gain):
    inv = lax.rsqrt(jnp.mean(x * x, axis=-1, keepdims=True) + EPS)
    return x * inv * gain


def _dot(a, b):
    return jnp.dot(a, b, preferred_element_type=_F32)


def _dot_nt(a, b):
    return lax.dot_general(a, b, (((1,), (1,)), ((), ())), preferred_element_type=_F32)


def _resident(block_shape, index_map):
    return pl.BlockSpec(block_shape, index_map, pipeline_mode=pl.Buffered(1))


def _cast_riders(rider_in, rider_out):
    for src_ref, dst_ref in zip(rider_in, rider_out):
        dst_ref[...] = src_ref[...].astype(_BF16)


def _rider_specs(riders, n_steps, step_of):
    specs = []
    for w in riders:
        n_slices = next(n for n in range(n_steps, 0, -1)
                        if w.shape[0] % n == 0 and (w.shape[0] // n) % V7X_BF16_ROWS_PER_VREG == 0)
        specs.append(pl.BlockSpec(
            (w.shape[0] // n_slices, w.shape[1]),
            lambda *ids, n_slices=n_slices: (jnp.minimum(step_of(*ids), n_slices - 1), 0)))
    return specs


def _ffn_step(x_ref, gain_ref, load_weights, fgain_ref, out_ref, h_scr, final_norm):
    f = pl.program_id(1)

    @pl.when(f == 0)
    def _():
        x = x_ref[...]
        h_scr[...] = _rms(x, gain_ref[...]).astype(_BF16)
        out_ref[...] = x

    wg, wu, wd = load_weights()
    h = h_scr[...]
    g = _dot(h, wg)
    u = _dot(h, wu)
    g_half = 0.5 * g
    a = ((g_half + g_half * jnp.tanh(g_half)) * (0.5 * u)).astype(_BF16)
    out_ref[...] += _dot(a, wd)

    if final_norm:
        @pl.when(f == pl.num_programs(1) - 1)
        def _():
            out_ref[...] = _rms(out_ref[...], fgain_ref[...])


def _ffn_first_tile_kernel(x_ref, gain_ref, wg32_ref, wu32_ref, wd32_ref, fgain_ref,
                           out_ref, wg_ref, wu_ref, wd_ref, h_scr, *, final_norm):
    def load_weights():
        wg = wg32_ref[...].astype(_BF16)
        wu = wu32_ref[...].astype(_BF16)
        wd = wd32_ref[...].astype(_BF16)
        wg_ref[...] = wg
        wu_ref[...] = wu
        wd_ref[...] = wd
        return wg, wu, wd

    _ffn_step(x_ref, gain_ref, load_weights, fgain_ref, out_ref, h_scr, final_norm)


def _ffn_rest_kernel(x_ref, gain_ref, wg_ref, wu_ref, wd_ref, fgain_ref, first_hbm, *rest, final_norm, n_riders):
    rider_in, rest = rest[:n_riders], rest[n_riders:]
    out_ref, rider_out, (h_scr, copy_sem) = rest[0], rest[1:1 + n_riders], rest[1 + n_riders:]
    _cast_riders(rider_in, rider_out)
    i = pl.program_id(0)

    @pl.when((i == 0) & (pl.program_id(1) == 0))
    def _():
        copy = pltpu.make_async_copy(first_hbm, out_ref, copy_sem)
        copy.start()
        copy.wait()

    @pl.when(i > 0)
    def _():
        load_weights = lambda: (wg_ref[...], wu_ref[...], wd_ref[...])
        _ffn_step(x_ref, gain_ref, load_weights, fgain_ref, out_ref, h_scr, final_norm)


def _ffn(x2d, gain, wg32, wu32, wd32, fgain, riders, *, final_norm, name):
    t, d = x2d.shape
    d_ff = wg32.shape[1]
    tm, tf, tf0 = FFN_TOKEN_TILE, FFN_FF_TILE, FFN_FIRST_FF_TILE
    assert t % tm == 0 and d_ff % tf == 0 and d_ff % tf0 == 0
    params = pltpu.CompilerParams(dimension_semantics=("parallel", "arbitrary"),
                                  vmem_limit_bytes=V7X_VMEM_LIMIT_BYTES)
    row = lambda i, f: (0, 0)
    first, wg, wu, wd = pl.pallas_call(
        functools.partial(_ffn_first_tile_kernel, final_norm=final_norm),
        grid=(1, d_ff // tf0),
        in_specs=[
            _resident((tm, d), lambda i, f: (0, 0)),
            pl.BlockSpec((1, d), row),
            pl.BlockSpec((d, tf0), lambda i, f: (0, f)),
            pl.BlockSpec((d, tf0), lambda i, f: (0, f)),
            pl.BlockSpec((tf0, d), lambda i, f: (f, 0)),
            pl.BlockSpec((1, d), row),
        ],
        out_specs=[
            pl.BlockSpec((tm, d), lambda i, f: (0, 0)),
            pl.BlockSpec((d, tf0), lambda i, f: (0, f)),
            pl.BlockSpec((d, tf0), lambda i, f: (0, f)),
            pl.BlockSpec((tf0, d), lambda i, f: (f, 0)),
        ],
        out_shape=[
            jax.ShapeDtypeStruct((tm, d), _F32),
            jax.ShapeDtypeStruct(wg32.shape, _BF16),
            jax.ShapeDtypeStruct(wu32.shape, _BF16),
            jax.ShapeDtypeStruct(wd32.shape, _BF16),
        ],
        scratch_shapes=[pltpu.VMEM((tm, d), _BF16)],
        compiler_params=params,
        name=name + "_first",
    )(x2d, gain, wg32, wu32, wd32, fgain)
    n_f = d_ff // tf
    rider_specs = _rider_specs(riders, (t // tm) * n_f, lambda i, f: i * n_f + f)
    outs = pl.pallas_call(
        functools.partial(_ffn_rest_kernel, final_norm=final_norm, n_riders=len(riders)),
        grid=(t // tm, n_f),
        in_specs=[
            pl.BlockSpec((tm, d), lambda i, f: (jnp.maximum(i, 1), 0)),
            pl.BlockSpec((1, d), row),
            pl.BlockSpec((d, tf), lambda i, f: (0, jnp.where(i == 0, 0, f))),
            pl.BlockSpec((d, tf), lambda i, f: (0, jnp.where(i == 0, 0, f))),
            pl.BlockSpec((tf, d), lambda i, f: (jnp.where(i == 0, 0, f), 0)),
            pl.BlockSpec((1, d), row),
            pl.BlockSpec(memory_space=pl.ANY),
        ] + rider_specs,
        out_specs=[pl.BlockSpec((tm, d), lambda i, f: (i, 0))] + rider_specs,
        out_shape=[jax.ShapeDtypeStruct((t, d), _F32)] + [jax.ShapeDtypeStruct(w.shape, _BF16) for w in riders],
        scratch_shapes=[pltpu.VMEM((tm, d), _BF16), pltpu.SemaphoreType.DMA(())],
        compiler_params=params,
        name=name + "_rest",
    )(x2d, gain, wg, wu, wd, fgain, first, *riders)
    return outs[0], outs[1:]


def _ffn_bf16_kernel(x_ref, gain_ref, wg_ref, wu_ref, wd_ref, fgain_ref, out_ref, h_scr, *, final_norm):
    load_weights = lambda: (wg_ref[...], wu_ref[...], wd_ref[...])
    _ffn_step(x_ref, gain_ref, load_weights, fgain_ref, out_ref, h_scr, final_norm)


def _ffn_bf16(x2d, gain, wg, wu, wd, fgain, *, final_norm, name):
    t, d = x2d.shape
    d_ff = wg.shape[1]
    tm, tf = FFN_TOKEN_TILE, FFN_FF_TILE
    assert t % tm == 0 and d_ff % tf == 0
    row = lambda i, f: (0, 0)
    return pl.pallas_call(
        functools.partial(_ffn_bf16_kernel, final_norm=final_norm),
        grid=(t // tm, d_ff // tf),
        in_specs=[
            pl.BlockSpec((tm, d), lambda i, f: (i, 0)),
            pl.BlockSpec((1, d), row),
            pl.BlockSpec((d, tf), lambda i, f: (0, f)),
            pl.BlockSpec((d, tf), lambda i, f: (0, f)),
            pl.BlockSpec((tf, d), lambda i, f: (f, 0)),
            pl.BlockSpec((1, d), row),
        ],
        out_specs=pl.BlockSpec((tm, d), lambda i, f: (i, 0)),
        out_shape=jax.ShapeDtypeStruct((t, d), _F32),
        scratch_shapes=[pltpu.VMEM((tm, d), _BF16)],
        compiler_params=pltpu.CompilerParams(dimension_semantics=("parallel", "arbitrary"),
                                             vmem_limit_bytes=V7X_VMEM_LIMIT_BYTES),
        name=name,
    )(x2d, gain, wg, wu, wd, fgain)


def _in_proj_kernel(x_ref, gain_ref, pos_ref, freq_ref, wqt_ref, wvt_ref, wk_ref, wu_ref,
                    qt_ref, k_ref, vt_ref, u_ref, kmean_ref):
    tm = x_ref.shape[1]
    h = _rms(x_ref[0], gain_ref[...]).astype(_BF16)

    k = _dot(h, wk_ref[...])
    u = _dot(h, wu_ref[...])
    qt = _dot_nt(wqt_ref[...], h)
    vt = _dot_nt(wvt_ref[...], h)

    ang_t = freq_ref[...] * pos_ref[0].astype(_F32)
    cos_t, sin_t = jnp.cos(ang_t), jnp.sin(ang_t)
    pad = HEAD_DIM - ROT_DIM
    cos_n = jnp.transpose(jnp.concatenate([cos_t, cos_t, jnp.ones((pad, tm), _F32)], axis=0))
    sin_n = jnp.transpose(jnp.concatenate([-sin_t, sin_t, jnp.zeros((pad, tm), _F32)], axis=0))
    lane = lax.broadcasted_iota(jnp.int32, (1, HEAD_DIM), 1)

    u_ref[0] = u
    for hh in range(N_ATTN_HEADS):
        kh = k[:, hh * HEAD_DIM:(hh + 1) * HEAD_DIM]
        partner = jnp.where(lane < ROT_HALF,
                            pltpu.roll(kh, HEAD_DIM - ROT_HALF, 1),
                            pltpu.roll(kh, ROT_HALF, 1))
        kr = kh * cos_n + partner * sin_n
        k_ref[0, hh] = kr.astype(_BF16)
        for j in range(tm // MOBA_BLOCK):
            kmean_ref[0, j, :, hh * HEAD_DIM:(hh + 1) * HEAD_DIM] = jnp.mean(
                kr[j * MOBA_BLOCK:(j + 1) * MOBA_BLOCK], axis=0, keepdims=True)

    cos_q, sin_q = cos_t * QUERY_SCALE, sin_t * QUERY_SCALE
    for hh in range(N_ATTN_HEADS):
        base = hh * HEAD_DIM
        x1 = qt[base:base + ROT_HALF]
        x2 = qt[base + ROT_HALF:base + ROT_DIM]
        qt_ref[0, hh, 0:ROT_HALF, :] = (x1 * cos_q - x2 * sin_q).astype(_BF16)
        qt_ref[0, hh, ROT_HALF:ROT_DIM, :] = (x2 * cos_q + x1 * sin_q).astype(_BF16)
        qt_ref[0, hh, ROT_DIM:, :] = (qt[base + ROT_DIM:base + HEAD_DIM] * QUERY_SCALE).astype(_BF16)

    ones = jnp.ones((V_ROWS - HEAD_DIM, tm), _BF16)
    for hh in range(N_ATTN_HEADS):
        vt_ref[0, hh, 0:HEAD_DIM, :] = vt[hh * HEAD_DIM:(hh + 1) * HEAD_DIM].astype(_BF16)
        vt_ref[0, hh, HEAD_DIM:, :] = ones


def _in_proj(x1, gain, positions, freq, w_in, w_in_t):
    b, s, d = x1.shape
    tm = PROJ_TOKEN_TILE
    assert s % tm == 0 and tm % MOBA_BLOCK == 0
    nb = s // MOBA_BLOCK
    bpt = tm // MOBA_BLOCK
    pos_row = positions.reshape(b, 1, s)
    const2 = lambda bi, i: (0, 0)
    return pl.pallas_call(
        _in_proj_kernel,
        grid=(b, s // tm),
        in_specs=[
            pl.BlockSpec((1, tm, d), lambda bi, i: (bi, i, 0)),
            pl.BlockSpec((1, d), const2),
            pl.BlockSpec((1, 1, tm), lambda bi, i: (bi, 0, i)),
            pl.BlockSpec((ROT_HALF, 1), const2),
            _resident((ATTN_WIDTH, d), lambda bi, i: (0, 0)),
            _resident((ATTN_WIDTH, d), lambda bi, i: (2, 0)),
            _resident((d, ATTN_WIDTH), lambda bi, i: (0, 1)),
            _resident((d, POOL_WIDTH), lambda bi, i: (0, 3)),
        ],
        out_specs=[
            pl.BlockSpec((1, N_ATTN_HEADS, HEAD_DIM, tm), lambda bi, i: (bi, 0, 0, i)),
            pl.BlockSpec((1, N_ATTN_HEADS, tm, HEAD_DIM), lambda bi, i: (bi, 0, i, 0)),
            pl.BlockSpec((1, N_ATTN_HEADS, V_ROWS, tm), lambda bi, i: (bi, 0, 0, i)),
            pl.BlockSpec((1, tm, POOL_WIDTH), lambda bi, i: (bi, i, 0)),
            pl.BlockSpec((1, bpt, 1, ATTN_WIDTH), lambda bi, i: (bi, i, 0, 0)),
        ],
        out_shape=[
            jax.ShapeDtypeStruct((b, N_ATTN_HEADS, HEAD_DIM, s), _BF16),
            jax.ShapeDtypeStruct((b, N_ATTN_HEADS, s, HEAD_DIM), _BF16),
            jax.ShapeDtypeStruct((b, N_ATTN_HEADS, V_ROWS, s), _BF16),
            jax.ShapeDtypeStruct((b, s, POOL_WIDTH), _F32),
            jax.ShapeDtypeStruct((b, nb, 1, ATTN_WIDTH), _F32),
        ],
        compiler_params=pltpu.CompilerParams(
            dimension_semantics=("parallel", "parallel"),
            vmem_limit_bytes=V7X_VMEM_LIMIT_BYTES),
        name="in_proj",
    )(x1, gain, pos_row, freq, w_in_t, w_in_t, w_in, w_in)


def _moba_kernel(qt_ref, k_ref, vt_ref, kmean_ref, *rest, nb, hg, n_riders):
    rider_in, rest = rest[:n_riders], rest[n_riders:]
    out_ref, rider_out, rest = rest[0], rest[1:1 + n_riders], rest[1 + n_riders:]
    bias_scr, acc_scr, m_scr, s_even, s_odd, top_even, top_odd = rest

    i = pl.program_id(2)
    even, odd = (s_even, top_even), (s_odd, top_odd)
    blk = MOBA_BLOCK
    kb_id = lax.broadcasted_iota(jnp.int32, (nb, blk), 0)
    past = kb_id < i
    kb_f = kb_id.astype(_F32)

    def gate_scores(h):
        km = kmean_ref[0, :, h * HEAD_DIM:(h + 1) * HEAD_DIM]
        km_hi = km.astype(_BF16)
        km_lo = (km - km_hi.astype(_F32)).astype(_BF16)
        return _dot(km_hi, qt_ref[0, h]) + _dot(km_lo, qt_ref[0, h])

    def select_blocks(h, gate):
        gate = jnp.where(past, gate, NEG_INF)
        bias = jnp.full((nb, blk), NEG_INF, _F32)
        for _ in range(MOBA_TOPK):
            top = jnp.max(gate, axis=0, keepdims=True)
            first = jnp.min(jnp.where(gate == top, kb_f, float(nb)), axis=0, keepdims=True)
            hit = kb_f == first
            bias = jnp.where(hit, 0.0, bias)
            gate = jnp.where(hit, -jnp.inf, gate)
        bias_scr[h] = jnp.where(past, bias, NEG_INF)

    def scores(h, n):
        kblk = k_ref[0, h, pl.ds(pl.multiple_of(n * blk, blk), blk), :]
        return _dot(kblk, qt_ref[0, h])

    def stage(h, n, staged):
        s_buf, top_buf = staged
        s = scores(h, n)
        s_buf[h] = s
        top_buf[h] = jnp.max(s, axis=0, keepdims=True)

    def weighted_values(h, n, p):
        vblk = vt_ref[0, h, :, pl.ds(pl.multiple_of(n * blk, blk), blk)]
        return _dot(vblk, p)

    key_pos = lax.broadcasted_iota(jnp.int32, (blk, blk), 0)
    qry_pos = lax.broadcasted_iota(jnp.int32, (blk, blk), 1)
    causal = key_pos <= qry_pos
    gates = [gate_scores(h) for h in range(hg)]
    ss = [jnp.where(causal, scores(h, i), NEG_INF) for h in range(hg)]
    for h in range(hg):
        select_blocks(h, gates[h])
    for h in range(hg):
        m_own = jnp.max(ss[h], axis=0, keepdims=True)
        m_scr[h] = m_own
        acc_scr[h] = weighted_values(h, i, jnp.exp2(ss[h] - m_own).astype(_BF16))
        stage(h, 0, even)

    def step(n, cur, nxt, stage_next):
        n_nxt = jnp.minimum(n + 1, nb - 1)
        for h in range(hg):
            if stage_next:
                stage(h, n_nxt, nxt)
            s = cur[0][h]
            b_n = bias_scr[h, pl.ds(n, 1), :]
            m_run = m_scr[h]
            m_new = jnp.maximum(m_run, cur[1][h] + b_n)
            m_scr[h] = m_new
            alpha = jnp.exp2(m_run - m_new)
            p = jnp.exp2(s - (m_new - b_n)).astype(_BF16)
            acc_scr[h] = alpha * acc_scr[h] + weighted_values(h, n, p)

    def run(n0, count, stage_last=True):
        for c in range(count):
            bufs = (even, odd) if c % 2 == 0 else (odd, even)
            step(n0 + c, *bufs, stage_next=stage_last or c + 1 < count)

    def body(j, carry):
        run(8 * j, 8)
        return carry

    lax.fori_loop(0, i // 8, body, 0)

    @pl.when(i % 8 >= 4)
    def _():
        run((i // 8) * 8, 4)

    @pl.when(i % 4 >= 2)
    def _():
        run((i // 4) * 4, 2)

    @pl.when(i % 2 == 1)
    def _():
        run(i - 1, 1, stage_last=False)

    _cast_riders(rider_in, rider_out)
    for h in range(hg):
        acc = acc_scr[h]
        out_ref[0, :, h * HEAD_DIM:(h + 1) * HEAD_DIM] = jnp.transpose(
            acc[0:HEAD_DIM] / acc[HEAD_DIM:HEAD_DIM + 1]).astype(out_ref.dtype)


def _moba(qt, k, vt, kmean, riders):
    b, nh, hd, s = qt.shape
    nb = s // MOBA_BLOCK
    hg = MOBA_HEADS_PER_STEP
    assert nh % hg == 0
    groups = nh // hg
    n_steps = b * groups * nb
    rider_specs = _rider_specs(riders, n_steps, lambda bi, g, i: (bi * groups + g) * nb + i)
    outs = pl.pallas_call(
        functools.partial(_moba_kernel, nb=nb, hg=hg, n_riders=len(riders)),
        grid=(b, groups, nb),
        in_specs=[
            pl.BlockSpec((1, hg, hd, MOBA_BLOCK), lambda bi, g, i: (bi, g, 0, i)),
            pl.BlockSpec((1, hg, s, hd), lambda bi, g, i: (bi, g, 0, 0)),
            pl.BlockSpec((1, hg, V_ROWS, s), lambda bi, g, i: (bi, g, 0, 0)),
            pl.BlockSpec((1, nb, hg * hd), lambda bi, g, i: (bi, 0, g)),
        ] + rider_specs,
        out_specs=[pl.BlockSpec((1, MOBA_BLOCK, hg * hd), lambda bi, g, i: (bi, i, g))] + rider_specs,
        out_shape=[jax.ShapeDtypeStruct((b, s, nh * hd), _BF16)]
                  + [jax.ShapeDtypeStruct(w.shape, _BF16) for w in riders],
        scratch_shapes=[pltpu.VMEM((hg, nb, MOBA_BLOCK), _F32),
                        pltpu.VMEM((hg, V_ROWS, MOBA_BLOCK), _F32),
                        pltpu.VMEM((hg, 1, MOBA_BLOCK), _F32),
                        pltpu.VMEM((hg, MOBA_BLOCK, MOBA_BLOCK), _F32),
                        pltpu.VMEM((hg, MOBA_BLOCK, MOBA_BLOCK), _F32),
                        pltpu.VMEM((hg, 1, MOBA_BLOCK), _F32),
                        pltpu.VMEM((hg, 1, MOBA_BLOCK), _F32)],
        compiler_params=pltpu.CompilerParams(
            dimension_semantics=("parallel", "parallel", "arbitrary"),
            vmem_limit_bytes=V7X_VMEM_LIMIT_BYTES),
        name="moba",
    )(qt, k, vt, kmean, *riders)
    return outs[0], outs[1:]


def _mix_out_kernel(x_ref, attn_ref, u_ref, halo_ref, pw_ref, ps_ref, wo_ref, out_ref, ubuf, lhs_buf):
    tm = x_ref.shape[1]
    si = pl.program_id(1)
    ubuf[0:POOL_HALO, :] = jnp.where(si > 0, halo_ref[0], 0.0)
    ubuf[POOL_HALO:, :] = u_ref[0]

    lhs_buf[:, 0:ATTN_WIDTH] = attn_ref[0]
    n_blocks = tm // MIX_ROW_BLOCK
    n_groups = len(POOL_WINDOWS)

    def pool_group(r, g):
        w = POOL_WINDOWS[g]
        row0 = r * MIX_ROW_BLOCK
        cols = slice(g * POOL_GROUP_DIM, (g + 1) * POOL_GROUP_DIM)
        t = si * tm + row0 + lax.broadcasted_iota(jnp.int32, (MIX_ROW_BLOCK, 1), 0)
        ext = ubuf[row0:row0 + POOL_HALO + MIX_ROW_BLOCK, cols]
        wsum = ext
        span = 1
        while span < w:
            wsum = wsum + pltpu.roll(wsum, span, 0)
            span *= 2
        own = ext[POOL_HALO:]
        count = jnp.minimum(t + 1, w).astype(_F32)
        pooled = wsum[POOL_HALO:] / count - own
        y = _dot(pooled.astype(_BF16), pw_ref[g]) * ps_ref[:, cols]
        lhs_buf[row0:row0 + MIX_ROW_BLOCK, ATTN_WIDTH + g * POOL_GROUP_DIM:ATTN_WIDTH + (g + 1) * POOL_GROUP_DIM] = (
            y.astype(_BF16))

    for g in range(n_groups):
        pool_group(0, g)
    n_col_blocks = out_ref.shape[2] // MIX_COL_BLOCK
    for r in range(n_blocks):
        rows = slice(r * MIX_ROW_BLOCK, (r + 1) * MIX_ROW_BLOCK)
        lhs = lhs_buf[rows, :]
        for c in range(n_col_blocks):
            cols = slice(c * MIX_COL_BLOCK, (c + 1) * MIX_COL_BLOCK)
            out_ref[0, rows, cols] = x_ref[0, rows, cols] + _dot(lhs, wo_ref[:, cols])
            if r + 1 < n_blocks and c < n_groups:
                pool_group(r + 1, c)


def _mix_out(x1, attn, u, pool_w, pool_scale, w_out):
    b, s, d = x1.shape
    tm = MIX_TOKEN_TILE
    assert s % tm == 0 and tm % POOL_HALO == 0
    assert all(w <= POOL_HALO and w & (w - 1) == 0 for w in POOL_WINDOWS)
    halo_per_tile = tm // POOL_HALO
    const2 = lambda bi, i: (0, 0)
    return pl.pallas_call(
        _mix_out_kernel,
        grid=(b, s // tm),
        in_specs=[
            pl.BlockSpec((1, tm, d), lambda bi, i: (bi, i, 0)),
            pl.BlockSpec((1, tm, ATTN_WIDTH), lambda bi, i: (bi, i, 0)),
            pl.BlockSpec((1, tm, POOL_WIDTH), lambda bi, i: (bi, i, 0)),
            pl.BlockSpec((1, POOL_HALO, POOL_WIDTH),
                         lambda bi, i: (bi, jnp.maximum(i * halo_per_tile - 1, 0), 0)),
            _resident(pool_w.shape, lambda bi, i: (0, 0, 0)),
            pl.BlockSpec((1, POOL_WIDTH), const2),
            _resident(w_out.shape, const2),
        ],
        out_specs=pl.BlockSpec((1, tm, d), lambda bi, i: (bi, i, 0)),
        out_shape=jax.ShapeDtypeStruct((b, s, d), _F32),
        scratch_shapes=[pltpu.VMEM((POOL_HALO + tm, POOL_WIDTH), _F32), pltpu.VMEM((tm, ATTN_WIDTH + POOL_WIDTH), _BF16)],
        compiler_params=pltpu.CompilerParams(
            dimension_semantics=("parallel", "parallel"),
            vmem_limit_bytes=V7X_VMEM_LIMIT_BYTES),
        name="mix_out",
    )(x1, attn, u, u, pool_w, pool_scale, w_out)


def kernel(x, positions, norm_ffn1, w1_gate, w1_up, w1_down, norm_mix, w_in, pool_w, pool_scale,
           w_out, norm_ffn2, w2_gate, w2_up, w2_down, norm_final):
    b, s, d = x.shape
    depth = w_in.shape[0]
    inv_freq = ROPE_THETA ** (-jnp.arange(0, ROT_DIM, 2, dtype=_F32) / ROT_DIM)
    freq = inv_freq[:, None]
    fgain = norm_final[None, :]

    for l in range(depth):
        w_in_b = w_in[l].astype(_BF16)
        w_in_t = w_in[l].T.astype(_BF16)
        x1, _ = _ffn(x.reshape(b * s, d), norm_ffn1[l][None, :], w1_gate[l], w1_up[l], w1_down[l],
                     fgain, [], final_norm=False, name="ffn_pre")
        x1 = x1.reshape(b, s, d)
        qt, k, vt, u, kmean = _in_proj(x1, norm_mix[l][None, :], positions, freq, w_in_b, w_in_t)
        riders = [w2_gate[l], w2_up[l], w2_down[l], w_out[l], pool_w[l].reshape(-1, POOL_GROUP_DIM)]
        attn, (wg2, wu2, wd2, wo, pw) = _moba(qt, k, vt, kmean.reshape(b, s // MOBA_BLOCK, ATTN_WIDTH), riders)
        x2 = _mix_out(x1, attn, u, pw.reshape(pool_w.shape[1:]), pool_scale[l][None, :], wo)
        last = l == depth - 1
        x = _ffn_bf16(x2.reshape(b * s, d), norm_ffn2[l][None, :], wg2, wu2, wd2,
                      fgain, final_norm=last, name="ffn_post").reshape(b, s, d)
    return x
```

```python
import functools

import jax
import jax.numpy as jnp
from jax import lax
from jax.experimental import pallas as pl
from jax.experimental.pallas import tpu as pltpu

N_ATTN_HEADS = 8
HEAD_DIM = 128
ATTN_WIDTH = N_ATTN_HEADS * HEAD_DIM
POOL_WINDOWS = (2, 4, 8, 16)
POOL_GROUP_DIM = 256
POOL_WIDTH = POOL_GROUP_DIM * len(POOL_WINDOWS)
MOBA_BLOCK = 256
MOBA_TOPK = 3
ROT_DIM = HEAD_DIM // 4
ROT_HALF = ROT_DIM // 2
ROPE_THETA = 500000.0
EPS = 1e-6
NEG_INF = -1e30
LOG2_E = 1.4426950408889634
QUERY_SCALE = (HEAD_DIM ** -0.5) * LOG2_E
V7X_BF16_ROWS_PER_VREG = 16
V_ROWS = HEAD_DIM + V7X_BF16_ROWS_PER_VREG
POOL_HALO = 16

V7X_VMEM_LIMIT_BYTES = 58 * 1024 * 1024

FFN_TOKEN_TILE = 1024
FFN_FF_TILE = 512
FFN_FIRST_FF_TILE = 256
FIRST_RING = 3
PROJ_TOKEN_TILE = 512
MIX_TOKEN_TILE = 512
MIX_ROW_BLOCK = 256
MIX_COL_BLOCK = 256
MOBA_HEADS_PER_STEP = 8

_BF16 = jnp.bfloat16
_F32 = jnp.float32


def _rms(x, gain):
    inv = lax.rsqrt(jnp.mean(x * x, axis=-1, keepdims=True) + EPS)
    return x * inv * gain


def _dot(a, b):
    return jnp.dot(a, b, preferred_element_type=_F32)


def _dot_nt(a, b):
    return lax.dot_general(a, b, (((1,), (1,)), ((), ())), preferred_element_type=_F32)


def _resident(block_shape, index_map):
    return pl.BlockSpec(block_shape, index_map, pipeline_mode=pl.Buffered(1))


def _cast_riders(rider_in, rider_out):
    for src_ref, dst_ref in zip(rider_in, rider_out):
        dst_ref[...] = src_ref[...].astype(_BF16)


def _rider_specs(riders, n_steps, step_of):
    specs = []
    for w in riders:
        n_slices = next(n for n in range(n_steps, 0, -1)
                        if w.shape[0] % n == 0 and (w.shape[0] // n) % V7X_BF16_ROWS_PER_VREG == 0)
        specs.append(pl.BlockSpec(
            (w.shape[0] // n_slices, w.shape[1]),
            lambda *ids, n_slices=n_slices: (jnp.minimum(step_of(*ids), n_slices - 1), 0)))
    return specs


def _ffn_step(x_ref, gain_ref, load_weights, fgain_ref, out_ref, h_scr, final_norm):
    f = pl.program_id(1)

    @pl.when(f == 0)
    def _():
        x = x_ref[...]
        h_scr[...] = _rms(x, gain_ref[...]).astype(_BF16)
        out_ref[...] = x

    wg, wu, wd = load_weights()
    h = h_scr[...]
    g = _dot(h, wg)
    u = _dot(h, wu)
    g_half = 0.5 * g
    a = ((g_half + g_half * jnp.tanh(g_half)) * (0.5 * u)).astype(_BF16)
    out_ref[...] += _dot(a, wd)

    if final_norm:
        @pl.when(f == pl.num_programs(1) - 1)
        def _():
            out_ref[...] = _rms(out_ref[...], fgain_ref[...])


def _ffn_first_tile_kernel(x_ref, gain_ref, wg32_hbm, wu32_hbm, wd32_hbm, fgain_ref,
                           out_ref, wg_ref, wu_ref, wd_ref, h_scr, gbuf, ubuf, dbuf, gsem, usem, dsem,
                           *, final_norm):
    f = pl.program_id(1)
    n = pl.num_programs(1)
    tf0 = gbuf.shape[2]

    def copies(c, slot):
        cols = pl.ds(pl.multiple_of(c * tf0, tf0), tf0)
        return (pltpu.make_async_copy(wg32_hbm.at[:, cols], gbuf.at[slot], gsem.at[slot]),
                pltpu.make_async_copy(wu32_hbm.at[:, cols], ubuf.at[slot], usem.at[slot]),
                pltpu.make_async_copy(wd32_hbm.at[cols, :], dbuf.at[slot], dsem.at[slot]))

    @pl.when(f == 0)
    def _():
        for c in range(FIRST_RING - 1):
            for cp in copies(c, c):
                cp.start()

    @pl.when(f + FIRST_RING - 1 < n)
    def _():
        for cp in copies(f + FIRST_RING - 1, (f + FIRST_RING - 1) % FIRST_RING):
            cp.start()

    slot = f % FIRST_RING
    for cp in copies(f, slot):
        cp.wait()

    def load_weights():
        wg = gbuf[slot].astype(_BF16)
        wu = ubuf[slot].astype(_BF16)
        wd = dbuf[slot].astype(_BF16)
        wg_ref[...] = wg
        wu_ref[...] = wu
        wd_ref[...] = wd
        return wg, wu, wd

    _ffn_step(x_ref, gain_ref, load_weights, fgain_ref, out_ref, h_scr, final_norm)


def _ffn_rest_kernel(x_ref, gain_ref, wg_ref, wu_ref, wd_ref, fgain_ref, first_hbm, *rest, final_norm, n_riders):
    rider_in, rest = rest[:n_riders], rest[n_riders:]
    out_ref, rider_out, (h_scr, copy_sem) = rest[0], rest[1:1 + n_riders], rest[1 + n_riders:]
    _cast_riders(rider_in, rider_out)
    i = pl.program_id(0)

    @pl.when((i == 0) & (pl.program_id(1) == 0))
    def _():
        copy = pltpu.make_async_copy(first_hbm, out_ref, copy_sem)
        copy.start()
        copy.wait()

    @pl.when(i > 0)
    def _():
        load_weights = lambda: (wg_ref[...], wu_ref[...], wd_ref[...])
        _ffn_step(x_ref, gain_ref, load_weights, fgain_ref, out_ref, h_scr, final_norm)


def _ffn(x2d, gain, wg32, wu32, wd32, fgain, riders, *, final_norm, name):
    t, d = x2d.shape
    d_ff = wg32.shape[1]
    tm, tf, tf0 = FFN_TOKEN_TILE, FFN_FF_TILE, FFN_FIRST_FF_TILE
    assert t % tm == 0 and d_ff % tf == 0 and d_ff % tf0 == 0
    params = pltpu.CompilerParams(dimension_semantics=("parallel", "arbitrary"),
                                  vmem_limit_bytes=V7X_VMEM_LIMIT_BYTES)
    row = lambda i, f: (0, 0)
    first, wg, wu, wd = pl.pallas_call(
        functools.partial(_ffn_first_tile_kernel, final_norm=final_norm),
        grid=(1, d_ff // tf0),
        in_specs=[
            _resident((tm, d), lambda i, f: (0, 0)),
            pl.BlockSpec((1, d), row),
            pl.BlockSpec(memory_space=pl.ANY),
            pl.BlockSpec(memory_space=pl.ANY),
            pl.BlockSpec(memory_space=pl.ANY),
            pl.BlockSpec((1, d), row),
        ],
        out_specs=[
            pl.BlockSpec((tm, d), lambda i, f: (0, 0)),
            pl.BlockSpec((d, tf0), lambda i, f: (0, f)),
            pl.BlockSpec((d, tf0), lambda i, f: (0, f)),
            pl.BlockSpec((tf0, d), lambda i, f: (f, 0)),
        ],
        out_shape=[
            jax.ShapeDtypeStruct((tm, d), _F32),
            jax.ShapeDtypeStruct(wg32.shape, _BF16),
            jax.ShapeDtypeStruct(wu32.shape, _BF16),
            jax.ShapeDtypeStruct(wd32.shape, _BF16),
        ],
        scratch_shapes=[pltpu.VMEM((tm, d), _BF16),
                        pltpu.VMEM((FIRST_RING, d, tf0), _F32), pltpu.VMEM((FIRST_RING, d, tf0), _F32),
                        pltpu.VMEM((FIRST_RING, tf0, d), _F32),
                        pltpu.SemaphoreType.DMA((FIRST_RING,)), pltpu.SemaphoreType.DMA((FIRST_RING,)),
                        pltpu.SemaphoreType.DMA((FIRST_RING,))],
        compiler_params=pltpu.CompilerParams(dimension_semantics=("arbitrary", "arbitrary"),
                                             vmem_limit_bytes=V7X_VMEM_LIMIT_BYTES),
        name=name + "_first",
    )(x2d, gain, wg32, wu32, wd32, fgain)
    n_f = d_ff // tf
    rider_specs = _rider_specs(riders, (t // tm) * n_f, lambda i, f: i * n_f + f)
    outs = pl.pallas_call(
        functools.partial(_ffn_rest_kernel, final_norm=final_norm, n_riders=len(riders)),
        grid=(t // tm, n_f),
        in_specs=[
            pl.BlockSpec((tm, d), lambda i, f: (jnp.maximum(i, 1), 0)),
            pl.BlockSpec((1, d), row),
            pl.BlockSpec((d, tf), lambda i, f: (0, jnp.where(i == 0, 0, f))),
            pl.BlockSpec((d, tf), lambda i, f: (0, jnp.where(i == 0, 0, f))),
            pl.BlockSpec((tf, d), lambda i, f: (jnp.where(i == 0, 0, f), 0)),
            pl.BlockSpec((1, d), row),
            pl.BlockSpec(memory_space=pl.ANY),
        ] + rider_specs,
        out_specs=[pl.BlockSpec((tm, d), lambda i, f: (i, 0))] + rider_specs,
        out_shape=[jax.ShapeDtypeStruct((t, d), _F32)] + [jax.ShapeDtypeStruct(w.shape, _BF16) for w in riders],
        scratch_shapes=[pltpu.VMEM((tm, d), _BF16), pltpu.SemaphoreType.DMA(())],
        compiler_params=params,
        name=name + "_rest",
    )(x2d, gain, wg, wu, wd, fgain, first, *riders)
    return outs[0], outs[1:]


def _ffn_bf16_kernel(x_ref, gain_ref, wg_ref, wu_ref, wd_ref, fgain_ref, out_ref, h_scr, *, final_norm):
    load_weights = lambda: (wg_ref[...], wu_ref[...], wd_ref[...])
    _ffn_step(x_ref, gain_ref, load_weights, fgain_ref, out_ref, h_scr, final_norm)


def _ffn_bf16(x2d, gain, wg, wu, wd, fgain, *, final_norm, name):
    t, d = x2d.shape
    d_ff = wg.shape[1]
    tm, tf = FFN_TOKEN_TILE, FFN_FF_TILE
    assert t % tm == 0 and d_ff % tf == 0
    row = lambda i, f: (0, 0)
    return pl.pallas_call(
        functools.partial(_ffn_bf16_kernel, final_norm=final_norm),
        grid=(t // tm, d_ff // tf),
        in_specs=[
            pl.BlockSpec((tm, d), lambda i, f: (i, 0)),
            pl.BlockSpec((1, d), row),
            pl.BlockSpec((d, tf), lambda i, f: (0, f)),
            pl.BlockSpec((d, tf), lambda i, f: (0, f)),
            pl.BlockSpec((tf, d), lambda i, f: (f, 0)),
            pl.BlockSpec((1, d), row),
        ],
        out_specs=pl.BlockSpec((tm, d), lambda i, f: (i, 0)),
        out_shape=jax.ShapeDtypeStruct((t, d), _F32),
        scratch_shapes=[pltpu.VMEM((tm, d), _BF16)],
        compiler_params=pltpu.CompilerParams(dimension_semantics=("parallel", "arbitrary"),
                                             vmem_limit_bytes=V7X_VMEM_LIMIT_BYTES),
        name=name,
    )(x2d, gain, wg, wu, wd, fgain)


def _in_proj_kernel(x_ref, gain_ref, pos_ref, freq_ref, wqt_ref, wvt_ref, wk_ref, wu_ref,
                    qt_ref, k_ref, vt_ref, u_ref, kmean_ref):
    tm = x_ref.shape[1]
    h = _rms(x_ref[0], gain_ref[...]).astype(_BF16)

    k = _dot(h, wk_ref[...])
    u = _dot(h, wu_ref[...])
    qt = _dot_nt(wqt_ref[...], h)
    vt = _dot_nt(wvt_ref[...], h)

    ang_t = freq_ref[...] * pos_ref[0].astype(_F32)
    cos_t, sin_t = jnp.cos(ang_t), jnp.sin(ang_t)
    pad = HEAD_DIM - ROT_DIM
    cos_n = jnp.transpose(jnp.concatenate([cos_t, cos_t, jnp.ones((pad, tm), _F32)], axis=0))
    sin_n = jnp.transpose(jnp.concatenate([-sin_t, sin_t, jnp.zeros((pad, tm), _F32)], axis=0))
    lane = lax.broadcasted_iota(jnp.int32, (1, HEAD_DIM), 1)

    u_ref[0] = u
    for hh in range(N_ATTN_HEADS):
        kh = k[:, hh * HEAD_DIM:(hh + 1) * HEAD_DIM]
        partner = jnp.where(lane < ROT_HALF,
                            pltpu.roll(kh, HEAD_DIM - ROT_HALF, 1),
                            pltpu.roll(kh, ROT_HALF, 1))
        kr = kh * cos_n + partner * sin_n
        k_ref[0, hh] = kr.astype(_BF16)
        for j in range(tm // MOBA_BLOCK):
            kmean_ref[0, j, :, hh * HEAD_DIM:(hh + 1) * HEAD_DIM] = jnp.mean(
                kr[j * MOBA_BLOCK:(j + 1) * MOBA_BLOCK], axis=0, keepdims=True)

    cos_q, sin_q = cos_t * QUERY_SCALE, sin_t * QUERY_SCALE
    for hh in range(N_ATTN_HEADS):
        base = hh * HEAD_DIM
        x1 = qt[base:base + ROT_HALF]
        x2 = qt[base + ROT_HALF:base + ROT_DIM]
        qt_ref[0, hh, 0:ROT_HALF, :] = (x1 * cos_q - x2 * sin_q).astype(_BF16)
        qt_ref[0, hh, ROT_HALF:ROT_DIM, :] = (x2 * cos_q + x1 * sin_q).astype(_BF16)
        qt_ref[0, hh, ROT_DIM:, :] = (qt[base + ROT_DIM:base + HEAD_DIM] * QUERY_SCALE).astype(_BF16)

    ones = jnp.ones((V_ROWS - HEAD_DIM, tm), _BF16)
    for hh in range(N_ATTN_HEADS):
        vt_ref[0, hh, 0:HEAD_DIM, :] = vt[hh * HEAD_DIM:(hh + 1) * HEAD_DIM].astype(_BF16)
        vt_ref[0, hh, HEAD_DIM:, :] = ones


def _in_proj(x1, gain, positions, freq, w_in, w_in_t):
    b, s, d = x1.shape
    tm = PROJ_TOKEN_TILE
    assert s % tm == 0 and tm % MOBA_BLOCK == 0
    nb = s // MOBA_BLOCK
    bpt = tm // MOBA_BLOCK
    pos_row = positions.reshape(b, 1, s)
    const2 = lambda bi, i: (0, 0)
    return pl.pallas_call(
        _in_proj_kernel,
        grid=(b, s // tm),
        in_specs=[
            pl.BlockSpec((1, tm, d), lambda bi, i: (bi, i, 0)),
            pl.BlockSpec((1, d), const2),
            pl.BlockSpec((1, 1, tm), lambda bi, i: (bi, 0, i)),
            pl.BlockSpec((ROT_HALF, 1), const2),
            _resident((ATTN_WIDTH, d), lambda bi, i: (0, 0)),
            _resident((ATTN_WIDTH, d), lambda bi, i: (2, 0)),
            _resident((d, ATTN_WIDTH), lambda bi, i: (0, 1)),
            _resident((d, POOL_WIDTH), lambda bi, i: (0, 3)),
        ],
        out_specs=[
            pl.BlockSpec((1, N_ATTN_HEADS, HEAD_DIM, tm), lambda bi, i: (bi, 0, 0, i)),
            pl.BlockSpec((1, N_ATTN_HEADS, tm, HEAD_DIM), lambda bi, i: (bi, 0, i, 0)),
            pl.BlockSpec((1, N_ATTN_HEADS, V_ROWS, tm), lambda bi, i: (bi, 0, 0, i)),
            pl.BlockSpec((1, tm, POOL_WIDTH), lambda bi, i: (bi, i, 0)),
            pl.BlockSpec((1, bpt, 1, ATTN_WIDTH), lambda bi, i: (bi, i, 0, 0)),
        ],
        out_shape=[
            jax.ShapeDtypeStruct((b, N_ATTN_HEADS, HEAD_DIM, s), _BF16),
            jax.ShapeDtypeStruct((b, N_ATTN_HEADS, s, HEAD_DIM), _BF16),
            jax.ShapeDtypeStruct((b, N_ATTN_HEADS, V_ROWS, s), _BF16),
            jax.ShapeDtypeStruct((b, s, POOL_WIDTH), _F32),
            jax.ShapeDtypeStruct((b, nb, 1, ATTN_WIDTH), _F32),
        ],
        compiler_params=pltpu.CompilerParams(
            dimension_semantics=("parallel", "parallel"),
            vmem_limit_bytes=V7X_VMEM_LIMIT_BYTES),
        name="in_proj",
    )(x1, gain, pos_row, freq, w_in_t, w_in_t, w_in, w_in)


def _moba_kernel(qt_ref, k_ref, vt_ref, kmean_ref, *rest, nb, hg, n_riders):
    rider_in, rest = rest[:n_riders], rest[n_riders:]
    out_ref, rider_out, rest = rest[0], rest[1:1 + n_riders], rest[1 + n_riders:]
    bias_scr, acc_scr, m_scr, s_even, s_odd, top_even, top_odd = rest

    i = pl.program_id(2)
    even, odd = (s_even, top_even), (s_odd, top_odd)
    blk = MOBA_BLOCK
    kb_id = lax.broadcasted_iota(jnp.int32, (nb, blk), 0)
    past = kb_id < i
    kb_f = kb_id.astype(_F32)

    def gate_scores(h):
        km = kmean_ref[0, :, h * HEAD_DIM:(h + 1) * HEAD_DIM]
        km_hi = km.astype(_BF16)
        km_lo = (km - km_hi.astype(_F32)).astype(_BF16)
        return _dot(km_hi, qt_ref[0, h]) + _dot(km_lo, qt_ref[0, h])

    def select_blocks(h, gate):
        gate = jnp.where(past, gate, NEG_INF)
        bias = jnp.full((nb, blk), NEG_INF, _F32)
        for _ in range(MOBA_TOPK):
            top = jnp.max(gate, axis=0, keepdims=True)
            first = jnp.min(jnp.where(gate == top, kb_f, float(nb)), axis=0, keepdims=True)
            hit = kb_f == first
            bias = jnp.where(hit, 0.0, bias)
            gate = jnp.where(hit, -jnp.inf, gate)
        bias_scr[h] = jnp.where(past, bias, NEG_INF)

    def scores(h, n):
        kblk = k_ref[0, h, pl.ds(pl.multiple_of(n * blk, blk), blk), :]
        return _dot(kblk, qt_ref[0, h])

    def stage(h, n, staged):
        s_buf, top_buf = staged
        s = scores(h, n)
        s_buf[h] = s
        top_buf[h] = jnp.max(s, axis=0, keepdims=True)

    def weighted_values(h, n, p):
        vblk = vt_ref[0, h, :, pl.ds(pl.multiple_of(n * blk, blk), blk)]
        return _dot(vblk, p)

    key_pos = lax.broadcasted_iota(jnp.int32, (blk, blk), 0)
    qry_pos = lax.broadcasted_iota(jnp.int32, (blk, blk), 1)
    causal = key_pos <= qry_pos
    gates = [gate_scores(h) for h in range(hg)]
    ss = [jnp.where(causal, scores(h, i), NEG_INF) for h in range(hg)]
    for h in range(hg):
        select_blocks(h, gates[h])
    for h in range(hg):
        m_own = jnp.max(ss[h], axis=0, keepdims=True)
        m_scr[h] = m_own
        acc_scr[h] = weighted_values(h, i, jnp.exp2(ss[h] - m_own).astype(_BF16))
        stage(h, 0, even)

    def step(n, cur, nxt, stage_next):
        n_nxt = jnp.minimum(n + 1, nb - 1)
        for h in range(hg):
            if stage_next:
                stage(h, n_nxt, nxt)
            s = cur[0][h]
            b_n = bias_scr[h, pl.ds(n, 1), :]
            m_run = m_scr[h]
            m_new = jnp.maximum(m_run, cur[1][h] + b_n)
            m_scr[h] = m_new
            alpha = jnp.exp2(m_run - m_new)
            p = jnp.exp2(s - (m_new - b_n)).astype(_BF16)
            acc_scr[h] = alpha * acc_scr[h] + weighted_values(h, n, p)

    def run(n0, count, stage_last=True):
        for c in range(count):
            bufs = (even, odd) if c % 2 == 0 else (odd, even)
            step(n0 + c, *bufs, stage_next=stage_last or c + 1 < count)

    def body(j, carry):
        run(8 * j, 8)
        return carry

    lax.fori_loop(0, i // 8, body, 0)

    @pl.when(i % 8 >= 4)
    def _():
        run((i // 8) * 8, 4)

    @pl.when(i % 4 >= 2)
    def _():
        run((i // 4) * 4, 2)

    @pl.when(i % 2 == 1)
    def _():
        run(i - 1, 1, stage_last=False)

    _cast_riders(rider_in, rider_out)
    for h in range(hg):
        acc = acc_scr[h]
        out_ref[0, :, h * HEAD_DIM:(h + 1) * HEAD_DIM] = jnp.transpose(
            acc[0:HEAD_DIM] / acc[HEAD_DIM:HEAD_DIM + 1]).astype(out_ref.dtype)


def _moba(qt, k, vt, kmean, riders):
    b, nh, hd, s = qt.shape
    nb = s // MOBA_BLOCK
    hg = MOBA_HEADS_PER_STEP
    assert nh % hg == 0
    groups = nh // hg
    n_steps = b * groups * nb
    rider_specs = _rider_specs(riders, n_steps, lambda bi, g, i: (bi * groups + g) * nb + i)
    outs = pl.pallas_call(
        functools.partial(_moba_kernel, nb=nb, hg=hg, n_riders=len(riders)),
        grid=(b, groups, nb),
        in_specs=[
            pl.BlockSpec((1, hg, hd, MOBA_BLOCK), lambda bi, g, i: (bi, g, 0, i)),
            pl.BlockSpec((1, hg, s, hd), lambda bi, g, i: (bi, g, 0, 0)),
            pl.BlockSpec((1, hg, V_ROWS, s), lambda bi, g, i: (bi, g, 0, 0)),
            pl.BlockSpec((1, nb, hg * hd), lambda bi, g, i: (bi, 0, g)),
        ] + rider_specs,
        out_specs=[pl.BlockSpec((1, MOBA_BLOCK, hg * hd), lambda bi, g, i: (bi, i, g))] + rider_specs,
        out_shape=[jax.ShapeDtypeStruct((b, s, nh * hd), _BF16)]
                  + [jax.ShapeDtypeStruct(w.shape, _BF16) for w in riders],
        scratch_shapes=[pltpu.VMEM((hg, nb, MOBA_BLOCK), _F32),
                        pltpu.VMEM((hg, V_ROWS, MOBA_BLOCK), _F32),
                        pltpu.VMEM((hg, 1, MOBA_BLOCK), _F32),
                        pltpu.VMEM((hg, MOBA_BLOCK, MOBA_BLOCK), _F32),
                        pltpu.VMEM((hg, MOBA_BLOCK, MOBA_BLOCK), _F32),
                        pltpu.VMEM((hg, 1, MOBA_BLOCK), _F32),
                        pltpu.VMEM((hg, 1, MOBA_BLOCK), _F32)],
        compiler_params=pltpu.CompilerParams(
            dimension_semantics=("parallel", "parallel", "arbitrary"),
            vmem_limit_bytes=V7X_VMEM_LIMIT_BYTES),
        name="moba",
    )(qt, k, vt, kmean, *riders)
    return outs[0], outs[1:]


def _mix_out_kernel(x_ref, attn_ref, u_ref, halo_ref, pw_ref, ps_ref, wo_ref, out_ref, ubuf, lhs_buf):
    tm = x_ref.shape[1]
    si = pl.program_id(1)
    ubuf[0:POOL_HALO, :] = jnp.where(si > 0, halo_ref[0], 0.0)
    ubuf[POOL_HALO:, :] = u_ref[0]

    lhs_buf[:, 0:ATTN_WIDTH] = attn_ref[0]
    n_blocks = tm // MIX_ROW_BLOCK
    n_groups = len(POOL_WINDOWS)

    def pool_group(r, g):
        w = POOL_WINDOWS[g]
        row0 = r * MIX_ROW_BLOCK
        cols = slice(g * POOL_GROUP_DIM, (g + 1) * POOL_GROUP_DIM)
        t = si * tm + row0 + lax.broadcasted_iota(jnp.int32, (MIX_ROW_BLOCK, 1), 0)
        ext = ubuf[row0:row0 + POOL_HALO + MIX_ROW_BLOCK, cols]
        wsum = ext
        span = 1
        while span < w:
            wsum = wsum + pltpu.roll(wsum, span, 0)
            span *= 2
        own = ext[POOL_HALO:]
        count = jnp.minimum(t + 1, w).astype(_F32)
        pooled = wsum[POOL_HALO:] / count - own
        y = _dot(pooled.astype(_BF16), pw_ref[g]) * ps_ref[:, cols]
        lhs_buf[row0:row0 + MIX_ROW_BLOCK, ATTN_WIDTH + g * POOL_GROUP_DIM:ATTN_WIDTH + (g + 1) * POOL_GROUP_DIM] = (
            y.astype(_BF16))

    for g in range(n_groups):
        pool_group(0, g)
    n_col_blocks = out_ref.shape[2] // MIX_COL_BLOCK
    for r in range(n_blocks):
        rows = slice(r * MIX_ROW_BLOCK, (r + 1) * MIX_ROW_BLOCK)
        lhs = lhs_buf[rows, :]
        for c in range(n_col_blocks):
            cols = slice(c * MIX_COL_BLOCK, (c + 1) * MIX_COL_BLOCK)
            out_ref[0, rows, cols] = x_ref[0, rows, cols] + _dot(lhs, wo_ref[:, cols])
            if r + 1 < n_blocks and c < n_groups:
                pool_group(r + 1, c)


def _mix_out(x1, attn, u, pool_w, pool_scale, w_out):
    b, s, d = x1.shape
    tm = MIX_TOKEN_TILE
    assert s % tm == 0 and tm % POOL_HALO == 0
    assert all(w <= POOL_HALO and w & (w - 1) == 0 for w in POOL_WINDOWS)
    halo_per_tile = tm // POOL_HALO
    const2 = lambda bi, i: (0, 0)
    return pl.pallas_call(
        _mix_out_kernel,
        grid=(b, s // tm),
        in_specs=[
            pl.BlockSpec((1, tm, d), lambda bi, i: (bi, i, 0)),
            pl.BlockSpec((1, tm, ATTN_WIDTH), lambda bi, i: (bi, i, 0)),
            pl.BlockSpec((1, tm, POOL_WIDTH), lambda bi, i: (bi, i, 0)),
            pl.BlockSpec((1, POOL_HALO, POOL_WIDTH),
                         lambda bi, i: (bi, jnp.maximum(i * halo_per_tile - 1, 0), 0)),
            _resident(pool_w.shape, lambda bi, i: (0, 0, 0)),
            pl.BlockSpec((1, POOL_WIDTH), const2),
            _resident(w_out.shape, const2),
        ],
        out_specs=pl.BlockSpec((1, tm, d), lambda bi, i: (bi, i, 0)),
        out_shape=jax.ShapeDtypeStruct((b, s, d), _F32),
        scratch_shapes=[pltpu.VMEM((POOL_HALO + tm, POOL_WIDTH), _F32), pltpu.VMEM((tm, ATTN_WIDTH + POOL_WIDTH), _BF16)],
        compiler_params=pltpu.CompilerParams(
            dimension_semantics=("parallel", "parallel"),
            vmem_limit_bytes=V7X_VMEM_LIMIT_BYTES),
        name="mix_out",
    )(x1, attn, u, u, pool_w, pool_scale, w_out)


def kernel(x, positions, norm_ffn1, w1_gate, w1_up, w1_down, norm_mix, w_in, pool_w, pool_scale,
           w_out, norm_ffn2, w2_gate, w2_up, w2_down, norm_final):
    b, s, d = x.shape
    depth = w_in.shape[0]
    inv_freq = ROPE_THETA ** (-jnp.arange(0, ROT_DIM, 2, dtype=_F32) / ROT_DIM)
    freq = inv_freq[:, None]
    fgain = norm_final[None, :]

    for l in range(depth):
        w_in_b = w_in[l].astype(_BF16)
        w_in_t = w_in_b.T
        x1, _ = _ffn(x.reshape(b * s, d), norm_ffn1[l][None, :], w1_gate[l], w1_up[l], w1_down[l],
                     fgain, [], final_norm=False, name="ffn_pre")
        x1 = x1.reshape(b, s, d)
        qt, k, vt, u, kmean = _in_proj(x1, norm_mix[l][None, :], positions, freq, w_in_b, w_in_t)
        riders = [w2_gate[l], w2_up[l], w2_down[l], w_out[l], pool_w[l].reshape(-1, POOL_GROUP_DIM)]
        attn, (wg2, wu2, wd2, wo, pw) = _moba(qt, k, vt, kmean.reshape(b, s // MOBA_BLOCK, ATTN_WIDTH), riders)
        x2 = _mix_out(x1, attn, u, pw.reshape(pool_w.shape[1:]), pool_scale[l][None, :], wo)
        last = l == depth - 1
        x = _ffn_bf16(x2.reshape(b * s, d), norm_ffn2[l][None, :], wg2, wu2, wd2,
                      fgain, final_norm=last, name="ffn_post").reshape(b, s, d)
    return x
```
